```python
import math
import jax, jax.numpy as jnp
from jax import lax
import numpy as np

D_MODEL = 1024
BATCH = 2
SEQ = 8192
DEPTH = 2
DEC_BATCH = 32
DEC_SEQ = 1
PAST_LEN = 16384
PAGE_SIZE = 128

N_A_LAYERS = DEPTH // 2
N_B_LAYERS = DEPTH - N_A_LAYERS
S5_GROUP = 16
S5_GROUPS = D_MODEL // S5_GROUP
S5_STATE = 64
S5_CHUNK = 128
N_HEADS = 16
HEAD_DIM = D_MODEL // N_HEADS
N_KV_HEADS = 4
GQ = N_HEADS // N_KV_HEADS
CMP_BLOCK = 32
CMP_STRIDE = 16
CMP_RATIO = CMP_BLOCK // CMP_STRIDE
CMP_HIDDEN = 4 * HEAD_DIM
SEL_BLOCK = 64
SEL_TOP = 16
WINDOW = 512
Q_BLOCK = 128
FORCE_SCORE = 1e4
N_BRANCH_KV = 6
REL_BUCKETS = 32
REL_MAX_DIST = 128
N_EXPERTS = 32
TOP_K = 4
D_EXPERT = D_MODEL
SWIGLU_LIMIT = 7.0
SWIGLU_ALPHA = 1.702
MOE_BLOCK = 128
RMS_EPS = 1e-5

kernel_name = "yoco_s5_nsa_moe_step"

F32 = jnp.float32


def rms_norm(x, g):
    xf = x.astype(F32)
    y = xf * lax.rsqrt(jnp.mean(xf * xf, axis=-1, keepdims=True) + RMS_EPS)
    return (y * g.astype(F32)).astype(x.dtype)


def modulate(h, shift, scale):
    return h * (1 + scale[:, None, :]) + shift[:, None, :]


def adaln(c, w, b, n):
    return jnp.split(jax.nn.silu(c) @ w + b, n, axis=-1)


def s5_discretize(lam_re, lam_im, log_dt, b_re, b_im):
    lr, li = lam_re.astype(F32), lam_im.astype(F32)
    dt = jnp.exp(log_dt.astype(F32))[:, None]
    mag = jnp.exp(dt * lr)
    ar, ai = mag * jnp.cos(dt * li), mag * jnp.sin(dt * li)
    den = lr * lr + li * li
    nr, ni = ar - 1.0, ai
    fr = (nr * lr + ni * li) / den
    fi = (ni * lr - nr * li) / den
    br, bi = b_re.astype(F32), b_im.astype(F32)
    bbr = fr[..., None] * br - fi[..., None] * bi
    bbi = fr[..., None] * bi + fi[..., None] * br
    return ar, ai, bbr, bbi


def _lin_rec_op(e1, e2):
    a1r, a1i, b1r, b1i = e1
    a2r, a2i, b2r, b2i = e2
    return (a2r * a1r - a2i * a1i, a2r * a1i + a2i * a1r,
            a2r * b1r - a2i * b1i + b2r, a2r * b1i + a2i * b1r + b2i)


def s5_mixer(u, h0, lam_re, lam_im, log_dt, b_re, b_im, c_re, c_im, d_skip, w_glu):
    Bn, L, _ = u.shape
    ar, ai, bbr, bbi = s5_discretize(lam_re, lam_im, log_dt, b_re, b_im)
    chunk = math.gcd(L, S5_CHUNK)
    uf = u.astype(F32)
    uc = uf.reshape(Bn, L // chunk, chunk, S5_GROUPS, S5_GROUP).transpose(1, 0, 2, 3, 4)
    cr, ci = c_re.astype(F32), c_im.astype(F32)
    a_shape = (Bn, chunk, S5_GROUPS, S5_STATE)

    def step(h, u_c):
        hr, hi = h
        bur = jnp.einsum('bsgc,gpc->bsgp', u_c, bbr)
        bui = jnp.einsum('bsgc,gpc->bsgp', u_c, bbi)
        acr, aci, bcr, bci = lax.associative_scan(
            _lin_rec_op, (jnp.broadcast_to(ar, a_shape), jnp.broadcast_to(ai, a_shape), bur, bui), axis=1)
        sr = acr * hr[:, None] - aci * hi[:, None] + bcr
        si = acr * hi[:, None] + aci * hr[:, None] + bci
        y = jnp.einsum('bsgp,gcp->bsgc', sr, cr) - jnp.einsum('bsgp,gcp->bsgc', si, ci)
        return (sr[:, -1], si[:, -1]), y

    (hr, hi), y = lax.scan(step, (h0[..., 0].astype(F32), h0[..., 1].astype(F32)), uc)
    y = y.transpose(1, 0, 2, 3, 4).reshape(Bn, L, D_MODEL) + d_skip.astype(F32) * uf
    g = jax.nn.gelu(y).astype(u.dtype)
    val, gate = jnp.split(g @ w_glu, 2, axis=-1)
    return val * jax.nn.sigmoid(gate), jnp.stack([hr, hi], axis=-1).astype(h0.dtype)


def rel_bucket(dist):
    n = jnp.maximum(dist, 0)
    max_exact = REL_BUCKETS // 2
    nf = jnp.maximum(n, 1).astype(F32)
    large = max_exact + (jnp.log(nf / max_exact) / math.log(REL_MAX_DIST / max_exact)
                         * (REL_BUCKETS - max_exact)).astype(jnp.int32)
    return jnp.where(n < max_exact, n, jnp.minimum(large, REL_BUCKETS - 1))


def masked_softmax(logits, mask):
    l = jnp.where(mask, logits, -jnp.inf)
    m = jnp.max(l, axis=-1, keepdims=True)
    m = jnp.where(jnp.isfinite(m), m, 0.0)
    p = jnp.exp(l - m)
    s = jnp.sum(p, axis=-1, keepdims=True)
    return p / jnp.where(s > 0, s, 1.0)


def cmp_parts(rows, w1, pos):
    Bn, T = rows.shape[:2]
    ch = rows.reshape(Bn, T // CMP_STRIDE, CMP_STRIDE, N_KV_HEADS, HEAD_DIM)
    w1r = w1.reshape(CMP_RATIO, CMP_STRIDE, HEAD_DIM, CMP_HIDDEN)
    pos_term = jnp.einsum('rsd,rsdf->rf', pos.reshape(CMP_RATIO, CMP_STRIDE, HEAD_DIM), w1r)
    return jnp.einsum('bcskd,rsdf->brckf', ch, w1r) + pos_term[None, :, None, None, :]


def cmp_finish(parts, b1, w2):
    n_cmp = parts.shape[2] - CMP_RATIO + 1
    pre = sum(parts[:, r, r:r + n_cmp] for r in range(CMP_RATIO))
    return jax.nn.gelu(pre + b1) @ w2


def gather_blocks(blocks, idx):
    b = jnp.arange(blocks.shape[0])[:, None, None, None]
    k = jnp.arange(N_KV_HEADS)[None, :, None, None]
    return blocks[b, k, idx]


def nsa_core(q, gates, t_pos, kc, vc, cmp_end, sel_gather, n_sel, kw, vw, pos_w, rel_bias):
    Bn, Q = q.shape[:2]
    qh = q.reshape(Bn, Q, N_KV_HEADS, GQ, HEAD_DIM)
    scale = HEAD_DIM ** -0.5
    tbl = rel_bias.astype(F32).reshape(REL_BUCKETS, N_KV_HEADS, GQ)
    lc = jnp.einsum('bqkgd,bnkd->bkgqn', qh, kc).astype(F32) * scale
    dist_c = t_pos[:, None] - cmp_end[None, :]
    lc = lc + jnp.transpose(tbl[rel_bucket(dist_c)], (2, 3, 0, 1))
    p_c = masked_softmax(lc, dist_c >= 0)
    o_c = jnp.einsum('bkgqn,bnkd->bqkgd', p_c.astype(vc.dtype), vc)
    ci = jnp.arange(cmp_end.shape[0])[:, None] * CMP_STRIDE
    sj = jnp.arange(n_sel)[None, :] * SEL_BLOCK
    overlap = ((ci < sj + SEL_BLOCK) & (ci + CMP_BLOCK > sj)).astype(F32)
    imp = jnp.einsum('bkgqn,ns->bkqs', p_c, overlap)
    blk = jnp.arange(n_sel)[None, :]
    cur = (t_pos // SEL_BLOCK)[:, None]
    forced = (blk == 0) | (blk == cur) | (blk == cur - 1)
    score = jnp.where(blk * SEL_BLOCK <= t_pos[:, None], imp + jnp.where(forced, FORCE_SCORE, 0.0), -jnp.inf)
    top_s, idx = lax.top_k(score, min(SEL_TOP, n_sel))
    n_top = idx.shape[-1]
    ks, vs = sel_gather(idx)
    ls = jnp.einsum('bqkgd,bkqnld->bkgqnl', qh, ks).astype(F32) * scale
    pos_s = idx[..., None] * SEL_BLOCK + jnp.arange(SEL_BLOCK)
    dist_s = t_pos[None, None, :, None, None] - pos_s
    kvh = jnp.arange(N_KV_HEADS)[None, :, None, None, None]
    bias_s = tbl.transpose(1, 0, 2)[kvh, rel_bucket(dist_s)]
    ls = ls + jnp.moveaxis(bias_s, -1, 2)
    mask_s = jnp.isfinite(top_s)[..., None] & (dist_s >= 0)
    p_s = masked_softmax(ls.reshape(Bn, N_KV_HEADS, GQ, Q, n_top * SEL_BLOCK),
                         mask_s.reshape(Bn, N_KV_HEADS, 1, Q, n_top * SEL_BLOCK)).reshape(ls.shape)
    o_s = jnp.einsum('bkgqnl,bkqnld->bqkgd', p_s.astype(vs.dtype), vs)
    lw = jnp.einsum('bqkgd,bwkd->bkgqw', qh, kw).astype(F32) * scale
    dist_w = t_pos[:, None] - pos_w[None, :]
    lw = lw + jnp.transpose(tbl[rel_bucket(dist_w)], (2, 3, 0, 1))
    mask_w = (dist_w >= 0) & (dist_w <= WINDOW) & (pos_w >= 0)[None, :]
    p_w = masked_softmax(lw, mask_w)
    o_w = jnp.einsum('bkgqw,bwkd->bqkgd', p_w.astype(vw.dtype), vw)
    g = gates.reshape(Bn, Q, N_KV_HEADS, GQ, 3, 1)
    o = g[..., 0, :] * o_c + g[..., 1, :] * o_s + g[..., 2, :] * o_w
    return o.reshape(Bn, Q, N_HEADS * HEAD_DIM)


def nsa_prompt(q, gates, kv, rel_bias, cmp_w1, cmp_b1, cmp_w2, cmp_pos):
    Bn, T = kv.shape[:2]
    kc = cmp_finish(cmp_parts(kv[:, :, 0], cmp_w1[0], cmp_pos[0]), cmp_b1[0], cmp_w2[0])
    vc = cmp_finish(cmp_parts(kv[:, :, 1], cmp_w1[1], cmp_pos[1]), cmp_b1[1], cmp_w2[1])
    cmp_end = jnp.arange(kc.shape[1]) * CMP_STRIDE + CMP_BLOCK - 1
    n_sel = -(-T // SEL_BLOCK)
    sel = jnp.pad(kv[:, :, 2:4], ((0, 0), (0, n_sel * SEL_BLOCK - T), (0, 0), (0, 0), (0, 0)))
    sel = sel.reshape(Bn, n_sel, SEL_BLOCK, 2, N_KV_HEADS, HEAD_DIM).transpose(0, 4, 1, 2, 3, 5)

    def sel_gather(idx):
        g = gather_blocks(sel, idx)
        return g[..., 0, :], g[..., 1, :]

    win = jnp.pad(kv[:, :, 4:6], ((0, 0), (WINDOW, 0), (0, 0), (0, 0), (0, 0)))
    qblk = math.gcd(T, Q_BLOCK)

    def block(qb):
        s0 = qb * qblk
        w = lax.dynamic_slice_in_dim(win, s0, WINDOW + qblk, axis=1)
        return nsa_core(lax.dynamic_slice_in_dim(q, s0, qblk, 1), lax.dynamic_slice_in_dim(gates, s0, qblk, 1),
                        s0 + jnp.arange(qblk), kc, vc, cmp_end, sel_gather, n_sel,
                        w[:, :, 0], w[:, :, 1], s0 - WINDOW + jnp.arange(WINDOW + qblk), rel_bias)

    o = lax.map(block, jnp.arange(T // qblk))
    return o.transpose(1, 0, 2, 3).reshape(Bn, T, N_HEADS * HEAD_DIM)


def nsa_sample(q, gates, kv_new, win_buf, cache_nsa_kv, page_table, rel_bias, cmp_w1, cmp_b1, cmp_w2, cmp_pos):
    Bn, S = kv_new.shape[:2]
    past = page_table.shape[1] * PAGE_SIZE
    cmp_past = cache_nsa_kv[page_table, :, :2].reshape(Bn, past, 2, N_KV_HEADS, HEAD_DIM)
    n_full = (S // CMP_STRIDE) * CMP_STRIDE

    def compress(j):
        parts = jnp.concatenate([cmp_parts(cmp_past[:, :, j], cmp_w1[j], cmp_pos[j]),
                                 cmp_parts(kv_new[:, :n_full, j], cmp_w1[j], cmp_pos[j])], axis=2)
        return cmp_finish(parts, cmp_b1[j], cmp_w2[j])

    kc, vc = compress(0), compress(1)
    cmp_end = jnp.arange(kc.shape[1]) * CMP_STRIDE + CMP_BLOCK - 1
    n_past_blk = past // SEL_BLOCK
    n_new_blk = -(-S // SEL_BLOCK)
    new_sel = jnp.pad(kv_new[:, :, 2:4], ((0, 0), (0, n_new_blk * SEL_BLOCK - S), (0, 0), (0, 0), (0, 0)))
    new_sel = new_sel.reshape(Bn, n_new_blk, SEL_BLOCK, 2, N_KV_HEADS, HEAD_DIM).transpose(0, 4, 1, 2, 3, 5)
    blk_per_page = PAGE_SIZE // SEL_BLOCK

    def sel_gather(idx):
        ip = jnp.minimum(idx, n_past_blk - 1)
        b = jnp.arange(Bn)[:, None, None, None]
        phys = page_table[b, ip // blk_per_page]
        rows = (ip % blk_per_page)[..., None] * SEL_BLOCK + jnp.arange(SEL_BLOCK)
        comp = 2 + jnp.arange(2)
        kvh = jnp.arange(N_KV_HEADS)[None, :, None, None, None, None]
        g_past = cache_nsa_kv[phys[..., None, None], rows[..., None], comp, kvh]
        g_new = gather_blocks(new_sel, jnp.clip(idx - n_past_blk, 0, n_new_blk - 1))
        g = jnp.where((idx < n_past_blk)[..., None, None, None], g_past, g_new)
        return g[..., 0, :], g[..., 1, :]

    wb = win_buf.shape[1]
    win = jnp.concatenate([win_buf, kv_new[:, :, 4:6]], axis=1)
    return nsa_core(q, gates, past + jnp.arange(S), kc, vc, cmp_end, sel_gather, n_past_blk + n_new_blk,
                    win[:, :, 0], win[:, :, 1], past - wb + jnp.arange(wb + S), rel_bias)


def moe_ffn(h, w_router, b_router, w_gu, b_gu, w_down, b_down):
    shp = h.shape
    x = h.reshape(-1, shp[-1])
    n = x.shape[0]
    logits = (x @ w_router).astype(F32) + b_router.astype(F32)
    top_v, top_i = lax.top_k(logits, TOP_K)
    top_w = jax.nn.softmax(top_v, axis=-1)
    e_flat = top_i.reshape(-1)
    n_asg = e_flat.shape[0]
    order = jnp.argsort(e_flat)
    e_sorted = e_flat[order]
    tok_sorted = (order // TOP_K).astype(jnp.int32)
    w_sorted = top_w.reshape(-1)[order]
    counts = jnp.bincount(e_flat, length=N_EXPERTS)
    padded = (counts + MOE_BLOCK - 1) // MOE_BLOCK * MOE_BLOCK
    pend = jnp.cumsum(padded)
    pstart = pend - padded
    start = jnp.cumsum(counts) - counts
    dest = pstart[e_sorted] + jnp.arange(n_asg) - start[e_sorted]
    n_blk = -(-n_asg // MOE_BLOCK) + N_EXPERTS
    row_tok = jnp.full((n_blk * MOE_BLOCK,), n, jnp.int32).at[dest].set(tok_sorted)
    row_w = jnp.zeros((n_blk * MOE_BLOCK,), F32).at[dest].set(w_sorted)
    blk_e = jnp.minimum(jnp.searchsorted(pend, jnp.arange(n_blk) * MOE_BLOCK, side='right'), N_EXPERTS - 1)
    x_pad = jnp.concatenate([x, jnp.zeros((1, shp[-1]), x.dtype)], axis=0)

    def expert_block(args):
        toks, e = args
        gu = x_pad[toks] @ w_gu[e] + b_gu[e]
        gate, up = jnp.split(gu, 2, axis=-1)
        gate = jnp.minimum(gate, SWIGLU_LIMIT)
        up = jnp.clip(up, -SWIGLU_LIMIT, SWIGLU_LIMIT)
        return ((up + 1) * gate * jax.nn.sigmoid(SWIGLU_ALPHA * gate)) @ w_down[e] + b_down[e]

    yb = lax.map(expert_block, (row_tok.reshape(n_blk, MOE_BLOCK), blk_e))
    y = jnp.zeros((n + 1, shp[-1]), F32).at[row_tok].add(yb.reshape(-1, shp[-1]).astype(F32) * row_w[:, None])
    return y[:n].astype(h.dtype).reshape(shp)


def setup_inputs(seed: int = 0) -> dict:
    key = jax.random.key(seed)
    keys = list(jax.random.split(key, 48))

    def nrm(shape, s):
        return jax.random.normal(keys.pop(), shape, F32) * s

    D = D_MODEL
    n_pages = PAST_LEN // PAGE_SIZE
    n_pool = (DEC_BATCH * n_pages * 5) // 4
    page_table = jax.random.permutation(keys.pop(), n_pool)[:DEC_BATCH * n_pages].reshape(DEC_BATCH, n_pages).astype(jnp.int32)
    win_rows = min(WINDOW, PAST_LEN)
    hd_all = N_HEADS * HEAD_DIM
    lam_im = jnp.pi * jnp.arange(S5_STATE, dtype=F32)
    return {
        "x_prompt": nrm((BATCH, SEQ, D), 1.0),
        "x_sample": nrm((DEC_BATCH, DEC_SEQ, D), 1.0),
        "c_prompt": nrm((BATCH, D), 1.0),
        "c_sample": nrm((DEC_BATCH, D), 1.0),
        "state_s5": nrm((N_A_LAYERS, DEC_BATCH, S5_GROUPS, S5_STATE, 2), 0.1),
        "cache_nsa_kv": nrm((n_pool, PAGE_SIZE, 4, N_KV_HEADS, HEAD_DIM), 1.0),
        "cache_win_kv": nrm((DEC_BATCH, win_rows, 2, N_KV_HEADS, HEAD_DIM), 1.0),
        "page_table": page_table,
        "w_mod": nrm((DEPTH, D, 6 * D), 0.5 * D ** -0.5),
        "b_mod": nrm((DEPTH, 6 * D), 0.02),
        "norm_g": 1.0 + nrm((DEPTH, 2, D), 0.02),
        "s5_lambda_re": -0.5 + nrm((N_A_LAYERS, S5_GROUPS, S5_STATE), 0.01),
        "s5_lambda_im": lam_im + nrm((N_A_LAYERS, S5_GROUPS, S5_STATE), 0.01),
        "s5_log_dt": jax.random.uniform(keys.pop(), (N_A_LAYERS, S5_GROUPS), F32, math.log(1e-3), math.log(1e-1)),
        "s5_b_re": nrm((N_A_LAYERS, S5_GROUPS, S5_STATE, S5_GROUP), (2 * S5_GROUP) ** -0.5),
        "s5_b_im": nrm((N_A_LAYERS, S5_GROUPS, S5_STATE, S5_GROUP), (2 * S5_GROUP) ** -0.5),
        "s5_c_re": nrm((N_A_LAYERS, S5_GROUPS, S5_GROUP, S5_STATE), 0.5),
        "s5_c_im": nrm((N_A_LAYERS, S5_GROUPS, S5_GROUP, S5_STATE), 0.5),
        "s5_d": nrm((N_A_LAYERS, D), 1.0),
        "s5_w_glu": nrm((N_A_LAYERS, D, 2 * D), D ** -0.5),
        "kv_mod_w": nrm((D, 2 * D), 0.5 * D ** -0.5),
        "kv_mod_b": nrm((2 * D,), 0.02),
        "kv_norm_g": 1.0 + nrm((D,), 0.02),
        "w_kv": nrm((D, N_BRANCH_KV * N_KV_HEADS * HEAD_DIM), D ** -0.5),
        "cmp_w1": nrm((2, CMP_BLOCK * HEAD_DIM, CMP_HIDDEN), (CMP_BLOCK * HEAD_DIM) ** -0.5),
        "cmp_b1": nrm((2, CMP_HIDDEN), 0.02),
        "cmp_w2": nrm((2, CMP_HIDDEN, HEAD_DIM), CMP_HIDDEN ** -0.5),
        "cmp_pos": nrm((2, CMP_BLOCK, HEAD_DIM), 0.1),
        "nsa_w_in": nrm((N_B_LAYERS, D, hd_all + 3 * N_HEADS), D ** -0.5),
        "nsa_w_out": nrm((N_B_LAYERS, hd_all, D), hd_all ** -0.5),
        "rel_bias": nrm((REL_BUCKETS, N_HEADS), 0.5),
        "moe_w_router": nrm((DEPTH, D, N_EXPERTS), D ** -0.5),
        "moe_b_router": nrm((DEPTH, N_EXPERTS), 0.01),
        "moe_w_gu": nrm((DEPTH, N_EXPERTS, D, 2 * D_EXPERT), D ** -0.5),
        "moe_b_gu": nrm((DEPTH, N_EXPERTS, 2 * D_EXPERT), 0.02),
        "moe_w_down": nrm((DEPTH, N_EXPERTS, D_EXPERT, D), D_EXPERT ** -0.5),
        "moe_b_down": nrm((DEPTH, N_EXPERTS, D), 0.02),
        "final_norm_g": 1.0 + nrm((D,), 0.02),
    }


def reference(x_prompt, x_sample, c_prompt, c_sample, state_s5, cache_nsa_kv, cache_win_kv, page_table,
              w_mod, b_mod, norm_g, s5_lambda_re, s5_lambda_im, s5_log_dt, s5_b_re, s5_b_im, s5_c_re, s5_c_im,
              s5_d, s5_w_glu, kv_mod_w, kv_mod_b, kv_norm_g, w_kv, cmp_w1, cmp_b1, cmp_w2, cmp_pos,
              nsa_w_in, nsa_w_out, rel_bias, moe_w_router, moe_b_router, moe_w_gu, moe_b_gu, moe_w_down,
              moe_b_down, final_norm_g):
    hq = N_HEADS * HEAD_DIM

    def kv_from_stream(x, c):
        shift, scale = adaln(c, kv_mod_w, kv_mod_b, 2)
        s = modulate(rms_norm(x, kv_norm_g), shift, scale)
        return (s @ w_kv).reshape(x.shape[0], x.shape[1], N_BRANCH_KV, N_KV_HEADS, HEAD_DIM)

    def run(x, c, s5_h0, b_mixer):
        states = []
        kv = None
        for i in range(DEPTH):
            sh1, sc1, g1, sh2, sc2, g2 = adaln(c, w_mod[i], b_mod[i], 6)
            h = modulate(rms_norm(x, norm_g[i, 0]), sh1, sc1)
            if i < N_A_LAYERS:
                out, st = s5_mixer(h, s5_h0[i], s5_lambda_re[i], s5_lambda_im[i], s5_log_dt[i], s5_b_re[i],
                                   s5_b_im[i], s5_c_re[i], s5_c_im[i], s5_d[i], s5_w_glu[i])
                states.append(st)
            else:
                if kv is None:
                    kv = kv_from_stream(x, c)
                j = i - N_A_LAYERS
                proj = h @ nsa_w_in[j]
                gates = jax.nn.sigmoid(proj[..., hq:].astype(F32)).astype(h.dtype)
                out = b_mixer(proj[..., :hq], gates, kv) @ nsa_w_out[j]
            x = x + g1[:, None] * out
            h = modulate(rms_norm(x, norm_g[i, 1]), sh2, sc2)
            x = x + g2[:, None] * moe_ffn(h, moe_w_router[i], moe_b_router[i], moe_w_gu[i], moe_b_gu[i],
                                          moe_w_down[i], moe_b_down[i])
        return rms_norm(x, final_norm_g), jnp.stack(states), kv

    h0 = jnp.zeros((N_A_LAYERS, x_prompt.shape[0], S5_GROUPS, S5_STATE, 2), x_prompt.dtype)
    y_p, st_p, kv_p = run(x_prompt, c_prompt, h0,
                          lambda q, g, kv: nsa_prompt(q, g, kv, rel_bias, cmp_w1, cmp_b1, cmp_w2, cmp_pos))
    y_s, st_s, kv_s = run(x_sample, c_sample, state_s5,
                          lambda q, g, kv: nsa_sample(q, g, kv, cache_win_kv, cache_nsa_kv, page_table, rel_bias,
                                                      cmp_w1, cmp_b1, cmp_w2, cmp_pos))
    win_p = kv_p[:, -min(WINDOW, x_prompt.shape[1]):, 4:6]
    win_s = jnp.concatenate([cache_win_kv, kv_s[:, :, 4:6]], axis=1)[:, -cache_win_kv.shape[1]:]
    return (y_p, y_s, st_p, st_s, kv_p[:, :, :4], kv_s[:, :, :4], win_p, win_s)
```

```python
import functools
import math

import numpy as np
import jax
import jax.numpy as jnp
from jax import lax
from jax.experimental import pallas as pl
from jax.experimental.pallas import tpu as pltpu

F32 = jnp.float32
BF16 = jnp.bfloat16
I32 = jnp.int32

S5_GROUP = 16
S5_STATE = 64
N_HEADS = 16
HEAD_DIM = 64
N_KV_HEADS = 4
GQ = N_HEADS // N_KV_HEADS
CMP_BLOCK = 32
CMP_STRIDE = 16
CMP_RATIO = CMP_BLOCK // CMP_STRIDE
SEL_BLOCK = 64
SEL_TOP = 16
WINDOW = 512
Q_BLOCK = 128
FORCE_SCORE = 1e4
N_BRANCH_KV = 6
REL_BUCKETS = 32
REL_MAX_DIST = 128
N_EXPERTS = 32
TOP_K = 4
SWIGLU_LIMIT = 7.0
SWIGLU_ALPHA = 1.702
RMS_EPS = 1e-5
PAGE_SIZE = 128

LANES = 128
SUBLANES = 8
VMEM_LIMIT = 56 * 1024 * 1024

MOE_ROWS = 256
S5_LANES = 256
S5_STATES = S5_LANES // S5_GROUP * S5_STATE
NEG = -1e30


def _cparams(*sem):
    return pltpu.CompilerParams(dimension_semantics=sem, vmem_limit_bytes=VMEM_LIMIT)


def _normmod(x, g, sh, sc):
    y = x * lax.rsqrt(jnp.mean(x * x, axis=-1, keepdims=True) + RMS_EPS)
    return (y * g) * (1.0 + sc) + sh


def _dot(a, b):
    return jnp.dot(a, b, preferred_element_type=F32)


def _dot_nt(a, b):
    return lax.dot_general(a, b, (((1,), (1,)), ((), ())), preferred_element_type=F32)


def _split(x):
    hi = x.astype(BF16)
    lo = (x - hi.astype(F32)).astype(BF16)
    return hi, lo


def _mod_spec(m, rb, tm):
    nb, mr, d = m.shape
    if mr == 1:
        return pl.BlockSpec((1, 1, d), lambda i, *_: ((i * tm) // rb, 0, 0))
    assert mr == rb and rb % tm == 0
    per = rb // tm
    return pl.BlockSpec((1, tm, d), lambda i, *_: (i // per, i % per, 0))


def _row_tile(rb, want):
    t = math.gcd(rb, want)
    return t


def _adaln_kernel(c_ref, w_ref, b_ref, o_ref):
    c = c_ref[...]
    a = (c * jax.nn.sigmoid(c)).astype(BF16)
    o_ref[0] = _dot(a, w_ref[0].astype(BF16)) + b_ref[0]


def _adaln(c, w, b):
    n, d = c.shape
    nl, _, nn = w.shape
    npad = -(-n // SUBLANES) * SUBLANES
    cp = jnp.pad(c, ((0, npad - n), (0, 0)))
    tn = math.gcd(nn, 1024)
    out = pl.pallas_call(
        _adaln_kernel,
        grid=(nl, nn // tn),
        in_specs=[pl.BlockSpec((npad, d), lambda l, j: (0, 0)),
                  pl.BlockSpec((1, d, tn), lambda l, j: (l, 0, j)),
                  pl.BlockSpec((1, 1, tn), lambda l, j: (l, 0, j))],
        out_specs=pl.BlockSpec((1, npad, tn), lambda l, j: (l, 0, j)),
        out_shape=jax.ShapeDtypeStruct((nl, npad, nn), F32),
        compiler_params=_cparams("parallel", "parallel"),
        name="adaln",
    )(cp, w, b.reshape(nl, 1, nn))
    return out[:, :n]


def _s5_disc_kernel(lr_ref, li_ref, ldt_ref, br_ref, bi_ref, ar_ref, ai_ref, bbr_ref, bbi_ref):
    lr, li = lr_ref[...], li_ref[...]
    dt = jnp.exp(ldt_ref[...])
    mag = jnp.exp(dt * lr)
    ar, ai = mag * jnp.cos(dt * li), mag * jnp.sin(dt * li)
    den = lr * lr + li * li
    nr, ni = ar - 1.0, ai
    fr = (nr * lr + ni * li) / den
    fi = (ni * lr - nr * li) / den
    br, bi = br_ref[...], bi_ref[...]
    ar_ref[...] = ar
    ai_ref[...] = ai
    bbr_ref[...] = fr * br - fi * bi
    bbi_ref[...] = fr * bi + fi * br


def _s5_discretize(lam_re, lam_im, log_dt, b_re, b_im):
    g, p = lam_re.shape
    c = b_re.shape[-1]
    full3 = lambda s: pl.BlockSpec(s, lambda: (0, 0, 0))
    return pl.pallas_call(
        _s5_disc_kernel,
        in_specs=[full3((g, p, 1)), full3((g, p, 1)), full3((g, 1, 1)), full3((g, p, c)), full3((g, p, c))],
        out_specs=[full3((g, p, 1)), full3((g, p, 1)), full3((g, p, c)), full3((g, p, c))],
        out_shape=[jax.ShapeDtypeStruct((g, p, 1), F32)] * 2 + [jax.ShapeDtypeStruct((g, p, c), F32)] * 2,
        name="s5_discretize",
    )(lam_re.reshape(g, p, 1), lam_im.reshape(g, p, 1), log_dt.reshape(g, 1, 1), b_re, b_im)


def _s5_kernel(x_ref, g_ref, sh_ref, sc_ref, d_ref, wbr_ref, wbi_ref, ar_ref, ai_ref, wcr_ref, wci_ref,
               h0r_ref, h0i_ref, y_ref, str_ref, sti_ref, bur, bui, sre, sim, car_r, car_i, *, seq, rows, nj):
    c = pl.program_id(1)
    h = _normmod(x_ref[0], g_ref[...], sh_ref[0], sc_ref[0])
    dsk = d_ref[...]
    for j in range(nj):
        lo, hi = j * S5_LANES, (j + 1) * S5_LANES
        slo, shi = j * S5_STATES, (j + 1) * S5_STATES
        hj = h[:, lo:hi]
        hb = hj.astype(BF16)
        ar, ai = ar_ref[:, slo:shi], ai_ref[:, slo:shi]
        if seq:
            @pl.when(c == 0)
            def _():
                car_r[j] = h0r_ref[0, :, slo:shi]
                car_i[j] = h0i_ref[0, :, slo:shi]
            bur[...] = _dot(hb, wbr_ref[j])
            bui[...] = _dot(hb, wbi_ref[j])

            def step(t, carry):
                sr, si = carry
                br = bur[pl.ds(t, 1), :]
                bi = bui[pl.ds(t, 1), :]
                nr = ar * sr - ai * si + br
                ni = ar * si + ai * sr + bi
                sre[pl.ds(t, 1), :] = nr
                sim[pl.ds(t, 1), :] = ni
                return nr, ni

            sr, si = lax.fori_loop(0, rows, step, (car_r[j], car_i[j]), unroll=8)
            car_r[j] = sr
            car_i[j] = si
            str_ref[0, :, slo:shi] = sr
            sti_ref[0, :, slo:shi] = si
            s_r, s_i = sre[...], sim[...]
        else:
            h0r, h0i = h0r_ref[0, :, slo:shi], h0i_ref[0, :, slo:shi]
            s_r = ar * h0r - ai * h0i + _dot(hb, wbr_ref[j])
            s_i = ar * h0i + ai * h0r + _dot(hb, wbi_ref[j])
            str_ref[0, :, slo:shi] = s_r
            sti_ref[0, :, slo:shi] = s_i
        y = _dot(s_r.astype(BF16), wcr_ref[j]) - _dot(s_i.astype(BF16), wci_ref[j])
        y_ref[0, :, lo:hi] = jax.nn.gelu(y + dsk[:, lo:hi] * hj)


def _s5_scan(x3, g, sh, sc, dsk, wbr, wbi, ar, ai, wcr, wci, h0r, h0i, *, seq):
    nb, rb, d = x3.shape
    ns = ar.shape[-1]
    nj = d // S5_LANES
    rows = _row_tile(rb, 256)
    hr = 1 if seq else rows
    mrow = sh.shape[1]
    mspec = (pl.BlockSpec((1, 1, d), lambda b, c: (b, 0, 0)) if mrow == 1
             else pl.BlockSpec((1, rows, d), lambda b, c: (b, c, 0)))
    hspec = (pl.BlockSpec((1, 1, ns), lambda b, c: (b, 0, 0)) if seq
             else pl.BlockSpec((1, rows, ns), lambda b, c: (b, c, 0)))
    const2 = lambda s: pl.BlockSpec(s, lambda b, c: (0, 0))
    const3 = lambda s: pl.BlockSpec(s, lambda b, c: (0, 0, 0))
    kern = functools.partial(_s5_kernel, seq=seq, rows=rows, nj=nj)
    return pl.pallas_call(
        kern,
        grid=(nb, rb // rows),
        in_specs=[pl.BlockSpec((1, rows, d), lambda b, c: (b, c, 0)), const2((1, d)), mspec, mspec, const2((1, d)),
                  const3(wbr.shape), const3(wbi.shape), const2((1, ns)), const2((1, ns)),
                  const3(wcr.shape), const3(wci.shape), hspec, hspec],
        out_specs=[pl.BlockSpec((1, rows, d), lambda b, c: (b, c, 0)), hspec, hspec],
        out_shape=[jax.ShapeDtypeStruct((nb, rb, d), F32),
                   jax.ShapeDtypeStruct(h0r.shape, F32), jax.ShapeDtypeStruct(h0i.shape, F32)],
        scratch_shapes=[pltpu.VMEM((rows, S5_STATES), F32)] * 4 + [pltpu.VMEM((nj, 1, S5_STATES), F32)] * 2,
        compiler_params=_cparams("parallel", "arbitrary"),
        name="s5_scan" if seq else "s5_step",
    )(x3, g, sh, sc, dsk, wbr, wbi, ar, ai, wcr, wci, h0r, h0i)


def _s5_weights(lam_re, lam_im, log_dt, b_re, b_im, c_re, c_im):
    ar, ai, bbr, bbi = _s5_discretize(lam_re, lam_im, log_dt, b_re, b_im)
    g, p, c = bbr.shape
    gs = S5_LANES // c
    nj = g // gs
    eye = jnp.eye(gs, dtype=F32)

    def wb(bb):
        w = jnp.einsum('jgpc,gh->jgchp', bb.reshape(nj, gs, p, c), eye)
        return w.reshape(nj, gs * c, gs * p).astype(BF16)

    def wc(cc):
        w = jnp.einsum('jgcp,gh->jgphc', cc.reshape(nj, gs, c, p), eye)
        return w.reshape(nj, gs * p, gs * c).astype(BF16)

    return (ar.reshape(1, g * p), ai.reshape(1, g * p), wb(bbr), wb(bbi), wc(c_re), wc(c_im))


def _glu_kernel(a_ref, x_ref, g1_ref, w_ref, o_ref, *, d):
    gu = _dot(a_ref[...].astype(BF16), w_ref[...])
    o_ref[...] = x_ref[...] + g1_ref[0] * (gu[:, :d] * jax.nn.sigmoid(gu[:, d:]))


def _glu_residual(a, x, g1, w_bf16, rb):
    t, d = x.shape
    tm = _row_tile(rb, 512)
    return pl.pallas_call(
        functools.partial(_glu_kernel, d=d),
        grid=(t // tm,),
        in_specs=[pl.BlockSpec((tm, d), lambda i: (i, 0)), pl.BlockSpec((tm, d), lambda i: (i, 0)),
                  _mod_spec(g1, rb, tm), pl.BlockSpec(w_bf16.shape, lambda i: (0, 0))],
        out_specs=pl.BlockSpec((tm, d), lambda i: (i, 0)),
        out_shape=jax.ShapeDtypeStruct((t, d), F32),
        compiler_params=_cparams("parallel"),
        name="glu_residual",
    )(a, x, g1, w_bf16)


def _router_kernel(x_ref, g_ref, sh_ref, sc_ref, w_ref, b_ref, ti_ref, tw_ref, rk_ref, cnt_ref, carry, *, tm):
    i = pl.program_id(0)

    @pl.when(i == 0)
    def _():
        carry[...] = jnp.zeros_like(carry)

    h = _normmod(x_ref[...], g_ref[...], sh_ref[0], sc_ref[0])
    h_hi, h_lo = _split(h)
    w_hi, w_lo = _split(w_ref[...])
    logits = _dot(h_hi, w_hi) + (_dot(h_lo, w_hi) + _dot(h_hi, w_lo)) + b_ref[...]
    lane = lax.broadcasted_iota(I32, (tm, LANES), 1)
    lane_f = lane.astype(F32)
    work = jnp.where(lane < N_EXPERTS, logits, -jnp.inf)
    vals, sels, idxs = [], [], []
    for _ in range(TOP_K):
        m = jnp.max(work, axis=-1, keepdims=True)
        idx = jnp.min(jnp.where(work == m, lane_f, float(LANES)), axis=-1, keepdims=True)
        sel = lane_f == idx
        work = jnp.where(sel, -jnp.inf, work)
        vals.append(m)
        sels.append(sel)
        idxs.append(idx)
    es = [jnp.exp(v - vals[0]) for v in vals]
    tot = es[0] + es[1] + es[2] + es[3]
    oh = jnp.zeros((tm, LANES), F32)
    for sel in sels:
        oh = oh + sel.astype(F32)
    r_io = lax.broadcasted_iota(I32, (tm, tm), 0)
    c_io = lax.broadcasted_iota(I32, (tm, tm), 1)
    lstrict = (c_io < r_io).astype(BF16)
    base = carry[...] + _dot(lstrict, oh.astype(BF16))
    ti = jnp.zeros((tm, LANES), F32)
    tw = jnp.zeros((tm, LANES), F32)
    rk = jnp.zeros((tm, LANES), F32)
    for k in range(TOP_K):
        rank = jnp.sum(jnp.where(sels[k], base, 0.0), axis=-1, keepdims=True)
        ti = jnp.where(lane == k, idxs[k], ti)
        tw = jnp.where(lane == k, es[k] / tot, tw)
        rk = jnp.where(lane == k, rank, rk)
    ti_ref[...] = ti.astype(I32)
    tw_ref[...] = tw
    rk_ref[...] = rk.astype(I32)
    carry[...] = carry[...] + jnp.sum(oh, axis=0, keepdims=True)
    cnt_ref[...] = carry[...]


def _router(x, rb, g, sh, sc, w_pad, b_pad):
    t, d = x.shape
    tm = _row_tile(rb, 512)
    row = lambda i: (i, 0)
    return pl.pallas_call(
        functools.partial(_router_kernel, tm=tm),
        grid=(t // tm,),
        in_specs=[pl.BlockSpec((tm, d), row), pl.BlockSpec((1, d), lambda i: (0, 0)),
                  _mod_spec(sh, rb, tm), _mod_spec(sc, rb, tm),
                  pl.BlockSpec((d, LANES), lambda i: (0, 0)), pl.BlockSpec((1, LANES), lambda i: (0, 0))],
        out_specs=[pl.BlockSpec((tm, LANES), row)] * 3 + [pl.BlockSpec((1, LANES), lambda i: (0, 0))],
        out_shape=[jax.ShapeDtypeStruct((t, LANES), I32), jax.ShapeDtypeStruct((t, LANES), F32),
                   jax.ShapeDtypeStruct((t, LANES), I32), jax.ShapeDtypeStruct((1, LANES), F32)],
        scratch_shapes=[pltpu.VMEM((1, LANES), F32)],
        compiler_params=_cparams("arbitrary"),
        name="moe_router",
    )(x, g, sh, sc, w_pad, b_pad)


def _row_copy(src, dst, sem):
    return pltpu.make_async_copy(src, dst, sem)


def _dispatch_kernel(dest_ref, x_ref, g_ref, sh_ref, sc_ref, xs_in_ref, xs_ref, hbuf, sem, *, tm):
    del xs_in_ref
    hbuf[...] = _normmod(x_ref[...], g_ref[...], sh_ref[0], sc_ref[0])

    def start(r, _):
        for k in range(TOP_K):
            d = dest_ref[r * TOP_K + k]
            _row_copy(hbuf.at[pl.ds(r, 1)], xs_ref.at[pl.ds(d, 1)], sem).start()
        return 0

    lax.fori_loop(0, tm, start, 0)

    def wait(r, _):
        for k in range(TOP_K):
            _row_copy(hbuf.at[pl.ds(0, 1)], xs_ref.at[pl.ds(0, 1)], sem).wait()
        return 0

    lax.fori_loop(0, tm, wait, 0)


def _dispatch(dest, x, rb, g, sh, sc, n_rows):
    t, d = x.shape
    tm = _row_tile(rb, 256)
    zeros = jnp.zeros((n_rows, d), F32)
    return pl.pallas_call(
        functools.partial(_dispatch_kernel, tm=tm),
        grid=(t // tm,),
        in_specs=[pl.BlockSpec((tm * TOP_K,), lambda i: (i,), memory_space=pltpu.SMEM),
                  pl.BlockSpec((tm, d), lambda i: (i, 0)), pl.BlockSpec((1, d), lambda i: (0, 0)),
                  _mod_spec(sh, rb, tm), _mod_spec(sc, rb, tm),
                  pl.BlockSpec(memory_space=pl.ANY)],
        out_specs=pl.BlockSpec(memory_space=pl.ANY),
        out_shape=jax.ShapeDtypeStruct((n_rows, d), F32),
        scratch_shapes=[pltpu.VMEM((tm, d), F32), pltpu.SemaphoreType.DMA(())],
        input_output_aliases={5: 0},
        compiler_params=_cparams("arbitrary"),
        name="moe_dispatch",
    )(dest, x, g, sh, sc, zeros)


def _expert_kernel(be_ref, xs_ref, wgu_ref, bgu_ref, wd_ref, bd_ref, ys_ref, wgu_s, wd_s, *, de):
    i = pl.program_id(0)
    changed = jnp.logical_or(i == 0, be_ref[i] != be_ref[jnp.maximum(i - 1, 0)])

    @pl.when(changed)
    def _():
        wgu_s[...] = wgu_ref[0].astype(BF16)
        wd_s[...] = wd_ref[0].astype(BF16)

    gu = _dot(xs_ref[...].astype(BF16), wgu_s[...]) + bgu_ref[0]
    gate = jnp.minimum(gu[:, :de], SWIGLU_LIMIT)
    up = jnp.clip(gu[:, de:], -SWIGLU_LIMIT, SWIGLU_LIMIT)
    act = (up + 1.0) * gate * jax.nn.sigmoid(SWIGLU_ALPHA * gate)
    ys_ref[...] = _dot(act.astype(BF16), wd_s[...]) + bd_ref[0]


def _experts(blk_e, xs, w_gu, b_gu, w_down, b_down):
    n_rows, d = xs.shape
    ne, _, de2 = w_gu.shape
    de = de2 // 2
    n_blk = n_rows // MOE_ROWS
    grid_spec = pltpu.PrefetchScalarGridSpec(
        num_scalar_prefetch=1,
        grid=(n_blk,),
        in_specs=[pl.BlockSpec((MOE_ROWS, d), lambda i, be: (i, 0)),
                  pl.BlockSpec((1, d, de2), lambda i, be: (be[i], 0, 0)),
                  pl.BlockSpec((1, 1, de2), lambda i, be: (be[i], 0, 0)),
                  pl.BlockSpec((1, de, d), lambda i, be: (be[i], 0, 0)),
                  pl.BlockSpec((1, 1, d), lambda i, be: (be[i], 0, 0))],
        out_specs=pl.BlockSpec((MOE_ROWS, d), lambda i, be: (i, 0)),
        scratch_shapes=[pltpu.VMEM((d, de2), BF16), pltpu.VMEM((de, d), BF16)],
    )
    return pl.pallas_call(
        functools.partial(_expert_kernel, de=de),
        grid_spec=grid_spec,
        out_shape=jax.ShapeDtypeStruct((n_rows, d), F32),
        compiler_params=_cparams("arbitrary"),
        name="moe_experts",
    )(blk_e, xs, w_gu, b_gu.reshape(ne, 1, de2), w_down, b_down.reshape(ne, 1, d))


def _combine_kernel(dest_ref, tw_ref, x_ref, g2_ref, ys_ref, o_ref, gbuf, sem, *, tm):
    def start(r, _):
        for k in range(TOP_K):
            d = dest_ref[r * TOP_K + k]
            _row_copy(ys_ref.at[pl.ds(d, 1)], gbuf.at[k, pl.ds(r, 1)], sem).start()
        return 0

    lax.fori_loop(0, tm, start, 0)

    def wait(r, _):
        for k in range(TOP_K):
            _row_copy(ys_ref.at[pl.ds(0, 1)], gbuf.at[k, pl.ds(0, 1)], sem).wait()
        return 0

    lax.fori_loop(0, tm, wait, 0)
    tw = tw_ref[...]
    y = tw[:, 0:1] * gbuf[0]
    for k in range(1, TOP_K):
        y = y + tw[:, k:k + 1] * gbuf[k]
    o_ref[...] = x_ref[...] + g2_ref[0] * y


def _combine(dest, tw, x, rb, g2, ys):
    t, d = x.shape
    tm = _row_tile(rb, 256)
    return pl.pallas_call(
        functools.partial(_combine_kernel, tm=tm),
        grid=(t // tm,),
        in_specs=[pl.BlockSpec((tm * TOP_K,), lambda i: (i,), memory_space=pltpu.SMEM),
                  pl.BlockSpec((tm, LANES), lambda i: (i, 0)), pl.BlockSpec((tm, d), lambda i: (i, 0)),
                  _mod_spec(g2, rb, tm), pl.BlockSpec(memory_space=pl.ANY)],
        out_specs=pl.BlockSpec((tm, d), lambda i: (i, 0)),
        out_shape=jax.ShapeDtypeStruct((t, d), F32),
        scratch_shapes=[pltpu.VMEM((TOP_K, tm, d), F32), pltpu.SemaphoreType.DMA(())],
        compiler_params=_cparams("arbitrary"),
        name="moe_combine",
    )(dest, tw, x, g2, ys)


def _moe(x, rb, g, sh, sc, g2, w_router, b_router, w_gu, b_gu, w_down, b_down):
    t, d = x.shape
    ne = w_router.shape[1]
    w_pad = jnp.pad(w_router, ((0, 0), (0, LANES - ne)))
    b_pad = jnp.pad(b_router, (0, LANES - ne)).reshape(1, LANES)
    ti, tw, rk, cnt = _router(x, rb, g, sh, sc, w_pad, b_pad)
    counts = cnt[0, :ne].astype(I32)
    padded = (counts + MOE_ROWS - 1) // MOE_ROWS * MOE_ROWS
    pend = jnp.cumsum(padded)
    pstart = pend - padded
    dest = (pstart[ti[:, :TOP_K]] + rk[:, :TOP_K]).reshape(-1).astype(I32)
    n_blk = -(-(t * TOP_K) // MOE_ROWS) + ne
    blk_e = jnp.minimum(jnp.searchsorted(pend, jnp.arange(n_blk, dtype=I32) * MOE_ROWS, side='right'),
                        ne - 1).astype(I32)
    xs = _dispatch(dest, x, rb, g, sh, sc, n_blk * MOE_ROWS)
    ys = _experts(blk_e, xs, w_gu, b_gu, w_down, b_down)
    return _combine(dest, tw, x, rb, g2, ys)


def _nm_matmul_kernel(x_ref, g_ref, sh_ref, sc_ref, w_ref, o_ref, *, sig_from):
    h = _normmod(x_ref[...], g_ref[...], sh_ref[0], sc_ref[0])
    o = _dot(h.astype(BF16), w_ref[...])
    if sig_from is not None:
        col = lax.broadcasted_iota(I32, o.shape, 1)
        o = jnp.where(col >= sig_from, jax.nn.sigmoid(o), o)
    o_ref[...] = o


def _normmod_matmul(x, rb, g, sh, sc, w_bf16, sig_from=None):
    t, d = x.shape
    n = w_bf16.shape[1]
    tm = _row_tile(rb, 512)
    return pl.pallas_call(
        functools.partial(_nm_matmul_kernel, sig_from=sig_from),
        grid=(t // tm,),
        in_specs=[pl.BlockSpec((tm, d), lambda i: (i, 0)), pl.BlockSpec((1, d), lambda i: (0, 0)),
                  _mod_spec(sh, rb, tm), _mod_spec(sc, rb, tm), pl.BlockSpec((d, n), lambda i: (0, 0))],
        out_specs=pl.BlockSpec((tm, n), lambda i: (i, 0)),
        out_shape=jax.ShapeDtypeStruct((t, n), F32),
        compiler_params=_cparams("parallel"),
        name="normmod_matmul",
    )(x, g, sh, sc, w_bf16)


def _mm_residual_kernel(a_ref, x_ref, g1_ref, w_ref, o_ref):
    o_ref[...] = x_ref[...] + g1_ref[0] * _dot(a_ref[...].astype(BF16), w_ref[...])


def _matmul_residual(a, x, g1, w_bf16, rb):
    t, d = x.shape
    k = a.shape[1]
    tm = _row_tile(rb, 512)
    return pl.pallas_call(
        _mm_residual_kernel,
        grid=(t // tm,),
        in_specs=[pl.BlockSpec((tm, k), lambda i: (i, 0)), pl.BlockSpec((tm, d), lambda i: (i, 0)),
                  _mod_spec(g1, rb, tm), pl.BlockSpec((k, d), lambda i: (0, 0))],
        out_specs=pl.BlockSpec((tm, d), lambda i: (i, 0)),
        out_shape=jax.ShapeDtypeStruct((t, d), F32),
        compiler_params=_cparams("parallel"),
        name="matmul_residual",
    )(a, x, g1, w_bf16)


def _final_norm_kernel(x_ref, g_ref, o_ref):
    x = x_ref[...]
    o_ref[...] = x * lax.rsqrt(jnp.mean(x * x, axis=-1, keepdims=True) + RMS_EPS) * g_ref[...]


def _final_norm(x, g, rb):
    t, d = x.shape
    tm = _row_tile(rb, 512)
    return pl.pallas_call(
        _final_norm_kernel,
        grid=(t // tm,),
        in_specs=[pl.BlockSpec((tm, d), lambda i: (i, 0)), pl.BlockSpec((1, d), lambda i: (0, 0))],
        out_specs=pl.BlockSpec((tm, d), lambda i: (i, 0)),
        out_shape=jax.ShapeDtypeStruct((t, d), F32),
        compiler_params=_cparams("parallel"),
        name="final_norm",
    )(x, g.reshape(1, d))


def _rel_bucket_table():
    n = np.arange(REL_MAX_DIST + 1)
    max_exact = REL_BUCKETS // 2
    nf = np.maximum(n, 1).astype(np.float32)
    large = max_exact + (np.log(nf / max_exact) / math.log(REL_MAX_DIST / max_exact)
                         * (REL_BUCKETS - max_exact)).astype(np.int32)
    return np.where(n < max_exact, n, np.minimum(large, REL_BUCKETS - 1)).astype(np.int32)


def _bias_by_dist(rel_bias):
    return rel_bias.astype(F32)[_rel_bucket_table()].T


def _stack_heads(tab):
    h, r, c = tab.shape
    return tab.reshape(N_KV_HEADS, GQ * r, c)


def _dist_tile(rows, cols, offset, col_step=1):
    d = offset + np.arange(rows)[:, None] - col_step * np.arange(cols)[None, :]
    return np.clip(d, 0, REL_MAX_DIST)


CMP_NEAR = 32
CMP_PAD = 16


def _overlap_matrix(n_cmp, n_sel_pad):
    ci = np.arange(n_cmp)[:, None] * CMP_STRIDE
    sj = np.arange(n_sel_pad)[None, :] * SEL_BLOCK
    ov = ((ci < sj + SEL_BLOCK) & (ci + CMP_BLOCK > sj)).astype(np.float32)
    return np.pad(ov, ((CMP_PAD, CMP_PAD), (0, 0)))


def _compress_kernel(x_ref, pos_ref, w1_ref, b1_ref, w2_ref, o_ref, *, hid):
    w1 = w1_ref[0]
    parts = _dot(x_ref[0, 0].astype(BF16), w1)
    pt = _dot(pos_ref[0].astype(BF16), w1)
    p0 = parts[:, :hid] + pt[0:1, :hid]
    p1 = parts[:, hid:] + pt[1:2, hid:]
    n = p1.shape[0]
    pre = p0 + pltpu.roll(p1, n - 1, axis=0) + b1_ref[0]
    o_ref[0, 0] = _dot(jax.nn.gelu(pre).astype(BF16), w2_ref[0])


def _compress(xc, cmp_w1, cmp_b1, cmp_w2, cmp_pos):
    _, ng, nch, kdim = xc.shape
    hid = cmp_w1.shape[-1]
    hd = cmp_w2.shape[-1]
    w1 = cmp_w1.reshape(2, CMP_RATIO, kdim, hid)
    w1cat = jnp.concatenate([w1[:, r] for r in range(CMP_RATIO)], axis=-1).astype(BF16)
    pos = jnp.pad(cmp_pos.reshape(2, CMP_RATIO, kdim), ((0, 0), (0, SUBLANES - CMP_RATIO), (0, 0)))
    return pl.pallas_call(
        functools.partial(_compress_kernel, hid=hid),
        grid=(2, ng),
        in_specs=[pl.BlockSpec((1, 1, nch, kdim), lambda c, n: (c, n, 0, 0)),
                  pl.BlockSpec((1, SUBLANES, kdim), lambda c, n: (c, 0, 0)),
                  pl.BlockSpec((1, kdim, CMP_RATIO * hid), lambda c, n: (c, 0, 0)),
                  pl.BlockSpec((1, 1, hid), lambda c, n: (c, 0, 0)),
                  pl.BlockSpec((1, hid, hd), lambda c, n: (c, 0, 0))],
        out_specs=pl.BlockSpec((1, 1, nch, hd), lambda c, n: (c, n, 0, 0)),
        out_shape=jax.ShapeDtypeStruct((2, ng, nch, hd), F32),
        compiler_params=_cparams("parallel", "parallel"),
        name="nsa_compress",
    )(xc, pos, w1cat, cmp_b1.reshape(2, 1, hid), cmp_w2.astype(BF16))


def _dot3_nt(a, b):
    a_hi, a_lo = _split(a)
    b_hi, b_lo = _split(b)
    return _dot_nt(a_hi, b_hi) + (_dot_nt(a_lo, b_hi) + _dot_nt(a_hi, b_lo))


def _top_blocks(score, ntop):
    lane_f = lax.broadcasted_iota(I32, score.shape, 1).astype(F32)
    sel = jnp.zeros(score.shape, F32)
    for _ in range(ntop):
        m = jnp.max(score, axis=-1, keepdims=True)
        idx = jnp.min(jnp.where(score == m, lane_f, float(score.shape[1])), axis=-1, keepdims=True)
        hit = lane_f == idx
        sel = jnp.where(jnp.logical_and(hit, m > -jnp.inf), 1.0, sel)
        score = jnp.where(hit, -jnp.inf, score)
    return sel


def _cmp_attn_kernel(q_ref, kc_ref, vc_ref, ov_ref, bnear_ref, bfar_ref, oc_ref, sel_ref, *, nch, ntop, qblk):
    qb = pl.program_id(2)
    rows = GQ * qblk
    scale = HEAD_DIM ** -0.5
    c = pl.multiple_of(qb * (qblk // CMP_STRIDE), SUBLANES)
    q = q_ref[0].reshape(rows, HEAD_DIM)
    s_far = _dot3_nt(q, kc_ref[0, 0, CMP_PAD:CMP_PAD + nch, :]) * scale + bfar_ref[0]
    s_near = _dot3_nt(q, kc_ref[0, 0, pl.ds(c, CMP_NEAR), :]) * scale + bnear_ref[0]
    n_io = lax.broadcasted_iota(I32, (rows, nch), 1)
    mask_far = n_io < c - CMP_PAD
    i_io = lax.broadcasted_iota(I32, (rows, CMP_NEAR), 0) & (qblk - 1)
    rel = lax.broadcasted_iota(I32, (rows, CMP_NEAR), 1)
    mask_near = jnp.logical_and(CMP_STRIDE * (rel - CMP_PAD) + CMP_BLOCK - 1 <= i_io, rel + c - CMP_PAD >= 0)
    sf = jnp.where(mask_far, s_far, NEG)
    sn = jnp.where(mask_near, s_near, NEG)
    m = jnp.maximum(jnp.max(sf, axis=-1, keepdims=True), jnp.max(sn, axis=-1, keepdims=True))
    pf = jnp.where(mask_far, jnp.exp(sf - m), 0.0)
    pn = jnp.where(mask_near, jnp.exp(sn - m), 0.0)
    l = jnp.sum(pf, axis=-1, keepdims=True) + jnp.sum(pn, axis=-1, keepdims=True)
    linv = 1.0 / jnp.where(l > 0, l, 1.0)
    vf = vc_ref[0, 0, CMP_PAD:CMP_PAD + nch, :].astype(BF16)
    vn = vc_ref[0, 0, pl.ds(c, CMP_NEAR), :].astype(BF16)
    o = (_dot(pf.astype(BF16), vf) + _dot(pn.astype(BF16), vn)) * linv
    oc_ref[0] = o.reshape(GQ, qblk, HEAD_DIM)
    pfn = pf * linv
    pnn = pn * linv
    pgf = pfn[0:qblk]
    pgn = pnn[0:qblk]
    for g in range(1, GQ):
        pgf = pgf + pfn[g * qblk:(g + 1) * qblk]
        pgn = pgn + pnn[g * qblk:(g + 1) * qblk]
    f_hi, f_lo = _split(pgf)
    n_hi, n_lo = _split(pgn)
    ovf = ov_ref[CMP_PAD:CMP_PAD + nch, :]
    ovn = ov_ref[pl.ds(c, CMP_NEAR), :]
    imp = (_dot(f_hi, ovf) + _dot(n_hi, ovn)) + (_dot(f_lo, ovf) + _dot(n_lo, ovn))
    nsp = imp.shape[1]
    s_io = lax.broadcasted_iota(I32, (qblk, nsp), 1)
    t = qb * qblk + lax.broadcasted_iota(I32, (qblk, nsp), 0)
    cur = lax.shift_right_logical(t, int(math.log2(SEL_BLOCK)))
    forced = jnp.logical_or(s_io == 0, jnp.logical_or(s_io == cur, s_io == cur - 1))
    score = jnp.where(s_io * SEL_BLOCK <= t, imp + jnp.where(forced, FORCE_SCORE, 0.0), -jnp.inf)
    sel_ref[0, 0] = _top_blocks(score, ntop)


def _cmp_attention(q_hm, kc_pad, vc_pad, ov, bnear, bfar, nch, ntop):
    b, _, t, hd = q_hm.shape
    qblk = math.gcd(t, Q_BLOCK)
    nsp = ov.shape[1]
    kern = functools.partial(_cmp_attn_kernel, nch=nch, ntop=ntop, qblk=qblk)
    kvspec = pl.BlockSpec((1, 1, kc_pad.shape[2], hd), lambda bb, k, i: (bb, k, 0, 0))
    return pl.pallas_call(
        kern,
        grid=(b, N_KV_HEADS, t // qblk),
        in_specs=[pl.BlockSpec((1, GQ, qblk, hd), lambda bb, k, i: (bb, k, i, 0)), kvspec, kvspec,
                  pl.BlockSpec(ov.shape, lambda bb, k, i: (0, 0)),
                  pl.BlockSpec((1,) + bnear.shape[1:], lambda bb, k, i: (k, 0, 0)),
                  pl.BlockSpec((1,) + bfar.shape[1:], lambda bb, k, i: (k, 0, 0))],
        out_specs=[pl.BlockSpec((1, GQ, qblk, hd), lambda bb, k, i: (bb, k, i, 0)),
                   pl.BlockSpec((1, 1, qblk, nsp), lambda bb, k, i: (bb, k, i, 0))],
        out_shape=[jax.ShapeDtypeStruct(q_hm.shape, F32), jax.ShapeDtypeStruct((b, N_KV_HEADS, t, nsp), F32)],
        compiler_params=_cparams("parallel", "parallel", "parallel"),
        name="nsa_cmp_attention",
    )(q_hm, kc_pad, vc_pad, ov, bnear, bfar)


def _slc_win_kernel(q_ref, sel_ref, oc_ref, gate_ref, sk_ref, sv_ref, wk_ref, wv_ref, bt_ref, bfar_ref, o_ref,
                    m_s, l_s, acc_s, *, qblk):
    qb = pl.program_id(2)
    rows = GQ * qblk
    scale = HEAD_DIM ** -0.5
    q = q_ref[0].reshape(rows, HEAD_DIM).astype(BF16)
    sel = sel_ref[0, 0].astype(BF16)
    nsp = sel.shape[1]
    bfar = bfar_ref[0]
    per = qblk // SEL_BLOCK
    blk_io = lax.broadcasted_iota(I32, (nsp, qblk), 0)
    key_blk = lax.shift_right_logical(lax.broadcasted_iota(I32, (nsp, qblk), 1), int(math.log2(SEL_BLOCK)))
    row_i = lax.broadcasted_iota(I32, (rows, qblk), 0) & (qblk - 1)
    col_j = lax.broadcasted_iota(I32, (rows, qblk), 1)

    def tile(ref, kt):
        return ref[0, 0, pl.ds(pl.multiple_of(kt * qblk, qblk), qblk), :]

    def sel_mask(kt):
        expand = (blk_io == per * kt + key_blk).astype(BF16)
        mk = _dot(sel, expand)
        return jnp.concatenate([mk] * GQ, axis=0) > 0.5

    def update(s, mask, v):
        s = jnp.where(mask, s, NEG)
        m_prev = m_s[...]
        m_new = jnp.maximum(m_prev, jnp.max(s, axis=-1, keepdims=True))
        alpha = jnp.exp(m_prev - m_new)
        p = jnp.where(mask, jnp.exp(s - m_new), 0.0)
        l_s[...] = alpha * l_s[...] + jnp.sum(p, axis=-1, keepdims=True)
        acc_s[...] = alpha * acc_s[...] + _dot(p.astype(BF16), v)
        m_s[...] = m_new

    m_s[...] = jnp.full(m_s.shape, NEG, F32)
    l_s[...] = jnp.zeros(l_s.shape, F32)
    acc_s[...] = jnp.zeros(acc_s.shape, F32)

    def far(kt, carry):
        update(_dot_nt(q, tile(sk_ref, kt)) * scale + bfar, sel_mask(kt), tile(sv_ref, kt))
        return carry

    lax.fori_loop(0, jnp.maximum(qb - 1, 0), far, 0)
    kt1 = jnp.maximum(qb - 1, 0)
    update(_dot_nt(q, tile(sk_ref, kt1)) * scale + bt_ref[0, 1],
           jnp.logical_and(sel_mask(kt1), qb >= 1), tile(sv_ref, kt1))
    update(_dot_nt(q, tile(sk_ref, qb)) * scale + bt_ref[0, 0],
           jnp.logical_and(sel_mask(qb), col_j <= row_i), tile(sv_ref, qb))
    l = l_s[...]
    o_s = acc_s[...] / jnp.where(l > 0, l, 1.0)

    nwin = WINDOW // qblk
    ss, masks, vs = [], [], []
    for u in range(nwin + 1):
        kt = qb - nwin + u
        ktc = jnp.maximum(kt, 0)
        delta = (nwin - u) * qblk
        bias = bt_ref[0, 0] if delta == 0 else (bt_ref[0, 1] if delta == qblk else bfar)
        s = _dot_nt(q, tile(wk_ref, ktc)) * scale + bias
        mask = kt >= 0
        if u == 0:
            mask = jnp.logical_and(mask, col_j >= row_i)
        elif u == nwin:
            mask = jnp.logical_and(mask, col_j <= row_i)
        else:
            mask = jnp.logical_and(mask, col_j >= 0)
        ss.append(jnp.where(mask, s, NEG))
        masks.append(mask)
        vs.append(tile(wv_ref, ktc))
    m = ss[0].max(axis=-1, keepdims=True)
    for s in ss[1:]:
        m = jnp.maximum(m, s.max(axis=-1, keepdims=True))
    lw = jnp.zeros((rows, 1), F32)
    ow = jnp.zeros((rows, HEAD_DIM), F32)
    for s, mask, v in zip(ss, masks, vs):
        p = jnp.where(mask, jnp.exp(s - m), 0.0)
        lw = lw + jnp.sum(p, axis=-1, keepdims=True)
        ow = ow + _dot(p.astype(BF16), v)
    o_w = ow / jnp.where(lw > 0, lw, 1.0)

    gt = gate_ref[0, 0]
    for g in range(GQ):
        r = slice(g * qblk, (g + 1) * qblk)
        o_ref[0, g] = (gt[:, 3 * g:3 * g + 1] * oc_ref[0, g] + gt[:, 3 * g + 1:3 * g + 2] * o_s[r]
                       + gt[:, 3 * g + 2:3 * g + 3] * o_w[r])


def _slc_win_attention(q_hm, sel, oc, gates_hm, sk, sv, wk, wv, bt, bfar):
    b, _, t, hd = q_hm.shape
    qblk = math.gcd(t, Q_BLOCK)
    nsp = sel.shape[-1]
    kvspec = pl.BlockSpec((1, 1, t, hd), lambda bb, k, i: (bb, k, 0, 0))
    qspec = pl.BlockSpec((1, GQ, qblk, hd), lambda bb, k, i: (bb, k, i, 0))
    return pl.pallas_call(
        functools.partial(_slc_win_kernel, qblk=qblk),
        grid=(b, N_KV_HEADS, t // qblk),
        in_specs=[qspec, pl.BlockSpec((1, 1, qblk, nsp), lambda bb, k, i: (bb, k, i, 0)), qspec,
                  pl.BlockSpec((1, 1, qblk, 3 * GQ), lambda bb, k, i: (bb, k, i, 0)),
                  kvspec, kvspec, kvspec, kvspec,
                  pl.BlockSpec((1,) + bt.shape[1:], lambda bb, k, i: (k, 0, 0, 0)),
                  pl.BlockSpec((1,) + bfar.shape[1:], lambda bb, k, i: (k, 0, 0))],
        out_specs=qspec,
        out_shape=jax.ShapeDtypeStruct(q_hm.shape, F32),
        scratch_shapes=[pltpu.VMEM((GQ * qblk, 1), F32), pltpu.VMEM((GQ * qblk, 1), F32),
                        pltpu.VMEM((GQ * qblk, hd), F32)],
        compiler_params=_cparams("parallel", "parallel", "parallel"),
        name="nsa_slc_win_attention",
    )(q_hm, sel, oc, gates_hm, sk, sv, wk, wv, bt, bfar)


def _nsa_prompt(q, gates, kv, rel_bias, cmp_w1, cmp_b1, cmp_w2, cmp_pos):
    b, t, _ = q.shape
    qblk = math.gcd(t, Q_BLOCK)
    assert qblk == Q_BLOCK and t % Q_BLOCK == 0 and WINDOW % Q_BLOCK == 0
    nch = t // CMP_STRIDE
    n_sel = t // SEL_BLOCK
    nsp = -(-n_sel // LANES) * LANES
    kv_hm = jnp.transpose(kv, (2, 0, 3, 1, 4))
    xc = kv_hm[:2].reshape(2, b * N_KV_HEADS, nch, CMP_STRIDE * HEAD_DIM)
    kvc = _compress(xc, cmp_w1, cmp_b1, cmp_w2, cmp_pos).reshape(2, b, N_KV_HEADS, nch, HEAD_DIM)
    kvc = jnp.pad(kvc, ((0, 0), (0, 0), (0, 0), (CMP_PAD, CMP_PAD), (0, 0)))
    tbl = _bias_by_dist(rel_bias)
    bfar = _stack_heads(jnp.broadcast_to(tbl[:, REL_MAX_DIST][:, None, None], (N_HEADS, qblk, 1)))
    bnear = _stack_heads(tbl[:, _dist_tile(qblk, CMP_NEAR, CMP_STRIDE * CMP_PAD - CMP_BLOCK + 1, CMP_STRIDE)])
    bt = jnp.stack([_stack_heads(tbl[:, _dist_tile(qblk, qblk, 0)]),
                    _stack_heads(tbl[:, _dist_tile(qblk, qblk, qblk)])], axis=1)
    ov = jnp.asarray(_overlap_matrix(nch, nsp), BF16)
    q_hm = jnp.transpose(q.reshape(b, t, N_HEADS, HEAD_DIM), (0, 2, 1, 3))
    gates_hm = jnp.transpose(gates.reshape(b, t, N_KV_HEADS, 3 * GQ), (0, 2, 1, 3))
    oc, sel = _cmp_attention(q_hm, kvc[0], kvc[1], ov, bnear, bfar, nch, min(SEL_TOP, n_sel))
    kvb = kv_hm[2:].astype(BF16)
    o_hm = _slc_win_attention(q_hm, sel, oc, gates_hm, kvb[0], kvb[1], kvb[2], kvb[3], bt, bfar)
    return jnp.transpose(o_hm, (0, 2, 1, 3)).reshape(b, t, N_HEADS * HEAD_DIM)


CMP_TAP_PAIRS = CMP_STRIDE // 2
PAGE_CHUNKS = PAGE_SIZE // CMP_STRIDE


def _paged_compress_kernel(pt_ref, cache_ref, wp_ref, w1_ref, pos_ref, b1_ref, w2_ref, o_ref, buf, carry, sem,
                           *, pg, npages, hid):
    b, g = pl.program_id(0), pl.program_id(1)
    m = pg * PAGE_CHUNKS
    hk = N_KV_HEADS * HEAD_DIM

    ngroups = buf.shape[0]

    def page_copy(p, lg):
        page = pt_ref[b * npages + g * pg + p]
        return pltpu.make_async_copy(cache_ref.at[page, :, pl.ds(lg * LANES, LANES)],
                                     buf.at[lg, pl.ds(p * PAGE_SIZE, PAGE_SIZE)], sem)

    for p in range(pg):
        for lg in range(ngroups):
            page_copy(p, lg).start()

    @pl.when(g == 0)
    def _():
        carry[...] = jnp.zeros_like(carry)

    for p in range(pg):
        for lg in range(ngroups):
            page_copy(p, lg).wait()

    row0 = lax.broadcasted_iota(I32, (m, hid), 0) == 0
    for comp in range(2):
        pt = _dot(pos_ref[comp].astype(BF16), w1_ref[comp])
        for hp in range(N_KV_HEADS // 2):
            lg = (comp * hk + hp * 2 * HEAD_DIM) // LANES
            acc = jnp.zeros((m, 2 * CMP_RATIO * hid), F32)
            for u in range(CMP_TAP_PAIRS):
                taps = [buf[lg, pl.ds(2 * u + e, m, stride=CMP_STRIDE), :] for e in range(2)]
                acc = acc + _dot(jnp.concatenate(taps, axis=1).astype(BF16), wp_ref[comp, u])
            for e in range(2):
                kvh = 2 * hp + e
                base = e * CMP_RATIO * hid
                p0 = acc[:, base:base + hid] + pt[0:1, :hid]
                p1 = acc[:, base + hid:base + 2 * hid] + pt[1:2, hid:]
                prev0 = jnp.where(row0, carry[comp, kvh], pltpu.roll(p0, 1, axis=0))
                carry[comp, kvh] = p0[m - 1:m, :]
                pre = prev0 + p1 + b1_ref[comp]
                o_ref[comp, 0, kvh] = _dot(jax.nn.gelu(pre).astype(BF16), w2_ref[comp])


def _paged_compress(page_table, cache_c, cmp_w1, cmp_b1, cmp_w2, cmp_pos):
    bs, npages = page_table.shape
    hid = cmp_w1.shape[-1]
    kdim = CMP_STRIDE * HEAD_DIM
    pg = math.gcd(npages, 32)
    m = pg * PAGE_CHUNKS
    w1 = cmp_w1.reshape(2, CMP_RATIO, kdim, hid)
    w1cat = jnp.concatenate([w1[:, r] for r in range(CMP_RATIO)], axis=-1)
    n1 = CMP_RATIO * hid
    wt = w1cat.reshape(2, CMP_TAP_PAIRS, 2, HEAD_DIM, n1)
    zero = jnp.zeros_like(wt)
    wp = jnp.concatenate([jnp.concatenate([wt, zero], axis=-1), jnp.concatenate([zero, wt], axis=-1)], axis=3)
    wp = wp.reshape(2, CMP_TAP_PAIRS, 4 * HEAD_DIM, 2 * n1).astype(BF16)
    pos = jnp.pad(cmp_pos.reshape(2, CMP_RATIO, kdim), ((0, 0), (0, SUBLANES - CMP_RATIO), (0, 0)))
    nch = npages * PAGE_CHUNKS
    grid_spec = pltpu.PrefetchScalarGridSpec(
        num_scalar_prefetch=1,
        grid=(bs, npages // pg),
        in_specs=[pl.BlockSpec(memory_space=pl.ANY),
                  pl.BlockSpec(wp.shape, lambda b, g, pt: (0, 0, 0, 0)),
                  pl.BlockSpec((2, kdim, n1), lambda b, g, pt: (0, 0, 0)),
                  pl.BlockSpec((2, SUBLANES, kdim), lambda b, g, pt: (0, 0, 0)),
                  pl.BlockSpec((2, 1, hid), lambda b, g, pt: (0, 0, 0)),
                  pl.BlockSpec((2, hid, HEAD_DIM), lambda b, g, pt: (0, 0, 0))],
        out_specs=pl.BlockSpec((2, 1, N_KV_HEADS, m, HEAD_DIM), lambda b, g, pt: (0, b, 0, g, 0)),
        scratch_shapes=[pltpu.VMEM((cache_c.shape[-1] // LANES, pg * PAGE_SIZE, LANES), F32),
                        pltpu.VMEM((2, N_KV_HEADS, 1, hid), F32), pltpu.SemaphoreType.DMA(())],
    )
    return pl.pallas_call(
        functools.partial(_paged_compress_kernel, pg=pg, npages=npages, hid=hid),
        grid_spec=grid_spec,
        out_shape=jax.ShapeDtypeStruct((2, bs, N_KV_HEADS, nch, HEAD_DIM), F32),
        compiler_params=_cparams("parallel", "arbitrary"),
        name="nsa_paged_compress",
    )(page_table.reshape(-1), cache_c, wp, w1cat.astype(BF16), pos, cmp_b1.reshape(2, 1, hid), cmp_w2.astype(BF16))


def _rows8(x):
    return jnp.concatenate([x, jnp.zeros((SUBLANES - x.shape[0], x.shape[1]), x.dtype)], axis=0)


def _sample_cmp_kernel(q_ref, kc_ref, vc_ref, ov_ref, bias_ref, oc_ref, idx_ref, *, nch, n_sel, n_past_blk, ntop):
    scale = HEAD_DIM ** -0.5
    nsp = ov_ref.shape[1]
    row = lax.broadcasted_iota(I32, (SUBLANES, nch), 0)
    lane = lax.broadcasted_iota(I32, (SUBLANES, nch), 1)
    mask = jnp.logical_and(lane >= 1, row < GQ)
    imp = jnp.zeros((SUBLANES, nsp), F32)
    row_s = lax.broadcasted_iota(I32, (SUBLANES, nsp), 0)
    for kvh in range(N_KV_HEADS):
        q8 = _rows8(q_ref[0, kvh * GQ:(kvh + 1) * GQ, :])
        s = jnp.where(mask, _dot3_nt(q8, kc_ref[0, kvh]) * scale + bias_ref[kvh], NEG)
        mx = jnp.max(s, axis=-1, keepdims=True)
        p = jnp.where(mask, jnp.exp(s - mx), 0.0)
        l = jnp.sum(p, axis=-1, keepdims=True)
        linv = 1.0 / jnp.where(l > 0, l, 1.0)
        o = _dot(p.astype(BF16), vc_ref[0, kvh].astype(BF16)) * linv
        oc_ref[0, kvh * GQ:(kvh + 1) * GQ, :] = o[0:GQ]
        p_hi, p_lo = _split(p * linv)
        contrib = _dot(p_hi, ov_ref[...]) + _dot(p_lo, ov_ref[...])
        imp = jnp.where(row_s == kvh, jnp.sum(contrib, axis=0, keepdims=True), imp)
    s_io = lax.broadcasted_iota(I32, (SUBLANES, nsp), 1)
    t = n_past_blk * SEL_BLOCK
    cur = t // SEL_BLOCK
    forced = jnp.logical_or(s_io == 0, jnp.logical_or(s_io == cur, s_io == cur - 1))
    valid = jnp.logical_and(s_io * SEL_BLOCK <= t, s_io < n_sel)
    score = jnp.where(valid, imp + jnp.where(forced, FORCE_SCORE, 0.0), -jnp.inf)
    lane_f = s_io.astype(F32)
    out_lane = lax.broadcasted_iota(I32, (SUBLANES, LANES), 1)
    picks = jnp.zeros((SUBLANES, LANES), F32)
    for it in range(ntop):
        mx = jnp.max(score, axis=-1, keepdims=True)
        idx = jnp.min(jnp.where(score == mx, lane_f, float(nsp)), axis=-1, keepdims=True)
        picks = jnp.where(out_lane == it, idx, picks)
        score = jnp.where(lane_f == idx, -jnp.inf, score)
    idx_ref[0] = picks.astype(I32)


def _sample_cmp_attention(q_hm, kc, vc, ov, bias, n_sel, n_past_blk, ntop):
    bs = q_hm.shape[0]
    nch = kc.shape[2]
    kvspec = pl.BlockSpec((1, N_KV_HEADS, nch, HEAD_DIM), lambda b: (b, 0, 0, 0))
    return pl.pallas_call(
        functools.partial(_sample_cmp_kernel, nch=nch, n_sel=n_sel, n_past_blk=n_past_blk, ntop=ntop),
        grid=(bs,),
        in_specs=[pl.BlockSpec((1, N_HEADS, HEAD_DIM), lambda b: (b, 0, 0)), kvspec, kvspec,
                  pl.BlockSpec(ov.shape, lambda b: (0, 0)), pl.BlockSpec(bias.shape, lambda b: (0, 0, 0))],
        out_specs=[pl.BlockSpec((1, N_HEADS, HEAD_DIM), lambda b: (b, 0, 0)),
                   pl.BlockSpec((1, SUBLANES, LANES), lambda b: (b, 0, 0))],
        out_shape=[jax.ShapeDtypeStruct(q_hm.shape, F32), jax.ShapeDtypeStruct((bs, SUBLANES, LANES), I32)],
        compiler_params=_cparams("parallel"),
        name="nsa_sample_cmp_attention",
    )(q_hm, kc, vc, ov, bias)


def _sample_slc_win_kernel(page_ref, row_ref, new_ref, d0_ref, case_ref,
                           q_ref, oc_ref, gate_ref, cache_ref, newblk_ref, win_ref, winnew_ref,
                           bslc_ref, bwin_ref, bwin_new_ref, o_ref, buf, sem, *, ntop):
    b = pl.program_id(0)
    scale = HEAD_DIM ** -0.5
    hk = N_KV_HEADS * HEAD_DIM
    nslots = N_KV_HEADS * ntop

    def block_copy(slot):
        i = b * nslots + slot
        row = pl.multiple_of(row_ref[i], SEL_BLOCK)
        return pltpu.make_async_copy(cache_ref.at[page_ref[i], pl.ds(row, SEL_BLOCK)], buf.at[slot], sem)

    for slot in range(nslots):
        is_new = new_ref[b * nslots + slot] == 1

        @pl.when(jnp.logical_not(is_new))
        def _():
            block_copy(slot).start()

        @pl.when(is_new)
        def _():
            buf[slot] = newblk_ref[0]

    for slot in range(nslots):
        @pl.when(new_ref[b * nslots + slot] == 0)
        def _():
            block_copy(slot).wait()

    lane64 = lax.broadcasted_iota(I32, (SUBLANES, SEL_BLOCK), 1)
    gt = gate_ref[0]
    for kvh in range(N_KV_HEADS):
        hs = slice(kvh * GQ, (kvh + 1) * GQ)
        ks = slice(kvh * HEAD_DIM, (kvh + 1) * HEAD_DIM)
        vsl = slice(hk + kvh * HEAD_DIM, hk + (kvh + 1) * HEAD_DIM)
        q8f = _rows8(q_ref[0, hs, :])
        q8 = q8f.astype(BF16)
        ss, masks = [], []
        for j in range(ntop):
            slot = kvh * ntop + j
            i = b * nslots + slot
            kj = buf[slot, :, ks].astype(BF16)
            s = _dot_nt(q8, kj) * scale + bslc_ref[kvh, case_ref[i]]
            mask = lane64 <= d0_ref[i]
            ss.append(jnp.where(mask, s, NEG))
            masks.append(mask)
        mx = ss[0].max(axis=-1, keepdims=True)
        for s in ss[1:]:
            mx = jnp.maximum(mx, s.max(axis=-1, keepdims=True))
        l = jnp.zeros((SUBLANES, 1), F32)
        acc = jnp.zeros((SUBLANES, HEAD_DIM), F32)
        for j in range(ntop):
            p = jnp.where(masks[j], jnp.exp(ss[j] - mx), 0.0)
            l = l + jnp.sum(p, axis=-1, keepdims=True)
            acc = acc + _dot(p.astype(BF16), buf[kvh * ntop + j, :, vsl].astype(BF16))
        o_s = acc / jnp.where(l > 0, l, 1.0)
        kw = win_ref[0, :, ks].astype(BF16)
        sw = _dot_nt(q8, kw) * scale + bwin_ref[kvh]
        s_new = jnp.sum(q8f * winnew_ref[0, 0:1, ks], axis=-1, keepdims=True) * scale + bwin_new_ref[kvh]
        mw = jnp.maximum(sw.max(axis=-1, keepdims=True), s_new)
        pw = jnp.exp(sw - mw)
        p_new = jnp.exp(s_new - mw)
        lw = jnp.sum(pw, axis=-1, keepdims=True) + p_new
        o_w = (_dot(pw.astype(BF16), win_ref[0, :, vsl].astype(BF16)) + p_new * winnew_ref[0, 0:1, vsl]) / lw
        g3 = gt[hs, :]
        o_ref[0, hs, :] = g3[:, 0:1] * oc_ref[0, hs, :] + g3[:, 1:2] * o_s[0:GQ] + g3[:, 2:3] * o_w[0:GQ]


def _sample_slc_win(picks, q_hm, oc, gates_h, cache_s, newblk, winbuf, winnew, bslc, bwin, bwin_new, ntop):
    bs = q_hm.shape[0]
    wb = winbuf.shape[1]
    width = cache_s.shape[-1]
    hspec = pl.BlockSpec((1, N_HEADS, HEAD_DIM), lambda b, *_: (b, 0, 0))
    grid_spec = pltpu.PrefetchScalarGridSpec(
        num_scalar_prefetch=5,
        grid=(bs,),
        in_specs=[hspec, hspec, pl.BlockSpec((1, N_HEADS, 3), lambda b, *_: (b, 0, 0)),
                  pl.BlockSpec(memory_space=pl.ANY),
                  pl.BlockSpec((1, SEL_BLOCK, width), lambda b, *_: (b, 0, 0)),
                  pl.BlockSpec((1, wb, width), lambda b, *_: (b, 0, 0)),
                  pl.BlockSpec((1, SUBLANES, width), lambda b, *_: (b, 0, 0)),
                  pl.BlockSpec(bslc.shape, lambda b, *_: (0, 0, 0, 0)),
                  pl.BlockSpec(bwin.shape, lambda b, *_: (0, 0, 0)),
                  pl.BlockSpec(bwin_new.shape, lambda b, *_: (0, 0, 0))],
        out_specs=hspec,
        scratch_shapes=[pltpu.VMEM((N_KV_HEADS * ntop, SEL_BLOCK, width), F32), pltpu.SemaphoreType.DMA(())],
    )
    return pl.pallas_call(
        functools.partial(_sample_slc_win_kernel, ntop=ntop),
        grid_spec=grid_spec,
        out_shape=jax.ShapeDtypeStruct(q_hm.shape, F32),
        compiler_params=_cparams("arbitrary"),
        name="nsa_sample_slc_win",
    )(*picks, q_hm, oc, gates_h, cache_s, newblk, winbuf, winnew, bslc, bwin, bwin_new)


def _nsa_sample(q, gates, kv_new, win_buf, cache, page_table, rel_bias, cmp_w1, cmp_b1, cmp_w2, cmp_pos):
    bs, npages = page_table.shape
    pool = cache.shape[0]
    hk = N_KV_HEADS * HEAD_DIM
    past = npages * PAGE_SIZE
    nch = past // CMP_STRIDE
    n_past_blk = past // SEL_BLOCK
    n_sel = n_past_blk + 1
    ntop = min(SEL_TOP, n_sel)
    nsp = -(-n_sel // LANES) * LANES
    wb = win_buf.shape[1]
    assert n_sel >= SEL_TOP and past >= wb
    cache_c = cache[:, :, :2].reshape(pool, PAGE_SIZE, 2 * hk)
    cache_s = cache[:, :, 2:].reshape(pool, PAGE_SIZE, 2 * hk)
    kvc = _paged_compress(page_table, cache_c, cmp_w1, cmp_b1, cmp_w2, cmp_pos)
    tbl = _bias_by_dist(rel_bias)
    tok = np.arange(nch) - 1
    dist_c = np.clip(past - (tok * CMP_STRIDE + CMP_BLOCK - 1), 0, REL_MAX_DIST)
    bias_c = jnp.pad(tbl[:, dist_c].reshape(N_KV_HEADS, GQ, nch), ((0, 0), (0, SUBLANES - GQ), (0, 0)))
    ov = np.zeros((nch, nsp), np.float32)
    ov[1:] = _overlap_matrix(nch - 1, nsp)[CMP_PAD:-CMP_PAD]
    q_hm = q.reshape(bs, N_HEADS, HEAD_DIM)
    oc, idx = _sample_cmp_attention(q_hm, kvc[0], kvc[1], jnp.asarray(ov, BF16), bias_c, n_sel, n_past_blk, ntop)
    idx = idx[:, :N_KV_HEADS, :ntop]
    ip = jnp.minimum(idx, n_past_blk - 1)
    per_page = PAGE_SIZE // SEL_BLOCK
    page = jnp.take_along_axis(page_table, (ip // per_page).reshape(bs, -1), axis=1)
    row0 = (ip % per_page) * SEL_BLOCK
    is_new = (idx >= n_past_blk).astype(I32)
    d0 = (n_past_blk - idx) * SEL_BLOCK
    case = jnp.clip(n_past_blk - idx, 0, 3)
    picks = [a.reshape(-1).astype(I32) for a in (page, row0, is_new, d0, case)]
    dist_s = np.stack([_dist_tile(1, SEL_BLOCK, cs * SEL_BLOCK)[0] for cs in range(4)])
    bslc = jnp.pad(jnp.transpose(tbl[:, dist_s].reshape(N_KV_HEADS, GQ, 4, SEL_BLOCK), (0, 2, 1, 3)),
                   ((0, 0), (0, 0), (0, SUBLANES - GQ), (0, 0)))
    dist_w = np.clip(wb - np.arange(wb), 0, REL_MAX_DIST)
    bwin = jnp.pad(tbl[:, dist_w].reshape(N_KV_HEADS, GQ, wb), ((0, 0), (0, SUBLANES - GQ), (0, 0)))
    bwin_new = jnp.pad(tbl[:, 0].reshape(N_KV_HEADS, GQ, 1), ((0, 0), (0, SUBLANES - GQ), (0, 0)))
    new_slc = kv_new[:, 2:4].reshape(bs, 1, 2 * hk)
    newblk = jnp.pad(new_slc, ((0, 0), (0, SEL_BLOCK - 1), (0, 0)))
    winnew = jnp.pad(kv_new[:, 4:6].reshape(bs, 1, 2 * hk), ((0, 0), (0, SUBLANES - 1), (0, 0)))
    winbuf = win_buf.reshape(bs, wb, 2 * hk)
    o = _sample_slc_win(picks, q_hm, oc, gates.reshape(bs, N_HEADS, 3), cache_s, newblk, winbuf, winnew,
                        bslc, bwin, bwin_new, ntop)
    return o.reshape(bs, N_HEADS * HEAD_DIM)


def _mods(mod_l, nb, mr):
    return [m.reshape(nb, mr, -1) for m in jnp.split(mod_l, 6, axis=-1)]


def _s5_layer(x, nb, rb, seq, norm_g, mods, s5w, dsk, w_glu_bf16, h0r, h0i):
    sh1, sc1, g1 = mods[0], mods[1], mods[2]
    d = x.shape[-1]
    ar, ai, wbr, wbi, wcr, wci = s5w
    act, st_r, st_i = _s5_scan(x.reshape(nb, rb, d), norm_g.reshape(1, d), sh1, sc1, dsk.reshape(1, d),
                               wbr, wbi, ar, ai, wcr, wci, h0r, h0i, seq=seq)
    return _glu_residual(act.reshape(-1, d), x, g1, w_glu_bf16, rb), st_r, st_i


def kernel(x_prompt, x_sample, c_prompt, c_sample, state_s5, cache_nsa_kv, cache_win_kv, page_table, w_mod, b_mod, norm_g, s5_lambda_re, s5_lambda_im, s5_log_dt, s5_b_re, s5_b_im, s5_c_re, s5_c_im, s5_d, s5_w_glu, kv_mod_w, kv_mod_b, kv_norm_g, w_kv, cmp_w1, cmp_b1, cmp_w2, cmp_pos, nsa_w_in, nsa_w_out, rel_bias, moe_w_router, moe_b_router, moe_w_gu, moe_b_gu, moe_w_down, moe_b_down, final_norm_g):
    bp, lp, d = x_prompt.shape
    bs, ls, _ = x_sample.shape
    assert ls == 1, "the sample path advances exactly one token per sequence"
    depth = w_mod.shape[0]
    n_a = s5_lambda_re.shape[0]
    hq = N_HEADS * HEAD_DIM
    groups = d // S5_GROUP

    c_all = jnp.concatenate([c_prompt, c_sample], axis=0)
    mod_all = _adaln(c_all, w_mod, b_mod)
    kvmod_all = _adaln(c_all, kv_mod_w[None], kv_mod_b[None])[0]
    s5ws = [_s5_weights(s5_lambda_re[i], s5_lambda_im[i], s5_log_dt[i], s5_b_re[i], s5_b_im[i],
                        s5_c_re[i], s5_c_im[i]) for i in range(n_a)]
    wglus = [s5_w_glu[i].astype(BF16) for i in range(n_a)]
    w_kv_b = w_kv.astype(BF16)
    n_in = nsa_w_in.shape[-1]
    n_in_pad = -(-n_in // LANES) * LANES
    w_ins = [jnp.pad(nsa_w_in[j], ((0, 0), (0, n_in_pad - n_in))).astype(BF16) for j in range(depth - n_a)]
    w_outs = [nsa_w_out[j].astype(BF16) for j in range(depth - n_a)]

    def run(x3, sl, seq, h0_all, mixer):
        if seq:
            nb, rb = x3.shape[0], x3.shape[1]
            mr = 1
        else:
            nb, rb = 1, x3.shape[0]
            mr = rb
        x = x3.reshape(-1, d)
        states = []
        kv = None
        for i in range(depth):
            mods = _mods(mod_all[i, sl], nb, mr)
            if i < n_a:
                ns = s5ws[i][0].shape[-1]
                h0r = h0_all[i][..., 0].reshape(nb, -1, ns)
                h0i = h0_all[i][..., 1].reshape(nb, -1, ns)
                x, sr, si = _s5_layer(x, nb, rb, seq, norm_g[i, 0], mods, s5ws[i], s5_d[i], wglus[i], h0r, h0i)
                states.append(jnp.stack([sr.reshape(-1, groups, S5_STATE), si.reshape(-1, groups, S5_STATE)], axis=-1))
            else:
                j = i - n_a
                if kv is None:
                    kvsh, kvsc = [m.reshape(nb, mr, d) for m in jnp.split(kvmod_all[sl], 2, axis=-1)]
                    kv = _normmod_matmul(x, rb, kv_norm_g.reshape(1, d), kvsh, kvsc, w_kv_b)
                proj = _normmod_matmul(x, rb, norm_g[i, 0].reshape(1, d), mods[0], mods[1], w_ins[j], sig_from=hq)
                o = mixer(proj[:, :hq], proj[:, hq:n_in], kv)
                x = _matmul_residual(o, x, mods[2], w_outs[j], rb)
            x = _moe(x, rb, norm_g[i, 1].reshape(1, d), mods[3], mods[4], mods[5], moe_w_router[i], moe_b_router[i],
                     moe_w_gu[i], moe_b_gu[i], moe_w_down[i], moe_b_down[i])
        return _final_norm(x, final_norm_g, rb), jnp.stack(states), kv

    def prompt_mixer(q, gates, kv):
        o = _nsa_prompt(q.reshape(bp, lp, hq), gates.reshape(bp, lp, -1),
                        kv.reshape(bp, lp, N_BRANCH_KV, N_KV_HEADS, HEAD_DIM), rel_bias, cmp_w1, cmp_b1, cmp_w2, cmp_pos)
        return o.reshape(bp * lp, hq)

    def sample_mixer(q, gates, kv):
        return _nsa_sample(q, gates, kv.reshape(bs, N_BRANCH_KV, N_KV_HEADS, HEAD_DIM), cache_win_kv, cache_nsa_kv,
                           page_table, rel_bias, cmp_w1, cmp_b1, cmp_w2, cmp_pos)

    h0_p = jnp.zeros((n_a, bp, groups, S5_STATE, 2), F32)
    y_p, st_p, kv_p = run(x_prompt, slice(0, bp), True, h0_p, prompt_mixer)
    y_s, st_s, kv_s = run(x_sample.reshape(bs, d), slice(bp, bp + bs), False, state_s5, sample_mixer)
    kv_p = kv_p.reshape(bp, lp, N_BRANCH_KV, N_KV_HEADS, HEAD_DIM)
    kv_s = kv_s.reshape(bs, ls, N_BRANCH_KV, N_KV_HEADS, HEAD_DIM)
    win_p = kv_p[:, -min(WINDOW, lp):, 4:6]
    win_s = jnp.concatenate([cache_win_kv, kv_s[:, :, 4:6]], axis=1)[:, -cache_win_kv.shape[1]:]
    return (y_p.reshape(bp, lp, d), y_s.reshape(bs, ls, d), st_p, st_s, kv_p[:, :, :4], kv_s[:, :, :4], win_p, win_s)
```

```python
import functools
import math

import numpy as np
import jax
import jax.numpy as jnp
from jax import lax
from jax.experimental import pallas as pl
from jax.experimental.pallas import tpu as pltpu

F32 = jnp.float32
BF16 = jnp.bfloat16
I32 = jnp.int32

S5_GROUP = 16
S5_STATE = 64
N_HEADS = 16
HEAD_DIM = 64
N_KV_HEADS = 4
GQ = N_HEADS // N_KV_HEADS
CMP_BLOCK = 32
CMP_STRIDE = 16
CMP_RATIO = CMP_BLOCK // CMP_STRIDE
SEL_BLOCK = 64
SEL_TOP = 16
WINDOW = 512
Q_BLOCK = 128
FORCE_SCORE = 1e4
N_BRANCH_KV = 6
REL_BUCKETS = 32
REL_MAX_DIST = 128
N_EXPERTS = 32
TOP_K = 4
SWIGLU_LIMIT = 7.0
SWIGLU_ALPHA = 1.702
RMS_EPS = 1e-5
PAGE_SIZE = 128

LANES = 128
SUBLANES = 8
VMEM_LIMIT = 56 * 1024 * 1024

MOE_ROWS = 256
FAR_GROUP = 4
S5_LANES = 256
S5_STATES = S5_LANES // S5_GROUP * S5_STATE
NEG = -1e30


def _cparams(*sem):
    return pltpu.CompilerParams(dimension_semantics=sem, vmem_limit_bytes=VMEM_LIMIT)


def _normmod(x, g, sh, sc):
    y = x * lax.rsqrt(jnp.mean(x * x, axis=-1, keepdims=True) + RMS_EPS)
    return (y * g) * (1.0 + sc) + sh


def _dot(a, b):
    return jnp.dot(a, b, preferred_element_type=F32)


def _dot_nt(a, b):
    return lax.dot_general(a, b, (((1,), (1,)), ((), ())), preferred_element_type=F32)


def _split(x):
    hi = x.astype(BF16)
    lo = (x - hi.astype(F32)).astype(BF16)
    return hi, lo


def _mod_spec(m, rb, tm):
    nb, mr, d = m.shape
    if mr == 1:
        return pl.BlockSpec((1, 1, d), lambda i, *_: ((i * tm) // rb, 0, 0))
    assert mr == rb and rb % tm == 0
    per = rb // tm
    return pl.BlockSpec((1, tm, d), lambda i, *_: (i // per, i % per, 0))


def _row_tile(rb, want):
    t = math.gcd(rb, want)
    return t


def _adaln_kernel(c_ref, w_ref, b_ref, o_ref):
    c = c_ref[...]
    a = (c * jax.nn.sigmoid(c)).astype(BF16)
    o_ref[0] = _dot(a, w_ref[0].astype(BF16)) + b_ref[0]


def _adaln(c, w, b):
    n, d = c.shape
    nl, _, nn = w.shape
    npad = -(-n // SUBLANES) * SUBLANES
    cp = jnp.pad(c, ((0, npad - n), (0, 0)))
    tn = math.gcd(nn, 1024)
    out = pl.pallas_call(
        _adaln_kernel,
        grid=(nl, nn // tn),
        in_specs=[pl.BlockSpec((npad, d), lambda l, j: (0, 0)),
                  pl.BlockSpec((1, d, tn), lambda l, j: (l, 0, j)),
                  pl.BlockSpec((1, 1, tn), lambda l, j: (l, 0, j))],
        out_specs=pl.BlockSpec((1, npad, tn), lambda l, j: (l, 0, j)),
        out_shape=jax.ShapeDtypeStruct((nl, npad, nn), F32),
        compiler_params=_cparams("parallel", "parallel"),
        name="adaln",
    )(cp, w, b.reshape(nl, 1, nn))
    return out[:, :n]


def _s5_disc_kernel(lr_ref, li_ref, ldt_ref, br_ref, bi_ref, ar_ref, ai_ref, bbr_ref, bbi_ref):
    lr, li = lr_ref[...], li_ref[...]
    dt = jnp.exp(ldt_ref[...])
    mag = jnp.exp(dt * lr)
    ar, ai = mag * jnp.cos(dt * li), mag * jnp.sin(dt * li)
    den = lr * lr + li * li
    nr, ni = ar - 1.0, ai
    fr = (nr * lr + ni * li) / den
    fi = (ni * lr - nr * li) / den
    br, bi = br_ref[...], bi_ref[...]
    ar_ref[...] = ar
    ai_ref[...] = ai
    bbr_ref[...] = fr * br - fi * bi
    bbi_ref[...] = fr * bi + fi * br


def _s5_discretize(lam_re, lam_im, log_dt, b_re, b_im):
    g, p = lam_re.shape
    c = b_re.shape[-1]
    full3 = lambda s: pl.BlockSpec(s, lambda: (0, 0, 0))
    return pl.pallas_call(
        _s5_disc_kernel,
        in_specs=[full3((g, p, 1)), full3((g, p, 1)), full3((g, 1, 1)), full3((g, p, c)), full3((g, p, c))],
        out_specs=[full3((g, p, 1)), full3((g, p, 1)), full3((g, p, c)), full3((g, p, c))],
        out_shape=[jax.ShapeDtypeStruct((g, p, 1), F32)] * 2 + [jax.ShapeDtypeStruct((g, p, c), F32)] * 2,
        name="s5_discretize",
    )(lam_re.reshape(g, p, 1), lam_im.reshape(g, p, 1), log_dt.reshape(g, 1, 1), b_re, b_im)


def _s5_kernel(x_ref, g_ref, sh_ref, sc_ref, d_ref, wbr_ref, wbi_ref, ar_ref, ai_ref, wcr_ref, wci_ref,
               h0r_ref, h0i_ref, y_ref, str_ref, sti_ref, bur, bui, sre, sim, car_r, car_i, *, seq, rows, nj):
    c = pl.program_id(1)
    h = _normmod(x_ref[0], g_ref[...], sh_ref[0], sc_ref[0])
    dsk = d_ref[...]
    for j in range(nj):
        lo, hi = j * S5_LANES, (j + 1) * S5_LANES
        slo, shi = j * S5_STATES, (j + 1) * S5_STATES
        hj = h[:, lo:hi]
        hb = hj.astype(BF16)
        ar, ai = ar_ref[:, slo:shi], ai_ref[:, slo:shi]
        if seq:
            @pl.when(c == 0)
            def _():
                car_r[j] = h0r_ref[0, :, slo:shi]
                car_i[j] = h0i_ref[0, :, slo:shi]
            bur[...] = _dot(hb, wbr_ref[j])
            bui[...] = _dot(hb, wbi_ref[j])

            def step(t, carry):
                sr, si = carry
                br = bur[pl.ds(t, 1), :]
                bi = bui[pl.ds(t, 1), :]
                nr = ar * sr - ai * si + br
                ni = ar * si + ai * sr + bi
                sre[pl.ds(t, 1), :] = nr
                sim[pl.ds(t, 1), :] = ni
                return nr, ni

            sr, si = lax.fori_loop(0, rows, step, (car_r[j], car_i[j]), unroll=8)
            car_r[j] = sr
            car_i[j] = si
            str_ref[0, :, slo:shi] = sr
            sti_ref[0, :, slo:shi] = si
            s_r, s_i = sre[...], sim[...]
        else:
            h0r, h0i = h0r_ref[0, :, slo:shi], h0i_ref[0, :, slo:shi]
            s_r = ar * h0r - ai * h0i + _dot(hb, wbr_ref[j])
            s_i = ar * h0i + ai * h0r + _dot(hb, wbi_ref[j])
            str_ref[0, :, slo:shi] = s_r
            sti_ref[0, :, slo:shi] = s_i
        y = _dot(s_r.astype(BF16), wcr_ref[j]) - _dot(s_i.astype(BF16), wci_ref[j])
        y_ref[0, :, lo:hi] = jax.nn.gelu(y + dsk[:, lo:hi] * hj)


def _s5_scan(x3, g, sh, sc, dsk, wbr, wbi, ar, ai, wcr, wci, h0r, h0i, *, seq):
    nb, rb, d = x3.shape
    ns = ar.shape[-1]
    nj = d // S5_LANES
    rows = _row_tile(rb, 256)
    hr = 1 if seq else rows
    mrow = sh.shape[1]
    mspec = (pl.BlockSpec((1, 1, d), lambda b, c: (b, 0, 0)) if mrow == 1
             else pl.BlockSpec((1, rows, d), lambda b, c: (b, c, 0)))
    hspec = (pl.BlockSpec((1, 1, ns), lambda b, c: (b, 0, 0)) if seq
             else pl.BlockSpec((1, rows, ns), lambda b, c: (b, c, 0)))
    const2 = lambda s: pl.BlockSpec(s, lambda b, c: (0, 0))
    const3 = lambda s: pl.BlockSpec(s, lambda b, c: (0, 0, 0))
    kern = functools.partial(_s5_kernel, seq=seq, rows=rows, nj=nj)
    return pl.pallas_call(
        kern,
        grid=(nb, rb // rows),
        in_specs=[pl.BlockSpec((1, rows, d), lambda b, c: (b, c, 0)), const2((1, d)), mspec, mspec, const2((1, d)),
                  const3(wbr.shape), const3(wbi.shape), const2((1, ns)), const2((1, ns)),
                  const3(wcr.shape), const3(wci.shape), hspec, hspec],
        out_specs=[pl.BlockSpec((1, rows, d), lambda b, c: (b, c, 0)), hspec, hspec],
        out_shape=[jax.ShapeDtypeStruct((nb, rb, d), F32),
                   jax.ShapeDtypeStruct(h0r.shape, F32), jax.ShapeDtypeStruct(h0i.shape, F32)],
        scratch_shapes=[pltpu.VMEM((rows, S5_STATES), F32)] * 4 + [pltpu.VMEM((nj, 1, S5_STATES), F32)] * 2,
        compiler_params=_cparams("parallel", "arbitrary"),
        name="s5_scan" if seq else "s5_step",
    )(x3, g, sh, sc, dsk, wbr, wbi, ar, ai, wcr, wci, h0r, h0i)


def _s5_weights(lam_re, lam_im, log_dt, b_re, b_im, c_re, c_im):
    ar, ai, bbr, bbi = _s5_discretize(lam_re, lam_im, log_dt, b_re, b_im)
    g, p, c = bbr.shape
    gs = S5_LANES // c
    nj = g // gs
    eye = jnp.eye(gs, dtype=F32)

    def wb(bb):
        w = jnp.einsum('jgpc,gh->jgchp', bb.reshape(nj, gs, p, c), eye)
        return w.reshape(nj, gs * c, gs * p).astype(BF16)

    def wc(cc):
        w = jnp.einsum('jgcp,gh->jgphc', cc.reshape(nj, gs, c, p), eye)
        return w.reshape(nj, gs * p, gs * c).astype(BF16)

    return (ar.reshape(1, g * p), ai.reshape(1, g * p), wb(bbr), wb(bbi), wc(c_re), wc(c_im))


def _glu_kernel(a_ref, x_ref, g1_ref, w_ref, o_ref, *, d):
    gu = _dot(a_ref[...].astype(BF16), w_ref[...])
    o_ref[...] = x_ref[...] + g1_ref[0] * (gu[:, :d] * jax.nn.sigmoid(gu[:, d:]))


def _glu_residual(a, x, g1, w_bf16, rb):
    t, d = x.shape
    tm = _row_tile(rb, 512)
    return pl.pallas_call(
        functools.partial(_glu_kernel, d=d),
        grid=(t // tm,),
        in_specs=[pl.BlockSpec((tm, d), lambda i: (i, 0)), pl.BlockSpec((tm, d), lambda i: (i, 0)),
                  _mod_spec(g1, rb, tm), pl.BlockSpec(w_bf16.shape, lambda i: (0, 0))],
        out_specs=pl.BlockSpec((tm, d), lambda i: (i, 0)),
        out_shape=jax.ShapeDtypeStruct((t, d), F32),
        compiler_params=_cparams("parallel"),
        name="glu_residual",
    )(a, x, g1, w_bf16)


def _router_kernel(x_ref, g_ref, sh_ref, sc_ref, w_ref, b_ref, ti_ref, tw_ref, rk_ref, cnt_ref, carry, *, tm):
    i = pl.program_id(0)

    @pl.when(i == 0)
    def _():
        carry[...] = jnp.zeros_like(carry)

    h = _normmod(x_ref[...], g_ref[...], sh_ref[0], sc_ref[0])
    h_hi, h_lo = _split(h)
    w_hi, w_lo = _split(w_ref[...])
    logits = _dot(h_hi, w_hi) + (_dot(h_lo, w_hi) + _dot(h_hi, w_lo)) + b_ref[...]
    lane = lax.broadcasted_iota(I32, (tm, LANES), 1)
    lane_f = lane.astype(F32)
    work = jnp.where(lane < N_EXPERTS, logits, -jnp.inf)
    vals, sels, idxs = [], [], []
    for _ in range(TOP_K):
        m = jnp.max(work, axis=-1, keepdims=True)
        idx = jnp.min(jnp.where(work == m, lane_f, float(LANES)), axis=-1, keepdims=True)
        sel = lane_f == idx
        work = jnp.where(sel, -jnp.inf, work)
        vals.append(m)
        sels.append(sel)
        idxs.append(idx)
    es = [jnp.exp(v - vals[0]) for v in vals]
    tot = es[0] + es[1] + es[2] + es[3]
    oh = jnp.zeros((tm, LANES), F32)
    for sel in sels:
        oh = oh + sel.astype(F32)
    r_io = lax.broadcasted_iota(I32, (tm, tm), 0)
    c_io = lax.broadcasted_iota(I32, (tm, tm), 1)
    lstrict = (c_io < r_io).astype(BF16)
    base = carry[...] + _dot(lstrict, oh.astype(BF16))
    ti = jnp.zeros((tm, LANES), F32)
    tw = jnp.zeros((tm, LANES), F32)
    rk = jnp.zeros((tm, LANES), F32)
    for k in range(TOP_K):
        rank = jnp.sum(jnp.where(sels[k], base, 0.0), axis=-1, keepdims=True)
        ti = jnp.where(lane == k, idxs[k], ti)
        tw = jnp.where(lane == k, es[k] / tot, tw)
        rk = jnp.where(lane == k, rank, rk)
    ti_ref[...] = ti.astype(I32)
    tw_ref[...] = tw
    rk_ref[...] = rk.astype(I32)
    carry[...] = carry[...] + jnp.sum(oh, axis=0, keepdims=True)
    cnt_ref[...] = carry[...]


def _router(x, rb, g, sh, sc, w_pad, b_pad):
    t, d = x.shape
    tm = _row_tile(rb, 512)
    row = lambda i: (i, 0)
    return pl.pallas_call(
        functools.partial(_router_kernel, tm=tm),
        grid=(t // tm,),
        in_specs=[pl.BlockSpec((tm, d), row), pl.BlockSpec((1, d), lambda i: (0, 0)),
                  _mod_spec(sh, rb, tm), _mod_spec(sc, rb, tm),
                  pl.BlockSpec((d, LANES), lambda i: (0, 0)), pl.BlockSpec((1, LANES), lambda i: (0, 0))],
        out_specs=[pl.BlockSpec((tm, LANES), row)] * 3 + [pl.BlockSpec((1, LANES), lambda i: (0, 0))],
        out_shape=[jax.ShapeDtypeStruct((t, LANES), I32), jax.ShapeDtypeStruct((t, LANES), F32),
                   jax.ShapeDtypeStruct((t, LANES), I32), jax.ShapeDtypeStruct((1, LANES), F32)],
        scratch_shapes=[pltpu.VMEM((1, LANES), F32)],
        compiler_params=_cparams("arbitrary"),
        name="moe_router",
    )(x, g, sh, sc, w_pad, b_pad)


def _row_copy(src, dst, sem):
    return pltpu.make_async_copy(src, dst, sem)


def _dispatch_kernel(dest_ref, x_ref, g_ref, sh_ref, sc_ref, xs_in_ref, xs_ref, hbuf, sem, *, tm):
    del xs_in_ref
    hbuf[...] = _normmod(x_ref[...], g_ref[...], sh_ref[0], sc_ref[0])

    def start(r, _):
        for k in range(TOP_K):
            d = dest_ref[r * TOP_K + k]
            _row_copy(hbuf.at[pl.ds(r, 1)], xs_ref.at[pl.ds(d, 1)], sem).start(priority=k % 2)
        return 0

    lax.fori_loop(0, tm, start, 0)

    def wait(r, _):
        for k in range(TOP_K):
            _row_copy(hbuf.at[pl.ds(0, 1)], xs_ref.at[pl.ds(0, 1)], sem).wait()
        return 0

    lax.fori_loop(0, tm, wait, 0)


def _dispatch(dest, x, rb, g, sh, sc, n_rows):
    t, d = x.shape
    tm = _row_tile(rb, 256)
    zeros = jnp.zeros((n_rows, d), F32)
    return pl.pallas_call(
        functools.partial(_dispatch_kernel, tm=tm),
        grid=(t // tm,),
        in_specs=[pl.BlockSpec((tm * TOP_K,), lambda i: (i,), memory_space=pltpu.SMEM),
                  pl.BlockSpec((tm, d), lambda i: (i, 0)), pl.BlockSpec((1, d), lambda i: (0, 0)),
                  _mod_spec(sh, rb, tm), _mod_spec(sc, rb, tm),
                  pl.BlockSpec(memory_space=pl.ANY)],
        out_specs=pl.BlockSpec(memory_space=pl.ANY),
        out_shape=jax.ShapeDtypeStruct((n_rows, d), F32),
        scratch_shapes=[pltpu.VMEM((tm, d), F32), pltpu.SemaphoreType.DMA(())],
        input_output_aliases={5: 0},
        compiler_params=_cparams("arbitrary"),
        name="moe_dispatch",
    )(dest, x, g, sh, sc, zeros)


def _expert_kernel(be_ref, xs_ref, wgu_ref, bgu_ref, wd_ref, bd_ref, ys_ref, wgu_s, wd_s, *, de):
    i = pl.program_id(0)
    changed = jnp.logical_or(i == 0, be_ref[i] != be_ref[jnp.maximum(i - 1, 0)])

    @pl.when(changed)
    def _():
        wgu_s[...] = wgu_ref[0].astype(BF16)
        wd_s[...] = wd_ref[0].astype(BF16)

    gu = _dot(xs_ref[...].astype(BF16), wgu_s[...]) + bgu_ref[0]
    gate = jnp.minimum(gu[:, :de], SWIGLU_LIMIT)
    up = jnp.clip(gu[:, de:], -SWIGLU_LIMIT, SWIGLU_LIMIT)
    act = (up + 1.0) * gate * jax.nn.sigmoid(SWIGLU_ALPHA * gate)
    ys_ref[...] = _dot(act.astype(BF16), wd_s[...]) + bd_ref[0]


def _experts(blk_e, xs, layer, w_gu, b_gu, w_down, b_down):
    n_rows, d = xs.shape
    nl, ne, _, de2 = w_gu.shape
    de = de2 // 2
    n_blk = n_rows // MOE_ROWS
    grid_spec = pltpu.PrefetchScalarGridSpec(
        num_scalar_prefetch=1,
        grid=(n_blk,),
        in_specs=[pl.BlockSpec((MOE_ROWS, d), lambda i, be: (i, 0)),
                  pl.BlockSpec((None, 1, d, de2), lambda i, be: (layer, be[i], 0, 0)),
                  pl.BlockSpec((None, 1, 1, de2), lambda i, be: (layer, be[i], 0, 0)),
                  pl.BlockSpec((None, 1, de, d), lambda i, be: (layer, be[i], 0, 0)),
                  pl.BlockSpec((None, 1, 1, d), lambda i, be: (layer, be[i], 0, 0))],
        out_specs=pl.BlockSpec((MOE_ROWS, d), lambda i, be: (i, 0)),
        scratch_shapes=[pltpu.VMEM((d, de2), BF16), pltpu.VMEM((de, d), BF16)],
    )
    return pl.pallas_call(
        functools.partial(_expert_kernel, de=de),
        grid_spec=grid_spec,
        out_shape=jax.ShapeDtypeStruct((n_rows, d), F32),
        compiler_params=_cparams("arbitrary"),
        name="moe_experts",
    )(blk_e, xs, w_gu, b_gu.reshape(nl, ne, 1, de2), w_down, b_down.reshape(nl, ne, 1, d))


def _combine_kernel(dest_ref, tw_ref, x_ref, g2_ref, ys_ref, o_ref, gbuf, sem, *, tm):
    def start(r, _):
        for k in range(TOP_K):
            d = dest_ref[r * TOP_K + k]
            _row_copy(ys_ref.at[pl.ds(d, 1)], gbuf.at[k, pl.ds(r, 1)], sem).start(priority=k % 2)
        return 0

    lax.fori_loop(0, tm, start, 0)

    def wait(r, _):
        for k in range(TOP_K):
            _row_copy(ys_ref.at[pl.ds(0, 1)], gbuf.at[k, pl.ds(0, 1)], sem).wait()
        return 0

    lax.fori_loop(0, tm, wait, 0)
    tw = tw_ref[...]
    y = tw[:, 0:1] * gbuf[0]
    for k in range(1, TOP_K):
        y = y + tw[:, k:k + 1] * gbuf[k]
    o_ref[...] = x_ref[...] + g2_ref[0] * y


def _combine(dest, tw, x, rb, g2, ys):
    t, d = x.shape
    tm = _row_tile(rb, 256)
    return pl.pallas_call(
        functools.partial(_combine_kernel, tm=tm),
        grid=(t // tm,),
        in_specs=[pl.BlockSpec((tm * TOP_K,), lambda i: (i,), memory_space=pltpu.SMEM),
                  pl.BlockSpec((tm, LANES), lambda i: (i, 0)), pl.BlockSpec((tm, d), lambda i: (i, 0)),
                  _mod_spec(g2, rb, tm), pl.BlockSpec(memory_space=pl.ANY)],
        out_specs=pl.BlockSpec((tm, d), lambda i: (i, 0)),
        out_shape=jax.ShapeDtypeStruct((t, d), F32),
        scratch_shapes=[pltpu.VMEM((TOP_K, tm, d), F32), pltpu.SemaphoreType.DMA(())],
        compiler_params=_cparams("arbitrary"),
        name="moe_combine",
    )(dest, tw, x, g2, ys)


def _moe(x, rb, g, sh, sc, g2, layer, w_router, b_router, w_gu, b_gu, w_down, b_down):
    t, d = x.shape
    ne = w_router.shape[1]
    w_pad = jnp.pad(w_router, ((0, 0), (0, LANES - ne)))
    b_pad = jnp.pad(b_router, (0, LANES - ne)).reshape(1, LANES)
    ti, tw, rk, cnt = _router(x, rb, g, sh, sc, w_pad, b_pad)
    counts = cnt[0, :ne].astype(I32)
    padded = (counts + MOE_ROWS - 1) // MOE_ROWS * MOE_ROWS
    pend = jnp.cumsum(padded)
    pstart = pend - padded
    dest = (pstart[ti[:, :TOP_K]] + rk[:, :TOP_K]).reshape(-1).astype(I32)
    n_blk = -(-(t * TOP_K) // MOE_ROWS) + ne
    first_row = jnp.arange(n_blk, dtype=I32)[:, None] * MOE_ROWS
    blk_e = jnp.minimum(jnp.sum((pend[None, :] <= first_row).astype(I32), axis=1), ne - 1)
    xs = _dispatch(dest, x, rb, g, sh, sc, n_blk * MOE_ROWS)
    ys = _experts(blk_e, xs, layer, w_gu, b_gu, w_down, b_down)
    return _combine(dest, tw, x, rb, g2, ys)


def _nm_matmul_kernel(x_ref, g_ref, sh_ref, sc_ref, w_ref, o_ref, *, sig_from):
    h = _normmod(x_ref[...], g_ref[...], sh_ref[0], sc_ref[0])
    o = _dot(h.astype(BF16), w_ref[...])
    if sig_from is not None:
        col = lax.broadcasted_iota(I32, o.shape, 1)
        o = jnp.where(col >= sig_from, jax.nn.sigmoid(o), o)
    o_ref[...] = o


def _normmod_matmul(x, rb, g, sh, sc, w_bf16, sig_from=None):
    t, d = x.shape
    n = w_bf16.shape[1]
    tm = _row_tile(rb, 512)
    return pl.pallas_call(
        functools.partial(_nm_matmul_kernel, sig_from=sig_from),
        grid=(t // tm,),
        in_specs=[pl.BlockSpec((tm, d), lambda i: (i, 0)), pl.BlockSpec((1, d), lambda i: (0, 0)),
                  _mod_spec(sh, rb, tm), _mod_spec(sc, rb, tm), pl.BlockSpec((d, n), lambda i: (0, 0))],
        out_specs=pl.BlockSpec((tm, n), lambda i: (i, 0)),
        out_shape=jax.ShapeDtypeStruct((t, n), F32),
        compiler_params=_cparams("parallel"),
        name="normmod_matmul",
    )(x, g, sh, sc, w_bf16)


def _mm_residual_kernel(a_ref, x_ref, g1_ref, w_ref, o_ref):
    o_ref[...] = x_ref[...] + g1_ref[0] * _dot(a_ref[...].astype(BF16), w_ref[...])


def _matmul_residual(a, x, g1, w_bf16, rb):
    t, d = x.shape
    k = a.shape[1]
    tm = _row_tile(rb, 512)
    return pl.pallas_call(
        _mm_residual_kernel,
        grid=(t // tm,),
        in_specs=[pl.BlockSpec((tm, k), lambda i: (i, 0)), pl.BlockSpec((tm, d), lambda i: (i, 0)),
                  _mod_spec(g1, rb, tm), pl.BlockSpec((k, d), lambda i: (0, 0))],
        out_specs=pl.BlockSpec((tm, d), lambda i: (i, 0)),
        out_shape=jax.ShapeDtypeStruct((t, d), F32),
        compiler_params=_cparams("parallel"),
        name="matmul_residual",
    )(a, x, g1, w_bf16)


def _final_norm_kernel(x_ref, g_ref, o_ref):
    x = x_ref[...]
    o_ref[...] = x * lax.rsqrt(jnp.mean(x * x, axis=-1, keepdims=True) + RMS_EPS) * g_ref[...]


def _final_norm(x, g, rb):
    t, d = x.shape
    tm = _row_tile(rb, 512)
    return pl.pallas_call(
        _final_norm_kernel,
        grid=(t // tm,),
        in_specs=[pl.BlockSpec((tm, d), lambda i: (i, 0)), pl.BlockSpec((1, d), lambda i: (0, 0))],
        out_specs=pl.BlockSpec((tm, d), lambda i: (i, 0)),
        out_shape=jax.ShapeDtypeStruct((t, d), F32),
        compiler_params=_cparams("parallel"),
        name="final_norm",
    )(x, g.reshape(1, d))


def _rel_bucket_table():
    n = np.arange(REL_MAX_DIST + 1)
    max_exact = REL_BUCKETS // 2
    nf = np.maximum(n, 1).astype(np.float32)
    large = max_exact + (np.log(nf / max_exact) / math.log(REL_MAX_DIST / max_exact)
                         * (REL_BUCKETS - max_exact)).astype(np.int32)
    return np.where(n < max_exact, n, np.minimum(large, REL_BUCKETS - 1)).astype(np.int32)


def _bias_by_dist(rel_bias):
    return rel_bias.astype(F32)[_rel_bucket_table()].T


def _stack_heads(tab):
    h, r, c = tab.shape
    return tab.reshape(N_KV_HEADS, GQ * r, c)


def _dist_tile(rows, cols, offset, col_step=1):
    d = offset + np.arange(rows)[:, None] - col_step * np.arange(cols)[None, :]
    return np.clip(d, 0, REL_MAX_DIST)


CMP_NEAR = 32
CMP_PAD = 16


def _overlap_matrix(n_cmp, n_sel_pad):
    ci = np.arange(n_cmp)[:, None] * CMP_STRIDE
    sj = np.arange(n_sel_pad)[None, :] * SEL_BLOCK
    ov = ((ci < sj + SEL_BLOCK) & (ci + CMP_BLOCK > sj)).astype(np.float32)
    return np.pad(ov, ((CMP_PAD, CMP_PAD), (0, 0)))


def _compress_kernel(x_ref, pos_ref, w1_ref, b1_ref, w2_ref, o_ref, *, hid):
    w1 = w1_ref[0]
    parts = _dot(x_ref[0, 0].astype(BF16), w1)
    pt = _dot(pos_ref[0].astype(BF16), w1)
    p0 = parts[:, :hid] + pt[0:1, :hid]
    p1 = parts[:, hid:] + pt[1:2, hid:]
    n = p1.shape[0]
    pre = p0 + pltpu.roll(p1, n - 1, axis=0) + b1_ref[0]
    o_ref[0, 0] = _dot(jax.nn.gelu(pre).astype(BF16), w2_ref[0])


def _compress(xc, cmp_w1, cmp_b1, cmp_w2, cmp_pos):
    _, ng, nch, kdim = xc.shape
    hid = cmp_w1.shape[-1]
    hd = cmp_w2.shape[-1]
    w1 = cmp_w1.reshape(2, CMP_RATIO, kdim, hid)
    w1cat = jnp.concatenate([w1[:, r] for r in range(CMP_RATIO)], axis=-1).astype(BF16)
    pos = jnp.pad(cmp_pos.reshape(2, CMP_RATIO, kdim), ((0, 0), (0, SUBLANES - CMP_RATIO), (0, 0)))
    return pl.pallas_call(
        functools.partial(_compress_kernel, hid=hid),
        grid=(2, ng),
        in_specs=[pl.BlockSpec((1, 1, nch, kdim), lambda c, n: (c, n, 0, 0)),
                  pl.BlockSpec((1, SUBLANES, kdim), lambda c, n: (c, 0, 0)),
                  pl.BlockSpec((1, kdim, CMP_RATIO * hid), lambda c, n: (c, 0, 0)),
                  pl.BlockSpec((1, 1, hid), lambda c, n: (c, 0, 0)),
                  pl.BlockSpec((1, hid, hd), lambda c, n: (c, 0, 0))],
        out_specs=pl.BlockSpec((1, 1, nch, hd), lambda c, n: (c, n, 0, 0)),
        out_shape=jax.ShapeDtypeStruct((2, ng, nch, hd), F32),
        compiler_params=_cparams("parallel", "parallel"),
        name="nsa_compress",
    )(xc, pos, w1cat, cmp_b1.reshape(2, 1, hid), cmp_w2.astype(BF16))


def _dot3_nt(a, b):
    a_hi, a_lo = _split(a)
    b_hi, b_lo = _split(b)
    return _dot_nt(a_hi, b_hi) + (_dot_nt(a_lo, b_hi) + _dot_nt(a_hi, b_lo))


def _eye(n):
    r = lax.broadcasted_iota(I32, (n, n), 0)
    c = lax.broadcasted_iota(I32, (n, n), 1)
    return (r == c).astype(BF16)


def _transpose_bf16(x, eye):
    return _dot_nt(eye, x).astype(BF16)


def _top_blocks_cols(score, ntop):
    row_f = lax.broadcasted_iota(I32, score.shape, 0).astype(F32)
    sel = jnp.zeros(score.shape, F32)
    for _ in range(ntop):
        m = jnp.max(score, axis=0, keepdims=True)
        idx = jnp.min(jnp.where(score == m, row_f, float(score.shape[0])), axis=0, keepdims=True)
        hit = row_f == idx
        sel = jnp.where(jnp.logical_and(hit, m > -jnp.inf), 1.0, sel)
        score = jnp.where(hit, -jnp.inf, score)
    return sel


def _cmp_attn_kernel(q_ref, kc_ref, vc_ref, vct_ref, ov_ref, ovt_ref, bnear_ref, bfar_ref, oct_ref, selt_ref,
                     *, nch, ntop, qblk):
    qb = pl.program_id(2)
    cols = GQ * qblk
    scale = HEAD_DIM ** -0.5
    c = pl.multiple_of(qb * (qblk // CMP_STRIDE), SUBLANES)
    eye = _eye(HEAD_DIM)
    q_hi, q_lo = _split(q_ref[0].reshape(cols, HEAD_DIM))
    qt_hi = _transpose_bf16(q_hi, eye)
    qt_lo = _transpose_bf16(q_lo, eye)

    def logits(k):
        k_hi, k_lo = _split(k)
        return _dot(k_hi, qt_hi) + (_dot(k_lo, qt_hi) + _dot(k_hi, qt_lo))

    s_far = logits(kc_ref[0, 0, CMP_PAD:CMP_PAD + nch, :]) * scale + bfar_ref[0]
    s_near = logits(kc_ref[0, 0, pl.ds(c, CMP_NEAR), :]) * scale + bnear_ref[0]
    mask_far = lax.broadcasted_iota(I32, (nch, cols), 0) < c - CMP_PAD
    rel = lax.broadcasted_iota(I32, (CMP_NEAR, cols), 0)
    i_io = lax.broadcasted_iota(I32, (CMP_NEAR, cols), 1) & (qblk - 1)
    mask_near = jnp.logical_and(CMP_STRIDE * (rel - CMP_PAD) + CMP_BLOCK - 1 <= i_io, rel + c - CMP_PAD >= 0)
    sf = jnp.where(mask_far, s_far, NEG)
    sn = jnp.where(mask_near, s_near, NEG)
    m = jnp.maximum(jnp.max(sf, axis=0, keepdims=True), jnp.max(sn, axis=0, keepdims=True))
    pf = jnp.where(mask_far, jnp.exp(sf - m), 0.0)
    pn = jnp.where(mask_near, jnp.exp(sn - m), 0.0)
    l = jnp.sum(pf, axis=0, keepdims=True) + jnp.sum(pn, axis=0, keepdims=True)
    linv = 1.0 / jnp.where(l > 0, l, 1.0)
    vnt = _transpose_bf16(vc_ref[0, 0, pl.ds(c, CMP_NEAR), :].astype(BF16), eye)
    o = _dot(vct_ref[0, 0].astype(BF16), pf.astype(BF16)) + _dot(vnt, pn.astype(BF16))
    oct_ref[0, 0, 0] = o * linv
    pfn = pf * linv
    pnn = pn * linv
    pgf = pfn[:, 0:qblk]
    pgn = pnn[:, 0:qblk]
    for g in range(1, GQ):
        pgf = pgf + pfn[:, g * qblk:(g + 1) * qblk]
        pgn = pgn + pnn[:, g * qblk:(g + 1) * qblk]
    f_hi, f_lo = _split(pgf)
    n_hi, n_lo = _split(pgn)
    nsp = ovt_ref.shape[0]
    ovt = ovt_ref[...]
    ovnt = _transpose_bf16(ov_ref[pl.ds(c, CMP_NEAR), :], _eye(nsp))
    imp = (_dot(ovt, f_hi) + _dot(ovnt, n_hi)) + (_dot(ovt, f_lo) + _dot(ovnt, n_lo))
    s_io = lax.broadcasted_iota(I32, (nsp, qblk), 0)
    t = qb * qblk + lax.broadcasted_iota(I32, (nsp, qblk), 1)
    cur = lax.shift_right_logical(t, int(math.log2(SEL_BLOCK)))
    forced = jnp.logical_or(s_io == 0, jnp.logical_or(s_io == cur, s_io == cur - 1))
    score = jnp.where(s_io * SEL_BLOCK <= t, imp + jnp.where(forced, FORCE_SCORE, 0.0), -jnp.inf)
    selt_ref[0, 0, 0] = _top_blocks_cols(score, ntop)


def _cmp_attention(q_hm, kc_pad, vc_pad, vct, ov, ovt, bnear, bfar, nch, ntop):
    b, _, t, hd = q_hm.shape
    qblk = math.gcd(t, Q_BLOCK)
    nqb = t // qblk
    nsp = ov.shape[1]
    cols = GQ * qblk
    kern = functools.partial(_cmp_attn_kernel, nch=nch, ntop=ntop, qblk=qblk)
    kvspec = pl.BlockSpec((1, 1, kc_pad.shape[2], hd), lambda bb, k, i: (bb, k, 0, 0))
    return pl.pallas_call(
        kern,
        grid=(b, N_KV_HEADS, nqb),
        in_specs=[pl.BlockSpec((1, GQ, qblk, hd), lambda bb, k, i: (bb, k, i, 0)), kvspec, kvspec,
                  pl.BlockSpec((1, 1, hd, nch), lambda bb, k, i: (bb, k, 0, 0)),
                  pl.BlockSpec(ov.shape, lambda bb, k, i: (0, 0)), pl.BlockSpec(ovt.shape, lambda bb, k, i: (0, 0)),
                  pl.BlockSpec((1,) + bnear.shape[1:], lambda bb, k, i: (k, 0, 0)),
                  pl.BlockSpec((1,) + bfar.shape[1:], lambda bb, k, i: (k, 0, 0))],
        out_specs=[pl.BlockSpec((1, 1, 1, hd, cols), lambda bb, k, i: (bb, k, i, 0, 0)),
                   pl.BlockSpec((1, 1, 1, nsp, qblk), lambda bb, k, i: (bb, k, i, 0, 0))],
        out_shape=[jax.ShapeDtypeStruct((b, N_KV_HEADS, nqb, hd, cols), F32),
                   jax.ShapeDtypeStruct((b, N_KV_HEADS, nqb, nsp, qblk), F32)],
        compiler_params=_cparams("parallel", "parallel", "parallel"),
        name="nsa_cmp_attention",
    )(q_hm, kc_pad, vc_pad, vct, ov, ovt, bnear, bfar)


def _slc_win_kernel(q_ref, selt_ref, oct_ref, gate_ref, sk_ref, svt_ref, wk_ref, wvt_ref, bt_ref, bfar_ref, o_ref,
                    m_s, l_s, acc_s, *, qblk):
    qb = pl.program_id(2)
    cols = GQ * qblk
    scale = HEAD_DIM ** -0.5
    qt = _transpose_bf16(q_ref[0].reshape(cols, HEAD_DIM).astype(BF16), _eye(HEAD_DIM))
    bfar = bfar_ref[0]
    per = qblk // SEL_BLOCK
    key_j = lax.broadcasted_iota(I32, (qblk, cols), 0)
    q_i = lax.broadcasted_iota(I32, (qblk, cols), 1) & (qblk - 1)

    def sel_mask(kt):
        pieces = [jnp.broadcast_to(selt_ref[0, 0, 0, pl.ds(per * kt + e, 1), :], (SEL_BLOCK, qblk))
                  for e in range(per)]
        mk = jnp.concatenate(pieces, axis=0)
        return jnp.concatenate([mk] * GQ, axis=1) > 0.5

    def reset():
        m_s[...] = jnp.full(m_s.shape, NEG, F32)
        l_s[...] = jnp.zeros(l_s.shape, F32)
        acc_s[...] = jnp.zeros(acc_s.shape, F32)

    def update(pieces, vts):
        s = jnp.concatenate(pieces, axis=0)
        vt = jnp.concatenate(vts, axis=1)
        m_prev = m_s[...]
        m_new = jnp.maximum(m_prev, jnp.max(s, axis=0, keepdims=True))
        alpha = jnp.exp(m_prev - m_new)
        p = jnp.where(s > 0.5 * NEG, jnp.exp(s - m_new), 0.0)
        l_s[...] = alpha * l_s[...] + jnp.sum(p, axis=0, keepdims=True)
        acc_s[...] = alpha * acc_s[...] + _dot(vt, p.astype(BF16))
        m_s[...] = m_new

    def result():
        l = l_s[...]
        return acc_s[...] / jnp.where(l > 0, l, 1.0)

    def masked_logits(k_ref, kt, bias, mask):
        return jnp.where(mask, _dot(k_ref[0, 0, kt], qt) * scale + bias, NEG)

    reset()
    n_far = jnp.maximum(qb - 1, 0)
    last = sk_ref.shape[2] - 1

    def far(i, carry):
        pieces, vts = [], []
        for e in range(FAR_GROUP):
            kt = i * FAR_GROUP + e
            ktc = jnp.minimum(kt, last)
            pieces.append(masked_logits(sk_ref, ktc, bfar, jnp.logical_and(sel_mask(ktc), kt < n_far)))
            vts.append(svt_ref[0, 0, ktc])
        update(pieces, vts)
        return carry

    lax.fori_loop(0, (n_far + FAR_GROUP - 1) // FAR_GROUP, far, 0)
    kt1 = jnp.maximum(qb - 1, 0)
    update([masked_logits(sk_ref, kt1, bt_ref[0, 1], jnp.logical_and(sel_mask(kt1), qb >= 1)),
            masked_logits(sk_ref, qb, bt_ref[0, 0], jnp.logical_and(sel_mask(qb), key_j <= q_i))],
           [svt_ref[0, 0, kt1], svt_ref[0, 0, qb]])
    o_s = result()

    reset()
    nwin = WINDOW // qblk
    pieces, vts = [], []
    for u in range(nwin + 1):
        kt = qb - nwin + u
        ktc = jnp.maximum(kt, 0)
        delta = (nwin - u) * qblk
        bias = bt_ref[0, 0] if delta == 0 else (bt_ref[0, 1] if delta == qblk else bfar)
        if u == 0:
            shape_mask = key_j >= q_i
        elif u == nwin:
            shape_mask = key_j <= q_i
        else:
            shape_mask = key_j >= 0
        pieces.append(masked_logits(wk_ref, ktc, bias, jnp.logical_and(shape_mask, kt >= 0)))
        vts.append(wvt_ref[0, 0, ktc])
    update(pieces, vts)
    o_w = result()

    gt = gate_ref[0, 0, 0]
    ot = gt[0:1] * oct_ref[0, 0, 0] + gt[1:2] * o_s + gt[2:3] * o_w
    o = _transpose_bf16(ot.astype(BF16), _eye(cols))
    o_ref[0] = o.reshape(GQ, qblk, HEAD_DIM)


def _slc_win_attention(q_hm, selt, oct, gates_blk, sk, svt, wk, wvt, bt, bfar):
    b, _, t, hd = q_hm.shape
    qblk = math.gcd(t, Q_BLOCK)
    nqb = t // qblk
    nsp = selt.shape[-2]
    cols = GQ * qblk
    kspec = pl.BlockSpec((1, 1, nqb, qblk, hd), lambda bb, k, i: (bb, k, 0, 0, 0))
    vspec = pl.BlockSpec((1, 1, nqb, hd, qblk), lambda bb, k, i: (bb, k, 0, 0, 0))
    qspec = pl.BlockSpec((1, GQ, qblk, hd), lambda bb, k, i: (bb, k, i, 0))
    return pl.pallas_call(
        functools.partial(_slc_win_kernel, qblk=qblk),
        grid=(b, N_KV_HEADS, nqb),
        in_specs=[qspec, pl.BlockSpec((1, 1, 1, nsp, qblk), lambda bb, k, i: (bb, k, i, 0, 0)),
                  pl.BlockSpec((1, 1, 1, hd, cols), lambda bb, k, i: (bb, k, i, 0, 0)),
                  pl.BlockSpec((1, 1, 1, 3, cols), lambda bb, k, i: (bb, k, i, 0, 0)),
                  kspec, vspec, kspec, vspec,
                  pl.BlockSpec((1,) + bt.shape[1:], lambda bb, k, i: (k, 0, 0, 0)),
                  pl.BlockSpec((1,) + bfar.shape[1:], lambda bb, k, i: (k, 0, 0))],
        out_specs=qspec,
        out_shape=jax.ShapeDtypeStruct(q_hm.shape, BF16),
        scratch_shapes=[pltpu.VMEM((1, cols), F32), pltpu.VMEM((1, cols), F32), pltpu.VMEM((hd, cols), F32)],
        compiler_params=_cparams("parallel", "parallel", "parallel"),
        name="nsa_slc_win_attention",
    )(q_hm, selt, oct, gates_blk, sk, svt, wk, wvt, bt, bfar)


def _cols_table(tab):
    h, nq, nk = tab.shape
    return jnp.transpose(tab.reshape(N_KV_HEADS, GQ, nq, nk), (0, 3, 1, 2)).reshape(N_KV_HEADS, nk, GQ * nq)


def _nsa_prompt(q, gates, kv, rel_bias, cmp_w1, cmp_b1, cmp_w2, cmp_pos):
    b, t, _ = q.shape
    qblk = math.gcd(t, Q_BLOCK)
    assert qblk == Q_BLOCK and t % Q_BLOCK == 0 and WINDOW % Q_BLOCK == 0
    nqb = t // qblk
    nch = t // CMP_STRIDE
    n_sel = t // SEL_BLOCK
    nsp = -(-n_sel // LANES) * LANES
    kv_hm = jnp.transpose(kv, (2, 0, 3, 1, 4))
    xc = kv_hm[:2].reshape(2, b * N_KV_HEADS, nch, CMP_STRIDE * HEAD_DIM)
    kvc = _compress(xc, cmp_w1, cmp_b1, cmp_w2, cmp_pos).reshape(2, b, N_KV_HEADS, nch, HEAD_DIM)
    vct = jnp.swapaxes(kvc[1], -1, -2)
    kvc = jnp.pad(kvc, ((0, 0), (0, 0), (0, 0), (CMP_PAD, CMP_PAD), (0, 0)))
    tbl = _bias_by_dist(rel_bias)
    bfar = _cols_table(jnp.broadcast_to(tbl[:, REL_MAX_DIST][:, None, None], (N_HEADS, qblk, 1)))
    bnear = _cols_table(tbl[:, _dist_tile(qblk, CMP_NEAR, CMP_STRIDE * CMP_PAD - CMP_BLOCK + 1, CMP_STRIDE)])
    bt = jnp.stack([_cols_table(tbl[:, _dist_tile(qblk, qblk, 0)]),
                    _cols_table(tbl[:, _dist_tile(qblk, qblk, qblk)])], axis=1)
    ov_np = _overlap_matrix(nch, nsp)
    ov = jnp.asarray(ov_np, BF16)
    ovt = jnp.asarray(ov_np[CMP_PAD:CMP_PAD + nch].T, BF16)
    q_hm = jnp.transpose(q.reshape(b, t, N_HEADS, HEAD_DIM), (0, 2, 1, 3))
    gates_blk = jnp.transpose(gates.reshape(b, nqb, qblk, N_KV_HEADS, GQ, 3), (0, 3, 1, 5, 4, 2))
    gates_blk = gates_blk.reshape(b, N_KV_HEADS, nqb, 3, GQ * qblk)
    oct, selt = _cmp_attention(q_hm, kvc[0], kvc[1], vct, ov, ovt, bnear, bfar, nch, min(SEL_TOP, n_sel))
    kvb = kv_hm[2:].astype(BF16).reshape(4, b, N_KV_HEADS, nqb, qblk, HEAD_DIM)
    sk, wk = kvb[0], kvb[2]
    svt, wvt = jnp.swapaxes(kvb[1], -1, -2), jnp.swapaxes(kvb[3], -1, -2)
    o_hm = _slc_win_attention(q_hm, selt, oct, gates_blk, sk, svt, wk, wvt, bt, bfar)
    return jnp.transpose(o_hm, (0, 2, 1, 3)).reshape(b, t, N_HEADS * HEAD_DIM)


CMP_TAP_PAIRS = CMP_STRIDE // 2
PAGE_CHUNKS = PAGE_SIZE // CMP_STRIDE


def _paged_compress_kernel(pt_ref, cache_ref, wp_ref, w1_ref, pos_ref, b1_ref, w2_ref, o_ref, buft, buf, carry, sem,
                           *, pg, npages, hid):
    b, g = pl.program_id(0), pl.program_id(1)
    m = pg * PAGE_CHUNKS
    hk = N_KV_HEADS * HEAD_DIM
    pairs = hk // LANES

    def page_copy(p):
        page = pt_ref[b * npages + g * pg + p]
        return pltpu.make_async_copy(cache_ref.at[page, pl.ds(0, 2)], buft.at[p], sem)

    for p in range(pg):
        page_copy(p).start()

    @pl.when(g == 0)
    def _():
        carry[...] = jnp.zeros_like(carry)

    for p in range(pg):
        page_copy(p).wait()

    def to_rows(p, carry_):
        r0 = pl.multiple_of(p * PAGE_SIZE, PAGE_SIZE)
        for comp in range(2):
            for hp in range(pairs):
                plane = buft[p, comp, hp * LANES:(hp + 1) * LANES, :]
                buf[comp * pairs + hp, pl.ds(r0, PAGE_SIZE), :] = plane.T
        return carry_

    lax.fori_loop(0, pg, to_rows, 0)

    row0 = lax.broadcasted_iota(I32, (m, hid), 0) == 0
    for comp in range(2):
        pt = _dot(pos_ref[comp].astype(BF16), w1_ref[comp])
        for hp in range(N_KV_HEADS // 2):
            lg = (comp * hk + hp * 2 * HEAD_DIM) // LANES
            acc = jnp.zeros((m, 2 * CMP_RATIO * hid), F32)
            for u in range(CMP_TAP_PAIRS):
                taps = [buf[lg, pl.ds(2 * u + e, m, stride=CMP_STRIDE), :] for e in range(2)]
                acc = acc + _dot(jnp.concatenate(taps, axis=1).astype(BF16), wp_ref[comp, u])
            for e in range(2):
                kvh = 2 * hp + e
                base = e * CMP_RATIO * hid
                p0 = acc[:, base:base + hid] + pt[0:1, :hid]
                p1 = acc[:, base + hid:base + 2 * hid] + pt[1:2, hid:]
                prev0 = jnp.where(row0, carry[comp, kvh], pltpu.roll(p0, 1, axis=0))
                carry[comp, kvh] = p0[m - 1:m, :]
                pre = prev0 + p1 + b1_ref[comp]
                o_ref[comp, 0, kvh] = _dot(jax.nn.gelu(pre).astype(BF16), w2_ref[comp])


def _paged_compress(page_table, cache_t, cmp_w1, cmp_b1, cmp_w2, cmp_pos):
    bs, npages = page_table.shape
    hid = cmp_w1.shape[-1]
    kdim = CMP_STRIDE * HEAD_DIM
    pg = math.gcd(npages, 32)
    m = pg * PAGE_CHUNKS
    w1 = cmp_w1.reshape(2, CMP_RATIO, kdim, hid)
    w1cat = jnp.concatenate([w1[:, r] for r in range(CMP_RATIO)], axis=-1)
    n1 = CMP_RATIO * hid
    wt = w1cat.reshape(2, CMP_TAP_PAIRS, 2, HEAD_DIM, n1)
    zero = jnp.zeros_like(wt)
    wp = jnp.concatenate([jnp.concatenate([wt, zero], axis=-1), jnp.concatenate([zero, wt], axis=-1)], axis=3)
    wp = wp.reshape(2, CMP_TAP_PAIRS, 4 * HEAD_DIM, 2 * n1).astype(BF16)
    pos = jnp.pad(cmp_pos.reshape(2, CMP_RATIO, kdim), ((0, 0), (0, SUBLANES - CMP_RATIO), (0, 0)))
    nch = npages * PAGE_CHUNKS
    grid_spec = pltpu.PrefetchScalarGridSpec(
        num_scalar_prefetch=1,
        grid=(bs, npages // pg),
        in_specs=[pl.BlockSpec(memory_space=pl.ANY),
                  pl.BlockSpec(wp.shape, lambda b, g, pt: (0, 0, 0, 0)),
                  pl.BlockSpec((2, kdim, n1), lambda b, g, pt: (0, 0, 0)),
                  pl.BlockSpec((2, SUBLANES, kdim), lambda b, g, pt: (0, 0, 0)),
                  pl.BlockSpec((2, 1, hid), lambda b, g, pt: (0, 0, 0)),
                  pl.BlockSpec((2, hid, HEAD_DIM), lambda b, g, pt: (0, 0, 0))],
        out_specs=pl.BlockSpec((2, 1, N_KV_HEADS, m, HEAD_DIM), lambda b, g, pt: (0, b, 0, g, 0)),
        scratch_shapes=[pltpu.VMEM((pg, 2, N_KV_HEADS * HEAD_DIM, PAGE_SIZE), F32),
                        pltpu.VMEM((2 * N_KV_HEADS * HEAD_DIM // LANES, pg * PAGE_SIZE, LANES), F32),
                        pltpu.VMEM((2, N_KV_HEADS, 1, hid), F32), pltpu.SemaphoreType.DMA(())],
    )
    return pl.pallas_call(
        functools.partial(_paged_compress_kernel, pg=pg, npages=npages, hid=hid),
        grid_spec=grid_spec,
        out_shape=jax.ShapeDtypeStruct((2, bs, N_KV_HEADS, nch, HEAD_DIM), F32),
        compiler_params=_cparams("parallel", "arbitrary"),
        name="nsa_paged_compress",
    )(page_table.reshape(-1), cache_t, wp, w1cat.astype(BF16), pos, cmp_b1.reshape(2, 1, hid), cmp_w2.astype(BF16))


def _rows8(x):
    return jnp.concatenate([x, jnp.zeros((SUBLANES - x.shape[0], x.shape[1]), x.dtype)], axis=0)


def _sample_cmp_kernel(q_ref, kc_ref, vc_ref, ov_ref, bias_ref, oc_ref, idx_ref, *, nch, n_sel, n_past_blk, ntop):
    scale = HEAD_DIM ** -0.5
    nsp = ov_ref.shape[1]
    row = lax.broadcasted_iota(I32, (SUBLANES, nch), 0)
    lane = lax.broadcasted_iota(I32, (SUBLANES, nch), 1)
    mask = jnp.logical_and(lane >= 1, row < GQ)
    imp = jnp.zeros((SUBLANES, nsp), F32)
    row_s = lax.broadcasted_iota(I32, (SUBLANES, nsp), 0)
    for kvh in range(N_KV_HEADS):
        q8 = _rows8(q_ref[0, kvh * GQ:(kvh + 1) * GQ, :])
        s = jnp.where(mask, _dot3_nt(q8, kc_ref[0, kvh]) * scale + bias_ref[kvh], NEG)
        mx = jnp.max(s, axis=-1, keepdims=True)
        p = jnp.where(mask, jnp.exp(s - mx), 0.0)
        l = jnp.sum(p, axis=-1, keepdims=True)
        linv = 1.0 / jnp.where(l > 0, l, 1.0)
        o = _dot(p.astype(BF16), vc_ref[0, kvh].astype(BF16)) * linv
        oc_ref[0, kvh * GQ:(kvh + 1) * GQ, :] = o[0:GQ]
        p_hi, p_lo = _split(p * linv)
        contrib = _dot(p_hi, ov_ref[...]) + _dot(p_lo, ov_ref[...])
        imp = jnp.where(row_s == kvh, jnp.sum(contrib, axis=0, keepdims=True), imp)
    s_io = lax.broadcasted_iota(I32, (SUBLANES, nsp), 1)
    t = n_past_blk * SEL_BLOCK
    cur = t // SEL_BLOCK
    forced = jnp.logical_or(s_io == 0, jnp.logical_or(s_io == cur, s_io == cur - 1))
    valid = jnp.logical_and(s_io * SEL_BLOCK <= t, s_io < n_sel)
    score = jnp.where(valid, imp + jnp.where(forced, FORCE_SCORE, 0.0), -jnp.inf)
    lane_f = s_io.astype(F32)
    out_lane = lax.broadcasted_iota(I32, (SUBLANES, LANES), 1)
    picks = jnp.zeros((SUBLANES, LANES), F32)
    for it in range(ntop):
        mx = jnp.max(score, axis=-1, keepdims=True)
        idx = jnp.min(jnp.where(score == mx, lane_f, float(nsp)), axis=-1, keepdims=True)
        picks = jnp.where(out_lane == it, idx, picks)
        score = jnp.where(lane_f == idx, -jnp.inf, score)
    idx_ref[0] = picks.astype(I32)


def _sample_cmp_attention(q_hm, kc, vc, ov, bias, n_sel, n_past_blk, ntop):
    bs = q_hm.shape[0]
    nch = kc.shape[2]
    kvspec = pl.BlockSpec((1, N_KV_HEADS, nch, HEAD_DIM), lambda b: (b, 0, 0, 0))
    return pl.pallas_call(
        functools.partial(_sample_cmp_kernel, nch=nch, n_sel=n_sel, n_past_blk=n_past_blk, ntop=ntop),
        grid=(bs,),
        in_specs=[pl.BlockSpec((1, N_HEADS, HEAD_DIM), lambda b: (b, 0, 0)), kvspec, kvspec,
                  pl.BlockSpec(ov.shape, lambda b: (0, 0)), pl.BlockSpec(bias.shape, lambda b: (0, 0, 0))],
        out_specs=[pl.BlockSpec((1, N_HEADS, HEAD_DIM), lambda b: (b, 0, 0)),
                   pl.BlockSpec((1, SUBLANES, LANES), lambda b: (b, 0, 0))],
        out_shape=[jax.ShapeDtypeStruct(q_hm.shape, F32), jax.ShapeDtypeStruct((bs, SUBLANES, LANES), I32)],
        compiler_params=_cparams("parallel"),
        name="nsa_sample_cmp_attention",
    )(q_hm, kc, vc, ov, bias)


def _sample_slc_win_kernel(page_ref, half_ref, new_ref, d0_ref, var_ref,
                           q_ref, oc_ref, gate_ref, cache_ref, newt_ref, wint_ref, winnew_ref,
                           bslc_ref, bwin_ref, bwin_new_ref, o_ref, buf, sem, *, ntop):
    b = pl.program_id(0)
    scale = HEAD_DIM ** -0.5
    hk = N_KV_HEADS * HEAD_DIM
    nslots = N_KV_HEADS * ntop

    def plane_copies(slot):
        i = b * nslots + slot
        kvh = slot // ntop
        return [pltpu.make_async_copy(cache_ref.at[page_ref[i], 2 + e, kvh], buf.at[slot, e], sem) for e in range(2)]

    for slot in range(nslots):
        is_new = new_ref[b * nslots + slot] == 1

        @pl.when(jnp.logical_not(is_new))
        def _():
            for cp in plane_copies(slot):
                cp.start()

        @pl.when(is_new)
        def _():
            buf[slot] = newt_ref[0, :, slot // ntop]

    for slot in range(nslots):
        @pl.when(new_ref[b * nslots + slot] == 0)
        def _():
            for cp in plane_copies(slot):
                cp.wait()

    lane = lax.broadcasted_iota(I32, (SUBLANES, PAGE_SIZE), 1)
    gt = gate_ref[0]
    for kvh in range(N_KV_HEADS):
        hs = slice(kvh * GQ, (kvh + 1) * GQ)
        ks = slice(kvh * HEAD_DIM, (kvh + 1) * HEAD_DIM)
        vsl = slice(hk + kvh * HEAD_DIM, hk + (kvh + 1) * HEAD_DIM)
        q8f = _rows8(q_ref[0, hs, :])
        q8 = q8f.astype(BF16)
        ss, masks = [], []
        for j in range(ntop):
            slot = kvh * ntop + j
            i = b * nslots + slot
            s = _dot(q8, buf[slot, 0].astype(BF16)) * scale + bslc_ref[kvh, var_ref[i]]
            off = lane - half_ref[i] * SEL_BLOCK
            mask = jnp.logical_and(jnp.logical_and(off >= 0, off < SEL_BLOCK), off <= d0_ref[i])
            ss.append(jnp.where(mask, s, NEG))
            masks.append(mask)
        mx = ss[0].max(axis=-1, keepdims=True)
        for s in ss[1:]:
            mx = jnp.maximum(mx, s.max(axis=-1, keepdims=True))
        l = jnp.zeros((SUBLANES, 1), F32)
        acc = jnp.zeros((SUBLANES, HEAD_DIM), F32)
        for j in range(ntop):
            p = jnp.where(masks[j], jnp.exp(ss[j] - mx), 0.0)
            l = l + jnp.sum(p, axis=-1, keepdims=True)
            acc = acc + _dot_nt(p.astype(BF16), buf[kvh * ntop + j, 1].astype(BF16))
        o_s = acc / jnp.where(l > 0, l, 1.0)
        sw = _dot(q8, wint_ref[0, 0, kvh].astype(BF16)) * scale + bwin_ref[kvh]
        s_new = jnp.sum(q8f * winnew_ref[0, 0:1, ks], axis=-1, keepdims=True) * scale + bwin_new_ref[kvh]
        mw = jnp.maximum(sw.max(axis=-1, keepdims=True), s_new)
        pw = jnp.exp(sw - mw)
        p_new = jnp.exp(s_new - mw)
        lw = jnp.sum(pw, axis=-1, keepdims=True) + p_new
        o_w = (_dot_nt(pw.astype(BF16), wint_ref[0, 1, kvh].astype(BF16)) + p_new * winnew_ref[0, 0:1, vsl]) / lw
        g3 = gt[hs, :]
        o_ref[0, hs, :] = g3[:, 0:1] * oc_ref[0, hs, :] + g3[:, 1:2] * o_s[0:GQ] + g3[:, 2:3] * o_w[0:GQ]


def _sample_slc_win(picks, q_hm, oc, gates_h, cache_t, newt, wint, winnew, bslc, bwin, bwin_new, ntop):
    bs = q_hm.shape[0]
    hspec = pl.BlockSpec((1, N_HEADS, HEAD_DIM), lambda b, *_: (b, 0, 0))
    grid_spec = pltpu.PrefetchScalarGridSpec(
        num_scalar_prefetch=5,
        grid=(bs,),
        in_specs=[hspec, hspec, pl.BlockSpec((1, N_HEADS, 3), lambda b, *_: (b, 0, 0)),
                  pl.BlockSpec(memory_space=pl.ANY),
                  pl.BlockSpec((1,) + newt.shape[1:], lambda b, *_: (b, 0, 0, 0, 0)),
                  pl.BlockSpec((1,) + wint.shape[1:], lambda b, *_: (b, 0, 0, 0, 0)),
                  pl.BlockSpec((1,) + winnew.shape[1:], lambda b, *_: (b, 0, 0)),
                  pl.BlockSpec(bslc.shape, lambda b, *_: (0, 0, 0, 0)),
                  pl.BlockSpec(bwin.shape, lambda b, *_: (0, 0, 0)),
                  pl.BlockSpec(bwin_new.shape, lambda b, *_: (0, 0, 0))],
        out_specs=hspec,
        scratch_shapes=[pltpu.VMEM((N_KV_HEADS * ntop, 2, HEAD_DIM, PAGE_SIZE), F32), pltpu.SemaphoreType.DMA(())],
    )
    return pl.pallas_call(
        functools.partial(_sample_slc_win_kernel, ntop=ntop),
        grid_spec=grid_spec,
        out_shape=jax.ShapeDtypeStruct(q_hm.shape, F32),
        compiler_params=_cparams("arbitrary"),
        name="nsa_sample_slc_win",
    )(*picks, q_hm, oc, gates_h, cache_t, newt, wint, winnew, bslc, bwin, bwin_new)


def _nsa_sample(q, gates, kv_new, win_buf, cache, page_table, rel_bias, cmp_w1, cmp_b1, cmp_w2, cmp_pos):
    bs, npages = page_table.shape
    pool = cache.shape[0]
    hk = N_KV_HEADS * HEAD_DIM
    past = npages * PAGE_SIZE
    nch = past // CMP_STRIDE
    n_past_blk = past // SEL_BLOCK
    n_sel = n_past_blk + 1
    ntop = min(SEL_TOP, n_sel)
    nsp = -(-n_sel // LANES) * LANES
    wb = win_buf.shape[1]
    assert n_sel >= SEL_TOP and past >= wb
    cache_t = jnp.transpose(cache, (0, 2, 3, 4, 1))
    kvc = _paged_compress(page_table, cache_t.reshape(pool, 4, hk, PAGE_SIZE), cmp_w1, cmp_b1, cmp_w2, cmp_pos)
    tbl = _bias_by_dist(rel_bias)
    tok = np.arange(nch) - 1
    dist_c = np.clip(past - (tok * CMP_STRIDE + CMP_BLOCK - 1), 0, REL_MAX_DIST)
    bias_c = jnp.pad(tbl[:, dist_c].reshape(N_KV_HEADS, GQ, nch), ((0, 0), (0, SUBLANES - GQ), (0, 0)))
    ov = np.zeros((nch, nsp), np.float32)
    ov[1:] = _overlap_matrix(nch - 1, nsp)[CMP_PAD:-CMP_PAD]
    q_hm = q.reshape(bs, N_HEADS, HEAD_DIM)
    oc, idx = _sample_cmp_attention(q_hm, kvc[0], kvc[1], jnp.asarray(ov, BF16), bias_c, n_sel, n_past_blk, ntop)
    idx = idx[:, :N_KV_HEADS, :ntop]
    ip = jnp.minimum(idx, n_past_blk - 1)
    per_page = PAGE_SIZE // SEL_BLOCK
    page = jnp.take_along_axis(page_table, (ip // per_page).reshape(bs, -1), axis=1)
    is_new = (idx >= n_past_blk).astype(I32)
    half = jnp.where(is_new == 1, 0, ip % per_page)
    d0 = (n_past_blk - idx) * SEL_BLOCK
    case = jnp.clip(n_past_blk - idx, 0, 3)
    picks = [a.reshape(-1).astype(I32) for a in (page, half, is_new, d0, case * per_page + half)]
    lanes = np.arange(PAGE_SIZE)
    dist_s = np.stack([np.clip(cs * SEL_BLOCK - (lanes - hf * SEL_BLOCK), 0, REL_MAX_DIST)
                       for cs in range(4) for hf in range(per_page)])
    nvar = dist_s.shape[0]
    bslc = jnp.pad(jnp.transpose(tbl[:, dist_s].reshape(N_KV_HEADS, GQ, nvar, PAGE_SIZE), (0, 2, 1, 3)),
                   ((0, 0), (0, 0), (0, SUBLANES - GQ), (0, 0)))
    dist_w = np.clip(wb - np.arange(wb), 0, REL_MAX_DIST)
    bwin = jnp.pad(tbl[:, dist_w].reshape(N_KV_HEADS, GQ, wb), ((0, 0), (0, SUBLANES - GQ), (0, 0)))
    bwin_new = jnp.pad(tbl[:, 0].reshape(N_KV_HEADS, GQ, 1), ((0, 0), (0, SUBLANES - GQ), (0, 0)))
    newt = jnp.pad(kv_new[:, 2:4][..., None], ((0, 0),) * 4 + ((0, PAGE_SIZE - 1),))
    winnew = jnp.pad(kv_new[:, 4:6].reshape(bs, 1, 2 * hk), ((0, 0), (0, SUBLANES - 1), (0, 0)))
    wint = jnp.transpose(win_buf, (0, 2, 3, 4, 1))
    o = _sample_slc_win(picks, q_hm, oc, gates.reshape(bs, N_HEADS, 3), cache_t, newt, wint, winnew,
                        bslc, bwin, bwin_new, ntop)
    return o.reshape(bs, N_HEADS * HEAD_DIM)


def _mods(mod_l, nb, mr):
    return [m.reshape(nb, mr, -1) for m in jnp.split(mod_l, 6, axis=-1)]


def _s5_layer(x, nb, rb, seq, norm_g, mods, s5w, dsk, w_glu_bf16, h0r, h0i):
    sh1, sc1, g1 = mods[0], mods[1], mods[2]
    d = x.shape[-1]
    ar, ai, wbr, wbi, wcr, wci = s5w
    act, st_r, st_i = _s5_scan(x.reshape(nb, rb, d), norm_g.reshape(1, d), sh1, sc1, dsk.reshape(1, d),
                               wbr, wbi, ar, ai, wcr, wci, h0r, h0i, seq=seq)
    return _glu_residual(act.reshape(-1, d), x, g1, w_glu_bf16, rb), st_r, st_i


def kernel(x_prompt, x_sample, c_prompt, c_sample, state_s5, cache_nsa_kv, cache_win_kv, page_table, w_mod, b_mod, norm_g, s5_lambda_re, s5_lambda_im, s5_log_dt, s5_b_re, s5_b_im, s5_c_re, s5_c_im, s5_d, s5_w_glu, kv_mod_w, kv_mod_b, kv_norm_g, w_kv, cmp_w1, cmp_b1, cmp_w2, cmp_pos, nsa_w_in, nsa_w_out, rel_bias, moe_w_router, moe_b_router, moe_w_gu, moe_b_gu, moe_w_down, moe_b_down, final_norm_g):
    bp, lp, d = x_prompt.shape
    bs, ls, _ = x_sample.shape
    assert ls == 1, "the sample path advances exactly one token per sequence"
    depth = w_mod.shape[0]
    n_a = s5_lambda_re.shape[0]
    hq = N_HEADS * HEAD_DIM
    groups = d // S5_GROUP

    c_all = jnp.concatenate([c_prompt, c_sample], axis=0)
    mod_all = _adaln(c_all, w_mod, b_mod)
    kvmod_all = _adaln(c_all, kv_mod_w[None], kv_mod_b[None])[0]
    s5ws = [_s5_weights(s5_lambda_re[i], s5_lambda_im[i], s5_log_dt[i], s5_b_re[i], s5_b_im[i],
                        s5_c_re[i], s5_c_im[i]) for i in range(n_a)]
    wglus = [s5_w_glu[i].astype(BF16) for i in range(n_a)]
    w_kv_b = w_kv.astype(BF16)
    n_in = nsa_w_in.shape[-1]
    n_in_pad = -(-n_in // LANES) * LANES
    w_ins = [jnp.pad(nsa_w_in[j], ((0, 0), (0, n_in_pad - n_in))).astype(BF16) for j in range(depth - n_a)]
    w_outs = [nsa_w_out[j].astype(BF16) for j in range(depth - n_a)]

    def run(x3, sl, seq, h0_all, mixer):
        if seq:
            nb, rb = x3.shape[0], x3.shape[1]
            mr = 1
        else:
            nb, rb = 1, x3.shape[0]
            mr = rb
        x = x3.reshape(-1, d)
        states = []
        kv = None
        for i in range(depth):
            mods = _mods(mod_all[i, sl], nb, mr)
            if i < n_a:
                ns = s5ws[i][0].shape[-1]
                h0r = h0_all[i][..., 0].reshape(nb, -1, ns)
                h0i = h0_all[i][..., 1].reshape(nb, -1, ns)
                x, sr, si = _s5_layer(x, nb, rb, seq, norm_g[i, 0], mods, s5ws[i], s5_d[i], wglus[i], h0r, h0i)
                states.append(jnp.stack([sr.reshape(-1, groups, S5_STATE), si.reshape(-1, groups, S5_STATE)], axis=-1))
            else:
                j = i - n_a
                if kv is None:
                    kvsh, kvsc = [m.reshape(nb, mr, d) for m in jnp.split(kvmod_all[sl], 2, axis=-1)]
                    kv = _normmod_matmul(x, rb, kv_norm_g.reshape(1, d), kvsh, kvsc, w_kv_b)
                proj = _normmod_matmul(x, rb, norm_g[i, 0].reshape(1, d), mods[0], mods[1], w_ins[j], sig_from=hq)
                o = mixer(proj[:, :hq], proj[:, hq:n_in], kv)
                x = _matmul_residual(o, x, mods[2], w_outs[j], rb)
            x = _moe(x, rb, norm_g[i, 1].reshape(1, d), mods[3], mods[4], mods[5], i, moe_w_router[i], moe_b_router[i],
                     moe_w_gu, moe_b_gu, moe_w_down, moe_b_down)
        return _final_norm(x, final_norm_g, rb), jnp.stack(states), kv

    def prompt_mixer(q, gates, kv):
        o = _nsa_prompt(q.reshape(bp, lp, hq), gates.reshape(bp, lp, -1),
                        kv.reshape(bp, lp, N_BRANCH_KV, N_KV_HEADS, HEAD_DIM), rel_bias, cmp_w1, cmp_b1, cmp_w2, cmp_pos)
        return o.reshape(bp * lp, hq)

    def sample_mixer(q, gates, kv):
        return _nsa_sample(q, gates, kv.reshape(bs, N_BRANCH_KV, N_KV_HEADS, HEAD_DIM), cache_win_kv, cache_nsa_kv,
                           page_table, rel_bias, cmp_w1, cmp_b1, cmp_w2, cmp_pos)

    h0_p = jnp.zeros((n_a, bp, groups, S5_STATE, 2), F32)
    y_p, st_p, kv_p = run(x_prompt, slice(0, bp), True, h0_p, prompt_mixer)
    y_s, st_s, kv_s = run(x_sample.reshape(bs, d), slice(bp, bp + bs), False, state_s5, sample_mixer)
    kv_p = kv_p.reshape(bp, lp, N_BRANCH_KV, N_KV_HEADS, HEAD_DIM)
    kv_s = kv_s.reshape(bs, ls, N_BRANCH_KV, N_KV_HEADS, HEAD_DIM)
    win_p = kv_p[:, -min(WINDOW, lp):, 4:6]
    win_s = jnp.concatenate([cache_win_kv, kv_s[:, :, 4:6]], axis=1)[:, -cache_win_kv.shape[1]:]
    return (y_p.reshape(bp, lp, d), y_s.reshape(bs, ls, d), st_p, st_s, kv_p[:, :, :4], kv_s[:, :, :4], win_p, win_s)
```

```python
import functools
import math

import numpy as np
import jax
import jax.numpy as jnp
from jax import lax
from jax.experimental import pallas as pl
from jax.experimental.pallas import tpu as pltpu

F32 = jnp.float32
BF16 = jnp.bfloat16
I32 = jnp.int32

S5_GROUP = 16
S5_STATE = 64
N_HEADS = 16
HEAD_DIM = 64
N_KV_HEADS = 4
GQ = N_HEADS // N_KV_HEADS
CMP_BLOCK = 32
CMP_STRIDE = 16
CMP_RATIO = CMP_BLOCK // CMP_STRIDE
SEL_BLOCK = 64
SEL_TOP = 16
WINDOW = 512
Q_BLOCK = 128
FORCE_SCORE = 1e4
N_BRANCH_KV = 6
REL_BUCKETS = 32
REL_MAX_DIST = 128
N_EXPERTS = 32
TOP_K = 4
SWIGLU_LIMIT = 7.0
SWIGLU_ALPHA = 1.702
RMS_EPS = 1e-5
PAGE_SIZE = 128

LANES = 128
SUBLANES = 8
VMEM_LIMIT = 56 * 1024 * 1024

MOE_ROWS = 256
FAR_GROUP = 4
S5_LANES = 256
S5_STATES = S5_LANES // S5_GROUP * S5_STATE
NEG = -1e30


def _cparams(*sem):
    return pltpu.CompilerParams(dimension_semantics=sem, vmem_limit_bytes=VMEM_LIMIT)


def _normmod(x, g, sh, sc):
    y = x * lax.rsqrt(jnp.mean(x * x, axis=-1, keepdims=True) + RMS_EPS)
    return (y * g) * (1.0 + sc) + sh


def _dot(a, b):
    return jnp.dot(a, b, preferred_element_type=F32)


def _dot_nt(a, b):
    return lax.dot_general(a, b, (((1,), (1,)), ((), ())), preferred_element_type=F32)


def _split(x):
    hi = x.astype(BF16)
    lo = (x - hi.astype(F32)).astype(BF16)
    return hi, lo


def _mod_spec(m, rb, tm):
    nb, mr, d = m.shape
    if mr == 1:
        return pl.BlockSpec((1, 1, d), lambda i, *_: ((i * tm) // rb, 0, 0))
    assert mr == rb and rb % tm == 0
    per = rb // tm
    return pl.BlockSpec((1, tm, d), lambda i, *_: (i // per, i % per, 0))


def _row_tile(rb, want):
    t = math.gcd(rb, want)
    return t


def _adaln_kernel(c_ref, w_ref, b_ref, o_ref):
    c = c_ref[...]
    a = (c * jax.nn.sigmoid(c)).astype(BF16)
    o_ref[0] = _dot(a, w_ref[0].astype(BF16)) + b_ref[0]


def _adaln(c, w, b):
    n, d = c.shape
    nl, _, nn = w.shape
    npad = -(-n // SUBLANES) * SUBLANES
    cp = jnp.pad(c, ((0, npad - n), (0, 0)))
    tn = math.gcd(nn, 1024)
    out = pl.pallas_call(
        _adaln_kernel,
        grid=(nl, nn // tn),
        in_specs=[pl.BlockSpec((npad, d), lambda l, j: (0, 0)),
                  pl.BlockSpec((1, d, tn), lambda l, j: (l, 0, j)),
                  pl.BlockSpec((1, 1, tn), lambda l, j: (l, 0, j))],
        out_specs=pl.BlockSpec((1, npad, tn), lambda l, j: (l, 0, j)),
        out_shape=jax.ShapeDtypeStruct((nl, npad, nn), F32),
        compiler_params=_cparams("parallel", "parallel"),
        name="adaln",
    )(cp, w, b.reshape(nl, 1, nn))
    return out[:, :n]


def _s5_disc_kernel(lr_ref, li_ref, ldt_ref, br_ref, bi_ref, ar_ref, ai_ref, bbr_ref, bbi_ref):
    lr, li = lr_ref[...], li_ref[...]
    dt = jnp.exp(ldt_ref[...])
    mag = jnp.exp(dt * lr)
    ar, ai = mag * jnp.cos(dt * li), mag * jnp.sin(dt * li)
    den = lr * lr + li * li
    nr, ni = ar - 1.0, ai
    fr = (nr * lr + ni * li) / den
    fi = (ni * lr - nr * li) / den
    br, bi = br_ref[...], bi_ref[...]
    ar_ref[...] = ar
    ai_ref[...] = ai
    bbr_ref[...] = fr * br - fi * bi
    bbi_ref[...] = fr * bi + fi * br


def _s5_discretize(lam_re, lam_im, log_dt, b_re, b_im):
    g, p = lam_re.shape
    c = b_re.shape[-1]
    full3 = lambda s: pl.BlockSpec(s, lambda: (0, 0, 0))
    return pl.pallas_call(
        _s5_disc_kernel,
        in_specs=[full3((g, p, 1)), full3((g, p, 1)), full3((g, 1, 1)), full3((g, p, c)), full3((g, p, c))],
        out_specs=[full3((g, p, 1)), full3((g, p, 1)), full3((g, p, c)), full3((g, p, c))],
        out_shape=[jax.ShapeDtypeStruct((g, p, 1), F32)] * 2 + [jax.ShapeDtypeStruct((g, p, c), F32)] * 2,
        name="s5_discretize",
    )(lam_re.reshape(g, p, 1), lam_im.reshape(g, p, 1), log_dt.reshape(g, 1, 1), b_re, b_im)


def _s5_kernel(x_ref, g_ref, sh_ref, sc_ref, d_ref, wbr_ref, wbi_ref, ar_ref, ai_ref, wcr_ref, wci_ref,
               h0r_ref, h0i_ref, y_ref, str_ref, sti_ref, bur, bui, sre, sim, car_r, car_i, *, seq, rows, nj):
    c = pl.program_id(1)
    h = _normmod(x_ref[0], g_ref[...], sh_ref[0], sc_ref[0])
    dsk = d_ref[...]
    for j in range(nj):
        lo, hi = j * S5_LANES, (j + 1) * S5_LANES
        slo, shi = j * S5_STATES, (j + 1) * S5_STATES
        hj = h[:, lo:hi]
        hb = hj.astype(BF16)
        ar, ai = ar_ref[:, slo:shi], ai_ref[:, slo:shi]
        if seq:
            @pl.when(c == 0)
            def _():
                car_r[j] = h0r_ref[0, :, slo:shi]
                car_i[j] = h0i_ref[0, :, slo:shi]
            bur[...] = _dot(hb, wbr_ref[j])
            bui[...] = _dot(hb, wbi_ref[j])

            def step(t, carry):
                sr, si = carry
                br = bur[pl.ds(t, 1), :]
                bi = bui[pl.ds(t, 1), :]
                nr = ar * sr - ai * si + br
                ni = ar * si + ai * sr + bi
                sre[pl.ds(t, 1), :] = nr
                sim[pl.ds(t, 1), :] = ni
                return nr, ni

            sr, si = lax.fori_loop(0, rows, step, (car_r[j], car_i[j]), unroll=8)
            car_r[j] = sr
            car_i[j] = si
            str_ref[0, :, slo:shi] = sr
            sti_ref[0, :, slo:shi] = si
            s_r, s_i = sre[...], sim[...]
        else:
            h0r, h0i = h0r_ref[0, :, slo:shi], h0i_ref[0, :, slo:shi]
            s_r = ar * h0r - ai * h0i + _dot(hb, wbr_ref[j])
            s_i = ar * h0i + ai * h0r + _dot(hb, wbi_ref[j])
            str_ref[0, :, slo:shi] = s_r
            sti_ref[0, :, slo:shi] = s_i
        y = _dot(s_r.astype(BF16), wcr_ref[j]) - _dot(s_i.astype(BF16), wci_ref[j])
        y_ref[0, :, lo:hi] = jax.nn.gelu(y + dsk[:, lo:hi] * hj)


def _s5_scan(x3, g, sh, sc, dsk, wbr, wbi, ar, ai, wcr, wci, h0r, h0i, *, seq):
    nb, rb, d = x3.shape
    ns = ar.shape[-1]
    nj = d // S5_LANES
    rows = _row_tile(rb, 256)
    hr = 1 if seq else rows
    mrow = sh.shape[1]
    mspec = (pl.BlockSpec((1, 1, d), lambda b, c: (b, 0, 0)) if mrow == 1
             else pl.BlockSpec((1, rows, d), lambda b, c: (b, c, 0)))
    hspec = (pl.BlockSpec((1, 1, ns), lambda b, c: (b, 0, 0)) if seq
             else pl.BlockSpec((1, rows, ns), lambda b, c: (b, c, 0)))
    const2 = lambda s: pl.BlockSpec(s, lambda b, c: (0, 0))
    const3 = lambda s: pl.BlockSpec(s, lambda b, c: (0, 0, 0))
    kern = functools.partial(_s5_kernel, seq=seq, rows=rows, nj=nj)
    return pl.pallas_call(
        kern,
        grid=(nb, rb // rows),
        in_specs=[pl.BlockSpec((1, rows, d), lambda b, c: (b, c, 0)), const2((1, d)), mspec, mspec, const2((1, d)),
                  const3(wbr.shape), const3(wbi.shape), const2((1, ns)), const2((1, ns)),
                  const3(wcr.shape), const3(wci.shape), hspec, hspec],
        out_specs=[pl.BlockSpec((1, rows, d), lambda b, c: (b, c, 0)), hspec, hspec],
        out_shape=[jax.ShapeDtypeStruct((nb, rb, d), F32),
                   jax.ShapeDtypeStruct(h0r.shape, F32), jax.ShapeDtypeStruct(h0i.shape, F32)],
        scratch_shapes=[pltpu.VMEM((rows, S5_STATES), F32)] * 4 + [pltpu.VMEM((nj, 1, S5_STATES), F32)] * 2,
        compiler_params=_cparams("parallel", "arbitrary"),
        name="s5_scan" if seq else "s5_step",
    )(x3, g, sh, sc, dsk, wbr, wbi, ar, ai, wcr, wci, h0r, h0i)


def _s5_weights(lam_re, lam_im, log_dt, b_re, b_im, c_re, c_im):
    ar, ai, bbr, bbi = _s5_discretize(lam_re, lam_im, log_dt, b_re, b_im)
    g, p, c = bbr.shape
    gs = S5_LANES // c
    nj = g // gs
    eye = jnp.eye(gs, dtype=F32)

    def wb(bb):
        w = jnp.einsum('jgpc,gh->jgchp', bb.reshape(nj, gs, p, c), eye)
        return w.reshape(nj, gs * c, gs * p).astype(BF16)

    def wc(cc):
        w = jnp.einsum('jgcp,gh->jgphc', cc.reshape(nj, gs, c, p), eye)
        return w.reshape(nj, gs * p, gs * c).astype(BF16)

    return (ar.reshape(1, g * p), ai.reshape(1, g * p), wb(bbr), wb(bbi), wc(c_re), wc(c_im))


def _glu_kernel(a_ref, x_ref, g1_ref, w_ref, o_ref, *, d):
    gu = _dot(a_ref[...].astype(BF16), w_ref[...])
    o_ref[...] = x_ref[...] + g1_ref[0] * (gu[:, :d] * jax.nn.sigmoid(gu[:, d:]))


def _glu_residual(a, x, g1, w_bf16, rb):
    t, d = x.shape
    tm = _row_tile(rb, 512)
    return pl.pallas_call(
        functools.partial(_glu_kernel, d=d),
        grid=(t // tm,),
        in_specs=[pl.BlockSpec((tm, d), lambda i: (i, 0)), pl.BlockSpec((tm, d), lambda i: (i, 0)),
                  _mod_spec(g1, rb, tm), pl.BlockSpec(w_bf16.shape, lambda i: (0, 0))],
        out_specs=pl.BlockSpec((tm, d), lambda i: (i, 0)),
        out_shape=jax.ShapeDtypeStruct((t, d), F32),
        compiler_params=_cparams("parallel"),
        name="glu_residual",
    )(a, x, g1, w_bf16)


def _router_kernel(x_ref, g_ref, sh_ref, sc_ref, w_ref, b_ref, ti_ref, tw_ref, rk_ref, cnt_ref, carry, *, tm):
    i = pl.program_id(0)

    @pl.when(i == 0)
    def _():
        carry[...] = jnp.zeros_like(carry)

    h = _normmod(x_ref[...], g_ref[...], sh_ref[0], sc_ref[0])
    h_hi, h_lo = _split(h)
    w_hi, w_lo = _split(w_ref[...])
    logits = _dot(h_hi, w_hi) + (_dot(h_lo, w_hi) + _dot(h_hi, w_lo)) + b_ref[...]
    lane = lax.broadcasted_iota(I32, (tm, LANES), 1)
    lane_f = lane.astype(F32)
    work = jnp.where(lane < N_EXPERTS, logits, -jnp.inf)
    vals, sels, idxs = [], [], []
    for _ in range(TOP_K):
        m = jnp.max(work, axis=-1, keepdims=True)
        idx = jnp.min(jnp.where(work == m, lane_f, float(LANES)), axis=-1, keepdims=True)
        sel = lane_f == idx
        work = jnp.where(sel, -jnp.inf, work)
        vals.append(m)
        sels.append(sel)
        idxs.append(idx)
    es = [jnp.exp(v - vals[0]) for v in vals]
    tot = es[0] + es[1] + es[2] + es[3]
    oh = jnp.zeros((tm, LANES), F32)
    for sel in sels:
        oh = oh + sel.astype(F32)
    r_io = lax.broadcasted_iota(I32, (tm, tm), 0)
    c_io = lax.broadcasted_iota(I32, (tm, tm), 1)
    lstrict = (c_io < r_io).astype(BF16)
    base = carry[...] + _dot(lstrict, oh.astype(BF16))
    ti = jnp.zeros((tm, LANES), F32)
    tw = jnp.zeros((tm, LANES), F32)
    rk = jnp.zeros((tm, LANES), F32)
    for k in range(TOP_K):
        rank = jnp.sum(jnp.where(sels[k], base, 0.0), axis=-1, keepdims=True)
        ti = jnp.where(lane == k, idxs[k], ti)
        tw = jnp.where(lane == k, es[k] / tot, tw)
        rk = jnp.where(lane == k, rank, rk)
    ti_ref[...] = ti.astype(I32)
    tw_ref[...] = tw
    rk_ref[...] = rk.astype(I32)
    carry[...] = carry[...] + jnp.sum(oh, axis=0, keepdims=True)
    cnt_ref[...] = carry[...]


def _router(x, rb, g, sh, sc, w_pad, b_pad):
    t, d = x.shape
    tm = _row_tile(rb, 512)
    row = lambda i: (i, 0)
    return pl.pallas_call(
        functools.partial(_router_kernel, tm=tm),
        grid=(t // tm,),
        in_specs=[pl.BlockSpec((tm, d), row), pl.BlockSpec((1, d), lambda i: (0, 0)),
                  _mod_spec(sh, rb, tm), _mod_spec(sc, rb, tm),
                  pl.BlockSpec((d, LANES), lambda i: (0, 0)), pl.BlockSpec((1, LANES), lambda i: (0, 0))],
        out_specs=[pl.BlockSpec((tm, LANES), row)] * 3 + [pl.BlockSpec((1, LANES), lambda i: (0, 0))],
        out_shape=[jax.ShapeDtypeStruct((t, LANES), I32), jax.ShapeDtypeStruct((t, LANES), F32),
                   jax.ShapeDtypeStruct((t, LANES), I32), jax.ShapeDtypeStruct((1, LANES), F32)],
        scratch_shapes=[pltpu.VMEM((1, LANES), F32)],
        compiler_params=_cparams("arbitrary"),
        name="moe_router",
    )(x, g, sh, sc, w_pad, b_pad)


def _row_copy(src, dst, sem):
    return pltpu.make_async_copy(src, dst, sem)


def _dispatch_kernel(dest_ref, x_ref, g_ref, sh_ref, sc_ref, xs_in_ref, xs_ref, hbuf, sem, *, tm):
    del xs_in_ref
    hbuf[...] = _normmod(x_ref[...], g_ref[...], sh_ref[0], sc_ref[0])

    def start(r, _):
        for k in range(TOP_K):
            d = dest_ref[r * TOP_K + k]
            _row_copy(hbuf.at[pl.ds(r, 1)], xs_ref.at[pl.ds(d, 1)], sem).start(priority=k % 2)
        return 0

    lax.fori_loop(0, tm, start, 0)

    def wait(r, _):
        for k in range(TOP_K):
            _row_copy(hbuf.at[pl.ds(0, 1)], xs_ref.at[pl.ds(0, 1)], sem).wait()
        return 0

    lax.fori_loop(0, tm, wait, 0)


def _dispatch(dest, x, rb, g, sh, sc, n_rows):
    t, d = x.shape
    tm = _row_tile(rb, 256)
    zeros = jnp.zeros((n_rows, d), F32)
    return pl.pallas_call(
        functools.partial(_dispatch_kernel, tm=tm),
        grid=(t // tm,),
        in_specs=[pl.BlockSpec((tm * TOP_K,), lambda i: (i,), memory_space=pltpu.SMEM),
                  pl.BlockSpec((tm, d), lambda i: (i, 0)), pl.BlockSpec((1, d), lambda i: (0, 0)),
                  _mod_spec(sh, rb, tm), _mod_spec(sc, rb, tm),
                  pl.BlockSpec(memory_space=pl.ANY)],
        out_specs=pl.BlockSpec(memory_space=pl.ANY),
        out_shape=jax.ShapeDtypeStruct((n_rows, d), F32),
        scratch_shapes=[pltpu.VMEM((tm, d), F32), pltpu.SemaphoreType.DMA(())],
        input_output_aliases={5: 0},
        compiler_params=_cparams("arbitrary"),
        name="moe_dispatch",
    )(dest, x, g, sh, sc, zeros)


def _expert_kernel(be_ref, xs_ref, wgu_ref, bgu_ref, wd_ref, bd_ref, ys_ref, wgu_s, wd_s, *, de):
    i = pl.program_id(0)
    changed = jnp.logical_or(i == 0, be_ref[i] != be_ref[jnp.maximum(i - 1, 0)])

    @pl.when(changed)
    def _():
        wgu_s[...] = wgu_ref[0].astype(BF16)
        wd_s[...] = wd_ref[0].astype(BF16)

    gu = _dot(xs_ref[...].astype(BF16), wgu_s[...]) + bgu_ref[0]
    gate = jnp.minimum(gu[:, :de], SWIGLU_LIMIT)
    up = jnp.clip(gu[:, de:], -SWIGLU_LIMIT, SWIGLU_LIMIT)
    act = (up + 1.0) * gate * jax.nn.sigmoid(SWIGLU_ALPHA * gate)
    ys_ref[...] = _dot(act.astype(BF16), wd_s[...]) + bd_ref[0]


def _experts(blk_e, xs, layer, w_gu, b_gu, w_down, b_down):
    n_rows, d = xs.shape
    nl, ne, _, de2 = w_gu.shape
    de = de2 // 2
    n_blk = n_rows // MOE_ROWS
    grid_spec = pltpu.PrefetchScalarGridSpec(
        num_scalar_prefetch=1,
        grid=(n_blk,),
        in_specs=[pl.BlockSpec((MOE_ROWS, d), lambda i, be: (i, 0)),
                  pl.BlockSpec((None, 1, d, de2), lambda i, be: (layer, be[i], 0, 0)),
                  pl.BlockSpec((None, 1, 1, de2), lambda i, be: (layer, be[i], 0, 0)),
                  pl.BlockSpec((None, 1, de, d), lambda i, be: (layer, be[i], 0, 0)),
                  pl.BlockSpec((None, 1, 1, d), lambda i, be: (layer, be[i], 0, 0))],
        out_specs=pl.BlockSpec((MOE_ROWS, d), lambda i, be: (i, 0)),
        scratch_shapes=[pltpu.VMEM((d, de2), BF16), pltpu.VMEM((de, d), BF16)],
    )
    return pl.pallas_call(
        functools.partial(_expert_kernel, de=de),
        grid_spec=grid_spec,
        out_shape=jax.ShapeDtypeStruct((n_rows, d), F32),
        compiler_params=_cparams("arbitrary"),
        name="moe_experts",
    )(blk_e, xs, w_gu, b_gu.reshape(nl, ne, 1, de2), w_down, b_down.reshape(nl, ne, 1, d))


def _combine_kernel(dest_ref, tw_ref, x_ref, g2_ref, ys_ref, o_ref, gbuf, sem, *, tm):
    def start(r, _):
        for k in range(TOP_K):
            d = dest_ref[r * TOP_K + k]
            _row_copy(ys_ref.at[pl.ds(d, 1)], gbuf.at[k, pl.ds(r, 1)], sem).start(priority=k % 2)
        return 0

    lax.fori_loop(0, tm, start, 0)

    def wait(r, _):
        for k in range(TOP_K):
            _row_copy(ys_ref.at[pl.ds(0, 1)], gbuf.at[k, pl.ds(0, 1)], sem).wait()
        return 0

    lax.fori_loop(0, tm, wait, 0)
    tw = tw_ref[...]
    y = tw[:, 0:1] * gbuf[0]
    for k in range(1, TOP_K):
        y = y + tw[:, k:k + 1] * gbuf[k]
    o_ref[...] = x_ref[...] + g2_ref[0] * y


def _combine(dest, tw, x, rb, g2, ys):
    t, d = x.shape
    tm = _row_tile(rb, 256)
    return pl.pallas_call(
        functools.partial(_combine_kernel, tm=tm),
        grid=(t // tm,),
        in_specs=[pl.BlockSpec((tm * TOP_K,), lambda i: (i,), memory_space=pltpu.SMEM),
                  pl.BlockSpec((tm, LANES), lambda i: (i, 0)), pl.BlockSpec((tm, d), lambda i: (i, 0)),
                  _mod_spec(g2, rb, tm), pl.BlockSpec(memory_space=pl.ANY)],
        out_specs=pl.BlockSpec((tm, d), lambda i: (i, 0)),
        out_shape=jax.ShapeDtypeStruct((t, d), F32),
        scratch_shapes=[pltpu.VMEM((TOP_K, tm, d), F32), pltpu.SemaphoreType.DMA(())],
        compiler_params=_cparams("arbitrary"),
        name="moe_combine",
    )(dest, tw, x, g2, ys)


def _moe(x, rb, g, sh, sc, g2, layer, w_router, b_router, w_gu, b_gu, w_down, b_down):
    t, d = x.shape
    ne = w_router.shape[1]
    w_pad = jnp.pad(w_router, ((0, 0), (0, LANES - ne)))
    b_pad = jnp.pad(b_router, (0, LANES - ne)).reshape(1, LANES)
    ti, tw, rk, cnt = _router(x, rb, g, sh, sc, w_pad, b_pad)
    counts = cnt[0, :ne].astype(I32)
    padded = (counts + MOE_ROWS - 1) // MOE_ROWS * MOE_ROWS
    pend = jnp.cumsum(padded)
    pstart = pend - padded
    dest = (pstart[ti[:, :TOP_K]] + rk[:, :TOP_K]).reshape(-1).astype(I32)
    n_blk = -(-(t * TOP_K) // MOE_ROWS) + ne
    first_row = jnp.arange(n_blk, dtype=I32)[:, None] * MOE_ROWS
    blk_e = jnp.minimum(jnp.sum((pend[None, :] <= first_row).astype(I32), axis=1), ne - 1)
    xs = _dispatch(dest, x, rb, g, sh, sc, n_blk * MOE_ROWS)
    ys = _experts(blk_e, xs, layer, w_gu, b_gu, w_down, b_down)
    return _combine(dest, tw, x, rb, g2, ys)


def _nm_matmul_kernel(x_ref, g_ref, sh_ref, sc_ref, w_ref, o_ref, *, sig_from):
    h = _normmod(x_ref[...], g_ref[...], sh_ref[0], sc_ref[0])
    o = _dot(h.astype(BF16), w_ref[...])
    if sig_from is not None:
        col = lax.broadcasted_iota(I32, o.shape, 1)
        o = jnp.where(col >= sig_from, jax.nn.sigmoid(o), o)
    o_ref[...] = o


def _normmod_matmul(x, rb, g, sh, sc, w_bf16, sig_from=None):
    t, d = x.shape
    n = w_bf16.shape[1]
    tm = _row_tile(rb, 512)
    return pl.pallas_call(
        functools.partial(_nm_matmul_kernel, sig_from=sig_from),
        grid=(t // tm,),
        in_specs=[pl.BlockSpec((tm, d), lambda i: (i, 0)), pl.BlockSpec((1, d), lambda i: (0, 0)),
                  _mod_spec(sh, rb, tm), _mod_spec(sc, rb, tm), pl.BlockSpec((d, n), lambda i: (0, 0))],
        out_specs=pl.BlockSpec((tm, n), lambda i: (i, 0)),
        out_shape=jax.ShapeDtypeStruct((t, n), F32),
        compiler_params=_cparams("parallel"),
        name="normmod_matmul",
    )(x, g, sh, sc, w_bf16)


def _mm_residual_kernel(a_ref, x_ref, g1_ref, w_ref, o_ref):
    o_ref[...] = x_ref[...] + g1_ref[0] * _dot(a_ref[...].astype(BF16), w_ref[...])


def _matmul_residual(a, x, g1, w_bf16, rb):
    t, d = x.shape
    k = a.shape[1]
    tm = _row_tile(rb, 512)
    return pl.pallas_call(
        _mm_residual_kernel,
        grid=(t // tm,),
        in_specs=[pl.BlockSpec((tm, k), lambda i: (i, 0)), pl.BlockSpec((tm, d), lambda i: (i, 0)),
                  _mod_spec(g1, rb, tm), pl.BlockSpec((k, d), lambda i: (0, 0))],
        out_specs=pl.BlockSpec((tm, d), lambda i: (i, 0)),
        out_shape=jax.ShapeDtypeStruct((t, d), F32),
        compiler_params=_cparams("parallel"),
        name="matmul_residual",
    )(a, x, g1, w_bf16)


def _final_norm_kernel(x_ref, g_ref, o_ref):
    x = x_ref[...]
    o_ref[...] = x * lax.rsqrt(jnp.mean(x * x, axis=-1, keepdims=True) + RMS_EPS) * g_ref[...]


def _final_norm(x, g, rb):
    t, d = x.shape
    tm = _row_tile(rb, 512)
    return pl.pallas_call(
        _final_norm_kernel,
        grid=(t // tm,),
        in_specs=[pl.BlockSpec((tm, d), lambda i: (i, 0)), pl.BlockSpec((1, d), lambda i: (0, 0))],
        out_specs=pl.BlockSpec((tm, d), lambda i: (i, 0)),
        out_shape=jax.ShapeDtypeStruct((t, d), F32),
        compiler_params=_cparams("parallel"),
        name="final_norm",
    )(x, g.reshape(1, d))


def _rel_bucket_table():
    n = np.arange(REL_MAX_DIST + 1)
    max_exact = REL_BUCKETS // 2
    nf = np.maximum(n, 1).astype(np.float32)
    large = max_exact + (np.log(nf / max_exact) / math.log(REL_MAX_DIST / max_exact)
                         * (REL_BUCKETS - max_exact)).astype(np.int32)
    return np.where(n < max_exact, n, np.minimum(large, REL_BUCKETS - 1)).astype(np.int32)


def _bias_by_dist(rel_bias):
    return rel_bias.astype(F32)[_rel_bucket_table()].T


def _stack_heads(tab):
    h, r, c = tab.shape
    return tab.reshape(N_KV_HEADS, GQ * r, c)


def _dist_tile(rows, cols, offset, col_step=1):
    d = offset + np.arange(rows)[:, None] - col_step * np.arange(cols)[None, :]
    return np.clip(d, 0, REL_MAX_DIST)


CMP_NEAR = 32
CMP_PAD = 16


def _overlap_matrix(n_cmp, n_sel_pad):
    ci = np.arange(n_cmp)[:, None] * CMP_STRIDE
    sj = np.arange(n_sel_pad)[None, :] * SEL_BLOCK
    ov = ((ci < sj + SEL_BLOCK) & (ci + CMP_BLOCK > sj)).astype(np.float32)
    return np.pad(ov, ((CMP_PAD, CMP_PAD), (0, 0)))


def _compress_kernel(x_ref, pos_ref, w1_ref, b1_ref, w2_ref, o_ref, *, hid):
    w1 = w1_ref[0]
    parts = _dot(x_ref[0, 0].astype(BF16), w1)
    pt = _dot(pos_ref[0].astype(BF16), w1)
    p0 = parts[:, :hid] + pt[0:1, :hid]
    p1 = parts[:, hid:] + pt[1:2, hid:]
    n = p1.shape[0]
    pre = p0 + pltpu.roll(p1, n - 1, axis=0) + b1_ref[0]
    o_ref[0, 0] = _dot(jax.nn.gelu(pre).astype(BF16), w2_ref[0])


def _compress(xc, cmp_w1, cmp_b1, cmp_w2, cmp_pos):
    _, ng, nch, kdim = xc.shape
    hid = cmp_w1.shape[-1]
    hd = cmp_w2.shape[-1]
    w1 = cmp_w1.reshape(2, CMP_RATIO, kdim, hid)
    w1cat = jnp.concatenate([w1[:, r] for r in range(CMP_RATIO)], axis=-1).astype(BF16)
    pos = jnp.pad(cmp_pos.reshape(2, CMP_RATIO, kdim), ((0, 0), (0, SUBLANES - CMP_RATIO), (0, 0)))
    return pl.pallas_call(
        functools.partial(_compress_kernel, hid=hid),
        grid=(2, ng),
        in_specs=[pl.BlockSpec((1, 1, nch, kdim), lambda c, n: (c, n, 0, 0)),
                  pl.BlockSpec((1, SUBLANES, kdim), lambda c, n: (c, 0, 0)),
                  pl.BlockSpec((1, kdim, CMP_RATIO * hid), lambda c, n: (c, 0, 0)),
                  pl.BlockSpec((1, 1, hid), lambda c, n: (c, 0, 0)),
                  pl.BlockSpec((1, hid, hd), lambda c, n: (c, 0, 0))],
        out_specs=pl.BlockSpec((1, 1, nch, hd), lambda c, n: (c, n, 0, 0)),
        out_shape=jax.ShapeDtypeStruct((2, ng, nch, hd), F32),
        compiler_params=_cparams("parallel", "parallel"),
        name="nsa_compress",
    )(xc, pos, w1cat, cmp_b1.reshape(2, 1, hid), cmp_w2.astype(BF16))


def _dot3_nt(a, b):
    a_hi, a_lo = _split(a)
    b_hi, b_lo = _split(b)
    return _dot_nt(a_hi, b_hi) + (_dot_nt(a_lo, b_hi) + _dot_nt(a_hi, b_lo))


def _eye(n):
    r = lax.broadcasted_iota(I32, (n, n), 0)
    c = lax.broadcasted_iota(I32, (n, n), 1)
    return (r == c).astype(BF16)


def _transpose_bf16(x, eye):
    return _dot_nt(eye, x).astype(BF16)


def _top_blocks_cols(score, ntop):
    row_f = lax.broadcasted_iota(I32, score.shape, 0).astype(F32)
    sel = jnp.zeros(score.shape, F32)
    for _ in range(ntop):
        m = jnp.max(score, axis=0, keepdims=True)
        idx = jnp.min(jnp.where(score == m, row_f, float(score.shape[0])), axis=0, keepdims=True)
        hit = row_f == idx
        sel = jnp.where(jnp.logical_and(hit, m > -jnp.inf), 1.0, sel)
        score = jnp.where(hit, -jnp.inf, score)
    return sel


def _cmp_attn_kernel(q_ref, kc_ref, vc_ref, vct_ref, ov_ref, ovt_ref, bnear_ref, bfar_ref, oct_ref, selt_ref,
                     *, nch, ntop, qblk):
    qb = pl.program_id(2)
    cols = GQ * qblk
    scale = HEAD_DIM ** -0.5
    c = pl.multiple_of(qb * (qblk // CMP_STRIDE), SUBLANES)
    eye = _eye(HEAD_DIM)
    q_hi, q_lo = _split(q_ref[0].reshape(cols, HEAD_DIM) * scale)
    qt_hi = _transpose_bf16(q_hi, eye)
    qt_lo = _transpose_bf16(q_lo, eye)

    def logits(k):
        k_hi, k_lo = _split(k)
        return _dot(k_hi, qt_hi) + (_dot(k_lo, qt_hi) + _dot(k_hi, qt_lo))

    s_far = logits(kc_ref[0, 0, CMP_PAD:CMP_PAD + nch, :]) + bfar_ref[0]
    s_near = logits(kc_ref[0, 0, pl.ds(c, CMP_NEAR), :]) + bnear_ref[0]
    mask_far = lax.broadcasted_iota(I32, (nch, cols), 0) < c - CMP_PAD
    rel = lax.broadcasted_iota(I32, (CMP_NEAR, cols), 0)
    i_io = lax.broadcasted_iota(I32, (CMP_NEAR, cols), 1) & (qblk - 1)
    mask_near = jnp.logical_and(CMP_STRIDE * (rel - CMP_PAD) + CMP_BLOCK - 1 <= i_io, rel + c - CMP_PAD >= 0)
    sf = jnp.where(mask_far, s_far, NEG)
    sn = jnp.where(mask_near, s_near, NEG)
    m = jnp.maximum(jnp.max(sf, axis=0, keepdims=True), jnp.max(sn, axis=0, keepdims=True))
    pf = jnp.where(mask_far, jnp.exp(sf - m), 0.0)
    pn = jnp.where(mask_near, jnp.exp(sn - m), 0.0)
    l = jnp.sum(pf, axis=0, keepdims=True) + jnp.sum(pn, axis=0, keepdims=True)
    linv = 1.0 / jnp.where(l > 0, l, 1.0)
    vnt = _transpose_bf16(vc_ref[0, 0, pl.ds(c, CMP_NEAR), :].astype(BF16), eye)
    o = _dot(vct_ref[0, 0].astype(BF16), pf.astype(BF16)) + _dot(vnt, pn.astype(BF16))
    oct_ref[0, 0, 0] = o * linv
    pfn = pf * linv
    pnn = pn * linv
    pgf = pfn[:, 0:qblk]
    pgn = pnn[:, 0:qblk]
    for g in range(1, GQ):
        pgf = pgf + pfn[:, g * qblk:(g + 1) * qblk]
        pgn = pgn + pnn[:, g * qblk:(g + 1) * qblk]
    f_hi, f_lo = _split(pgf)
    n_hi, n_lo = _split(pgn)
    nsp = ovt_ref.shape[0]
    ovt = ovt_ref[...]
    ovnt = _transpose_bf16(ov_ref[pl.ds(c, CMP_NEAR), :], _eye(nsp))
    imp = (_dot(ovt, f_hi) + _dot(ovnt, n_hi)) + (_dot(ovt, f_lo) + _dot(ovnt, n_lo))
    s_io = lax.broadcasted_iota(I32, (nsp, qblk), 0)
    t = qb * qblk + lax.broadcasted_iota(I32, (nsp, qblk), 1)
    cur = lax.shift_right_logical(t, int(math.log2(SEL_BLOCK)))
    forced = jnp.logical_or(s_io == 0, jnp.logical_or(s_io == cur, s_io == cur - 1))
    score = jnp.where(s_io * SEL_BLOCK <= t, imp + jnp.where(forced, FORCE_SCORE, 0.0), -jnp.inf)
    selt_ref[0, 0, 0] = jnp.where(_top_blocks_cols(score, ntop) > 0.5, 0.0, NEG)


def _cmp_attention(q_hm, kc_pad, vc_pad, vct, ov, ovt, bnear, bfar, nch, ntop):
    b, _, t, hd = q_hm.shape
    qblk = math.gcd(t, Q_BLOCK)
    nqb = t // qblk
    nsp = ov.shape[1]
    cols = GQ * qblk
    kern = functools.partial(_cmp_attn_kernel, nch=nch, ntop=ntop, qblk=qblk)
    kvspec = pl.BlockSpec((1, 1, kc_pad.shape[2], hd), lambda bb, k, i: (bb, k, 0, 0))
    return pl.pallas_call(
        kern,
        grid=(b, N_KV_HEADS, nqb),
        in_specs=[pl.BlockSpec((1, GQ, qblk, hd), lambda bb, k, i: (bb, k, i, 0)), kvspec, kvspec,
                  pl.BlockSpec((1, 1, hd, nch), lambda bb, k, i: (bb, k, 0, 0)),
                  pl.BlockSpec(ov.shape, lambda bb, k, i: (0, 0)), pl.BlockSpec(ovt.shape, lambda bb, k, i: (0, 0)),
                  pl.BlockSpec((1,) + bnear.shape[1:], lambda bb, k, i: (k, 0, 0)),
                  pl.BlockSpec((1,) + bfar.shape[1:], lambda bb, k, i: (k, 0, 0))],
        out_specs=[pl.BlockSpec((1, 1, 1, hd, cols), lambda bb, k, i: (bb, k, i, 0, 0)),
                   pl.BlockSpec((1, 1, 1, nsp, qblk), lambda bb, k, i: (bb, k, i, 0, 0))],
        out_shape=[jax.ShapeDtypeStruct((b, N_KV_HEADS, nqb, hd, cols), F32),
                   jax.ShapeDtypeStruct((b, N_KV_HEADS, nqb, nsp, qblk), F32)],
        compiler_params=_cparams("parallel", "parallel", "parallel"),
        name="nsa_cmp_attention",
    )(q_hm, kc_pad, vc_pad, vct, ov, ovt, bnear, bfar)


def _slc_win_kernel(q_ref, selt_ref, oct_ref, gate_ref, sk_ref, svt_ref, wk_ref, wvt_ref, bt_ref, bfar_ref, o_ref,
                    m_s, l_s, acc_s, *, qblk):
    qb = pl.program_id(2)
    cols = GQ * qblk
    scale = HEAD_DIM ** -0.5
    qt = _transpose_bf16((q_ref[0].reshape(cols, HEAD_DIM) * scale).astype(BF16), _eye(HEAD_DIM))
    bfar = bfar_ref[0]
    per = qblk // SEL_BLOCK

    def neg_unless(cond):
        return jnp.where(cond, 0.0, NEG)

    def sel_add(kt, extra):
        pieces = [jnp.broadcast_to(selt_ref[0, 0, 0, pl.ds(per * kt + e, 1), :] + extra, (SEL_BLOCK, qblk))
                  for e in range(per)]
        mk = jnp.concatenate(pieces, axis=0)
        return jnp.concatenate([mk] * GQ, axis=1)

    def reset():
        m_s[...] = jnp.full(m_s.shape, NEG, F32)
        l_s[...] = jnp.zeros(l_s.shape, F32)
        acc_s[...] = jnp.zeros(acc_s.shape, F32)

    def update(pieces, vts):
        s = jnp.concatenate(pieces, axis=0)
        vt = jnp.concatenate(vts, axis=1)
        m_prev = m_s[...]
        m_new = jnp.maximum(m_prev, jnp.max(s, axis=0, keepdims=True))
        alpha = jnp.exp(m_prev - m_new)
        p = jnp.exp(s - m_new)
        l_s[...] = alpha * l_s[...] + jnp.sum(p, axis=0, keepdims=True)
        acc_s[...] = alpha * acc_s[...] + _dot(vt, p.astype(BF16))
        m_s[...] = m_new

    def result():
        return acc_s[...] / l_s[...]

    def logits(k_ref, kt):
        return _dot(k_ref[0, 0, kt], qt)

    reset()
    n_far = jnp.maximum(qb - 1, 0)
    last = sk_ref.shape[2] - 1

    def far(i, carry):
        pieces, vts = [], []
        for e in range(FAR_GROUP):
            kt = i * FAR_GROUP + e
            ktc = jnp.minimum(kt, last)
            pieces.append(logits(sk_ref, ktc) + (sel_add(ktc, neg_unless(kt < n_far)) + bfar))
            vts.append(svt_ref[0, 0, ktc])
        update(pieces, vts)
        return carry

    lax.fori_loop(0, (n_far + FAR_GROUP - 1) // FAR_GROUP, far, 0)
    kt1 = jnp.maximum(qb - 1, 0)
    update([logits(sk_ref, kt1) + (sel_add(kt1, neg_unless(qb >= 1)) + bt_ref[0, 1]),
            logits(sk_ref, qb) + (sel_add(qb, 0.0) + bt_ref[0, 0])],
           [svt_ref[0, 0, kt1], svt_ref[0, 0, qb]])
    o_s = result()

    reset()
    nwin = WINDOW // qblk
    pieces, vts = [], []
    for u in range(nwin + 1):
        kt = qb - nwin + u
        ktc = jnp.maximum(kt, 0)
        delta = (nwin - u) * qblk
        if delta == 0:
            bias = bt_ref[0, 0]
        elif delta == qblk:
            bias = bt_ref[0, 1] + neg_unless(kt >= 0)
        elif u == 0:
            bias = bt_ref[0, 2] + neg_unless(kt >= 0)
        else:
            bias = bfar + neg_unless(kt >= 0)
        pieces.append(logits(wk_ref, ktc) + bias)
        vts.append(wvt_ref[0, 0, ktc])
    update(pieces, vts)
    o_w = result()

    gt = gate_ref[0, 0, 0]
    ot = gt[0:1] * oct_ref[0, 0, 0] + gt[1:2] * o_s + gt[2:3] * o_w
    o_ref[0, 0, 0] = ot.astype(BF16)


def _slc_win_attention(q_hm, selt, oct, gates_blk, sk, svt, wk, wvt, bt, bfar):
    b, _, t, hd = q_hm.shape
    qblk = math.gcd(t, Q_BLOCK)
    nqb = t // qblk
    nsp = selt.shape[-2]
    cols = GQ * qblk
    kspec = pl.BlockSpec((1, 1, nqb, qblk, hd), lambda bb, k, i: (bb, k, 0, 0, 0))
    vspec = pl.BlockSpec((1, 1, nqb, hd, qblk), lambda bb, k, i: (bb, k, 0, 0, 0))
    qspec = pl.BlockSpec((1, GQ, qblk, hd), lambda bb, k, i: (bb, k, i, 0))
    return pl.pallas_call(
        functools.partial(_slc_win_kernel, qblk=qblk),
        grid=(b, N_KV_HEADS, nqb),
        in_specs=[qspec, pl.BlockSpec((1, 1, 1, nsp, qblk), lambda bb, k, i: (bb, k, i, 0, 0)),
                  pl.BlockSpec((1, 1, 1, hd, cols), lambda bb, k, i: (bb, k, i, 0, 0)),
                  pl.BlockSpec((1, 1, 1, 3, cols), lambda bb, k, i: (bb, k, i, 0, 0)),
                  kspec, vspec, kspec, vspec,
                  pl.BlockSpec((1,) + bt.shape[1:], lambda bb, k, i: (k, 0, 0, 0)),
                  pl.BlockSpec((1,) + bfar.shape[1:], lambda bb, k, i: (k, 0, 0))],
        out_specs=pl.BlockSpec((1, 1, 1, hd, cols), lambda bb, k, i: (bb, k, i, 0, 0)),
        out_shape=jax.ShapeDtypeStruct((b, N_KV_HEADS, nqb, hd, cols), BF16),
        scratch_shapes=[pltpu.VMEM((1, cols), F32), pltpu.VMEM((1, cols), F32), pltpu.VMEM((hd, cols), F32)],
        compiler_params=_cparams("parallel", "parallel", "parallel"),
        name="nsa_slc_win_attention",
    )(q_hm, selt, oct, gates_blk, sk, svt, wk, wvt, bt, bfar)


def _cols_table(tab):
    h, nq, nk = tab.shape
    return jnp.transpose(tab.reshape(N_KV_HEADS, GQ, nq, nk), (0, 3, 1, 2)).reshape(N_KV_HEADS, nk, GQ * nq)


def _nsa_prompt(q, gates, kv, rel_bias, cmp_w1, cmp_b1, cmp_w2, cmp_pos):
    b, t, _ = q.shape
    qblk = math.gcd(t, Q_BLOCK)
    assert qblk == Q_BLOCK and t % Q_BLOCK == 0 and WINDOW % Q_BLOCK == 0
    nqb = t // qblk
    nch = t // CMP_STRIDE
    n_sel = t // SEL_BLOCK
    nsp = -(-n_sel // LANES) * LANES
    kv_hm = jnp.transpose(kv, (2, 0, 3, 1, 4))
    xc = kv_hm[:2].reshape(2, b * N_KV_HEADS, nch, CMP_STRIDE * HEAD_DIM)
    kvc = _compress(xc, cmp_w1, cmp_b1, cmp_w2, cmp_pos).reshape(2, b, N_KV_HEADS, nch, HEAD_DIM)
    vct = jnp.swapaxes(kvc[1], -1, -2)
    kvc = jnp.pad(kvc, ((0, 0), (0, 0), (0, 0), (CMP_PAD, CMP_PAD), (0, 0)))
    tbl = _bias_by_dist(rel_bias)
    bfar = _cols_table(jnp.broadcast_to(tbl[:, REL_MAX_DIST][:, None, None], (N_HEADS, qblk, 1)))
    bnear = _cols_table(tbl[:, _dist_tile(qblk, CMP_NEAR, CMP_STRIDE * CMP_PAD - CMP_BLOCK + 1, CMP_STRIDE)])
    qi = np.arange(qblk)[:, None]
    kj = np.arange(qblk)[None, :]
    far_tile = jnp.broadcast_to(tbl[:, REL_MAX_DIST][:, None, None], (N_HEADS, qblk, qblk))
    bt = jnp.stack([_cols_table(jnp.where(kj <= qi, tbl[:, _dist_tile(qblk, qblk, 0)], NEG)),
                    _cols_table(tbl[:, _dist_tile(qblk, qblk, qblk)]),
                    _cols_table(jnp.where(kj >= qi, far_tile, NEG))], axis=1)
    ov_np = _overlap_matrix(nch, nsp)
    ov = jnp.asarray(ov_np, BF16)
    ovt = jnp.asarray(ov_np[CMP_PAD:CMP_PAD + nch].T, BF16)
    q_hm = jnp.transpose(q.reshape(b, t, N_HEADS, HEAD_DIM), (0, 2, 1, 3))
    gates_blk = jnp.transpose(gates.reshape(b, nqb, qblk, N_KV_HEADS, GQ, 3), (0, 3, 1, 5, 4, 2))
    gates_blk = gates_blk.reshape(b, N_KV_HEADS, nqb, 3, GQ * qblk)
    oct, selt = _cmp_attention(q_hm, kvc[0], kvc[1], vct, ov, ovt, bnear, bfar, nch, min(SEL_TOP, n_sel))
    kvb = kv_hm[2:].astype(BF16).reshape(4, b, N_KV_HEADS, nqb, qblk, HEAD_DIM)
    sk, wk = kvb[0], kvb[2]
    svt, wvt = jnp.swapaxes(kvb[1], -1, -2), jnp.swapaxes(kvb[3], -1, -2)
    o_t = _slc_win_attention(q_hm, selt, oct, gates_blk, sk, svt, wk, wvt, bt, bfar)
    o_t = o_t.reshape(b, N_KV_HEADS, nqb, HEAD_DIM, GQ, qblk)
    return jnp.transpose(o_t, (0, 2, 5, 1, 4, 3)).reshape(b, t, N_HEADS * HEAD_DIM)


CMP_TAP_PAIRS = CMP_STRIDE // 2
PAGE_CHUNKS = PAGE_SIZE // CMP_STRIDE


def _paged_compress_kernel(pt_ref, cache_ref, wp_ref, w1_ref, pos_ref, b1_ref, w2_ref, o_ref, buft, buf, carry, sem,
                           *, pg, npages, hid):
    b, g = pl.program_id(0), pl.program_id(1)
    m = pg * PAGE_CHUNKS
    hk = N_KV_HEADS * HEAD_DIM
    pairs = hk // LANES

    def page_copy(p):
        page = pt_ref[b * npages + g * pg + p]
        return pltpu.make_async_copy(cache_ref.at[page, pl.ds(0, 2)], buft.at[p], sem)

    for p in range(pg):
        page_copy(p).start()

    @pl.when(g == 0)
    def _():
        carry[...] = jnp.zeros_like(carry)

    for p in range(pg):
        page_copy(p).wait()

    def to_rows(p, carry_):
        r0 = pl.multiple_of(p * PAGE_SIZE, PAGE_SIZE)
        for comp in range(2):
            for hp in range(pairs):
                plane = buft[p, comp, hp * LANES:(hp + 1) * LANES, :]
                buf[comp * pairs + hp, pl.ds(r0, PAGE_SIZE), :] = plane.T
        return carry_

    lax.fori_loop(0, pg, to_rows, 0)

    row0 = lax.broadcasted_iota(I32, (m, hid), 0) == 0
    for comp in range(2):
        pt = _dot(pos_ref[comp].astype(BF16), w1_ref[comp])
        for hp in range(N_KV_HEADS // 2):
            lg = (comp * hk + hp * 2 * HEAD_DIM) // LANES
            acc = jnp.zeros((m, 2 * CMP_RATIO * hid), F32)
            for u in range(CMP_TAP_PAIRS):
                taps = [buf[lg, pl.ds(2 * u + e, m, stride=CMP_STRIDE), :] for e in range(2)]
                acc = acc + _dot(jnp.concatenate(taps, axis=1).astype(BF16), wp_ref[comp, u])
            for e in range(2):
                kvh = 2 * hp + e
                base = e * CMP_RATIO * hid
                p0 = acc[:, base:base + hid] + pt[0:1, :hid]
                p1 = acc[:, base + hid:base + 2 * hid] + pt[1:2, hid:]
                prev0 = jnp.where(row0, carry[comp, kvh], pltpu.roll(p0, 1, axis=0))
                carry[comp, kvh] = p0[m - 1:m, :]
                pre = prev0 + p1 + b1_ref[comp]
                o_ref[comp, 0, kvh] = _dot(jax.nn.gelu(pre).astype(BF16), w2_ref[comp])


def _paged_compress(page_table, cache_t, cmp_w1, cmp_b1, cmp_w2, cmp_pos):
    bs, npages = page_table.shape
    hid = cmp_w1.shape[-1]
    kdim = CMP_STRIDE * HEAD_DIM
    pg = math.gcd(npages, 32)
    m = pg * PAGE_CHUNKS
    w1 = cmp_w1.reshape(2, CMP_RATIO, kdim, hid)
    w1cat = jnp.concatenate([w1[:, r] for r in range(CMP_RATIO)], axis=-1)
    n1 = CMP_RATIO * hid
    wt = w1cat.reshape(2, CMP_TAP_PAIRS, 2, HEAD_DIM, n1)
    zero = jnp.zeros_like(wt)
    wp = jnp.concatenate([jnp.concatenate([wt, zero], axis=-1), jnp.concatenate([zero, wt], axis=-1)], axis=3)
    wp = wp.reshape(2, CMP_TAP_PAIRS, 4 * HEAD_DIM, 2 * n1).astype(BF16)
    pos = jnp.pad(cmp_pos.reshape(2, CMP_RATIO, kdim), ((0, 0), (0, SUBLANES - CMP_RATIO), (0, 0)))
    nch = npages * PAGE_CHUNKS
    grid_spec = pltpu.PrefetchScalarGridSpec(
        num_scalar_prefetch=1,
        grid=(bs, npages // pg),
        in_specs=[pl.BlockSpec(memory_space=pl.ANY),
                  pl.BlockSpec(wp.shape, lambda b, g, pt: (0, 0, 0, 0)),
                  pl.BlockSpec((2, kdim, n1), lambda b, g, pt: (0, 0, 0)),
                  pl.BlockSpec((2, SUBLANES, kdim), lambda b, g, pt: (0, 0, 0)),
                  pl.BlockSpec((2, 1, hid), lambda b, g, pt: (0, 0, 0)),
                  pl.BlockSpec((2, hid, HEAD_DIM), lambda b, g, pt: (0, 0, 0))],
        out_specs=pl.BlockSpec((2, 1, N_KV_HEADS, m, HEAD_DIM), lambda b, g, pt: (0, b, 0, g, 0)),
        scratch_shapes=[pltpu.VMEM((pg, 2, N_KV_HEADS * HEAD_DIM, PAGE_SIZE), F32),
                        pltpu.VMEM((2 * N_KV_HEADS * HEAD_DIM // LANES, pg * PAGE_SIZE, LANES), F32),
                        pltpu.VMEM((2, N_KV_HEADS, 1, hid), F32), pltpu.SemaphoreType.DMA(())],
    )
    return pl.pallas_call(
        functools.partial(_paged_compress_kernel, pg=pg, npages=npages, hid=hid),
        grid_spec=grid_spec,
        out_shape=jax.ShapeDtypeStruct((2, bs, N_KV_HEADS, nch, HEAD_DIM), F32),
        compiler_params=_cparams("parallel", "arbitrary"),
        name="nsa_paged_compress",
    )(page_table.reshape(-1), cache_t, wp, w1cat.astype(BF16), pos, cmp_b1.reshape(2, 1, hid), cmp_w2.astype(BF16))


def _rows8(x):
    return jnp.concatenate([x, jnp.zeros((SUBLANES - x.shape[0], x.shape[1]), x.dtype)], axis=0)


def _sample_cmp_kernel(q_ref, kc_ref, vc_ref, ov_ref, bias_ref, oc_ref, idx_ref, *, nch, n_sel, n_past_blk, ntop):
    scale = HEAD_DIM ** -0.5
    nsp = ov_ref.shape[1]
    row = lax.broadcasted_iota(I32, (SUBLANES, nch), 0)
    lane = lax.broadcasted_iota(I32, (SUBLANES, nch), 1)
    mask = jnp.logical_and(lane >= 1, row < GQ)
    imp = jnp.zeros((SUBLANES, nsp), F32)
    row_s = lax.broadcasted_iota(I32, (SUBLANES, nsp), 0)
    for kvh in range(N_KV_HEADS):
        q8 = _rows8(q_ref[0, kvh * GQ:(kvh + 1) * GQ, :])
        s = jnp.where(mask, _dot3_nt(q8, kc_ref[0, kvh]) * scale + bias_ref[kvh], NEG)
        mx = jnp.max(s, axis=-1, keepdims=True)
        p = jnp.where(mask, jnp.exp(s - mx), 0.0)
        l = jnp.sum(p, axis=-1, keepdims=True)
        linv = 1.0 / jnp.where(l > 0, l, 1.0)
        o = _dot(p.astype(BF16), vc_ref[0, kvh].astype(BF16)) * linv
        oc_ref[0, kvh * GQ:(kvh + 1) * GQ, :] = o[0:GQ]
        p_hi, p_lo = _split(p * linv)
        contrib = _dot(p_hi, ov_ref[...]) + _dot(p_lo, ov_ref[...])
        imp = jnp.where(row_s == kvh, jnp.sum(contrib, axis=0, keepdims=True), imp)
    s_io = lax.broadcasted_iota(I32, (SUBLANES, nsp), 1)
    t = n_past_blk * SEL_BLOCK
    cur = t // SEL_BLOCK
    forced = jnp.logical_or(s_io == 0, jnp.logical_or(s_io == cur, s_io == cur - 1))
    valid = jnp.logical_and(s_io * SEL_BLOCK <= t, s_io < n_sel)
    score = jnp.where(valid, imp + jnp.where(forced, FORCE_SCORE, 0.0), -jnp.inf)
    lane_f = s_io.astype(F32)
    out_lane = lax.broadcasted_iota(I32, (SUBLANES, LANES), 1)
    picks = jnp.zeros((SUBLANES, LANES), F32)
    for it in range(ntop):
        mx = jnp.max(score, axis=-1, keepdims=True)
        idx = jnp.min(jnp.where(score == mx, lane_f, float(nsp)), axis=-1, keepdims=True)
        picks = jnp.where(out_lane == it, idx, picks)
        score = jnp.where(lane_f == idx, -jnp.inf, score)
    idx_ref[0] = picks.astype(I32)


def _sample_cmp_attention(q_hm, kc, vc, ov, bias, n_sel, n_past_blk, ntop):
    bs = q_hm.shape[0]
    nch = kc.shape[2]
    kvspec = pl.BlockSpec((1, N_KV_HEADS, nch, HEAD_DIM), lambda b: (b, 0, 0, 0))
    return pl.pallas_call(
        functools.partial(_sample_cmp_kernel, nch=nch, n_sel=n_sel, n_past_blk=n_past_blk, ntop=ntop),
        grid=(bs,),
        in_specs=[pl.BlockSpec((1, N_HEADS, HEAD_DIM), lambda b: (b, 0, 0)), kvspec, kvspec,
                  pl.BlockSpec(ov.shape, lambda b: (0, 0)), pl.BlockSpec(bias.shape, lambda b: (0, 0, 0))],
        out_specs=[pl.BlockSpec((1, N_HEADS, HEAD_DIM), lambda b: (b, 0, 0)),
                   pl.BlockSpec((1, SUBLANES, LANES), lambda b: (b, 0, 0))],
        out_shape=[jax.ShapeDtypeStruct(q_hm.shape, F32), jax.ShapeDtypeStruct((bs, SUBLANES, LANES), I32)],
        compiler_params=_cparams("parallel"),
        name="nsa_sample_cmp_attention",
    )(q_hm, kc, vc, ov, bias)


def _sample_slc_win_kernel(page_ref, half_ref, new_ref, d0_ref, var_ref,
                           q_ref, oc_ref, gate_ref, cache_ref, newt_ref, wint_ref, winnew_ref,
                           bslc_ref, bwin_ref, bwin_new_ref, o_ref, buf, sem, *, ntop):
    b = pl.program_id(0)
    scale = HEAD_DIM ** -0.5
    hk = N_KV_HEADS * HEAD_DIM
    nslots = N_KV_HEADS * ntop

    def plane_copies(slot):
        i = b * nslots + slot
        kvh = slot // ntop
        return [pltpu.make_async_copy(cache_ref.at[page_ref[i], 2 + e, kvh], buf.at[slot, e], sem) for e in range(2)]

    for slot in range(nslots):
        is_new = new_ref[b * nslots + slot] == 1

        @pl.when(jnp.logical_not(is_new))
        def _():
            for cp in plane_copies(slot):
                cp.start()

        @pl.when(is_new)
        def _():
            buf[slot] = newt_ref[0, :, slot // ntop]

    for slot in range(nslots):
        @pl.when(new_ref[b * nslots + slot] == 0)
        def _():
            for cp in plane_copies(slot):
                cp.wait()

    lane = lax.broadcasted_iota(I32, (SUBLANES, PAGE_SIZE), 1)
    gt = gate_ref[0]
    for kvh in range(N_KV_HEADS):
        hs = slice(kvh * GQ, (kvh + 1) * GQ)
        ks = slice(kvh * HEAD_DIM, (kvh + 1) * HEAD_DIM)
        vsl = slice(hk + kvh * HEAD_DIM, hk + (kvh + 1) * HEAD_DIM)
        q8f = _rows8(q_ref[0, hs, :])
        q8 = q8f.astype(BF16)
        ss, masks = [], []
        for j in range(ntop):
            slot = kvh * ntop + j
            i = b * nslots + slot
            s = _dot(q8, buf[slot, 0].astype(BF16)) * scale + bslc_ref[kvh, var_ref[i]]
            off = lane - half_ref[i] * SEL_BLOCK
            mask = jnp.logical_and(jnp.logical_and(off >= 0, off < SEL_BLOCK), off <= d0_ref[i])
            ss.append(jnp.where(mask, s, NEG))
            masks.append(mask)
        mx = ss[0].max(axis=-1, keepdims=True)
        for s in ss[1:]:
            mx = jnp.maximum(mx, s.max(axis=-1, keepdims=True))
        l = jnp.zeros((SUBLANES, 1), F32)
        acc = jnp.zeros((SUBLANES, HEAD_DIM), F32)
        for j in range(ntop):
            p = jnp.where(masks[j], jnp.exp(ss[j] - mx), 0.0)
            l = l + jnp.sum(p, axis=-1, keepdims=True)
            acc = acc + _dot_nt(p.astype(BF16), buf[kvh * ntop + j, 1].astype(BF16))
        o_s = acc / jnp.where(l > 0, l, 1.0)
        sw = _dot(q8, wint_ref[0, 0, kvh].astype(BF16)) * scale + bwin_ref[kvh]
        s_new = jnp.sum(q8f * winnew_ref[0, 0:1, ks], axis=-1, keepdims=True) * scale + bwin_new_ref[kvh]
        mw = jnp.maximum(sw.max(axis=-1, keepdims=True), s_new)
        pw = jnp.exp(sw - mw)
        p_new = jnp.exp(s_new - mw)
        lw = jnp.sum(pw, axis=-1, keepdims=True) + p_new
        o_w = (_dot_nt(pw.astype(BF16), wint_ref[0, 1, kvh].astype(BF16)) + p_new * winnew_ref[0, 0:1, vsl]) / lw
        g3 = gt[hs, :]
        o_ref[0, hs, :] = g3[:, 0:1] * oc_ref[0, hs, :] + g3[:, 1:2] * o_s[0:GQ] + g3[:, 2:3] * o_w[0:GQ]


def _sample_slc_win(picks, q_hm, oc, gates_h, cache_t, newt, wint, winnew, bslc, bwin, bwin_new, ntop):
    bs = q_hm.shape[0]
    hspec = pl.BlockSpec((1, N_HEADS, HEAD_DIM), lambda b, *_: (b, 0, 0))
    grid_spec = pltpu.PrefetchScalarGridSpec(
        num_scalar_prefetch=5,
        grid=(bs,),
        in_specs=[hspec, hspec, pl.BlockSpec((1, N_HEADS, 3), lambda b, *_: (b, 0, 0)),
                  pl.BlockSpec(memory_space=pl.ANY),
                  pl.BlockSpec((1,) + newt.shape[1:], lambda b, *_: (b, 0, 0, 0, 0)),
                  pl.BlockSpec((1,) + wint.shape[1:], lambda b, *_: (b, 0, 0, 0, 0)),
                  pl.BlockSpec((1,) + winnew.shape[1:], lambda b, *_: (b, 0, 0)),
                  pl.BlockSpec(bslc.shape, lambda b, *_: (0, 0, 0, 0)),
                  pl.BlockSpec(bwin.shape, lambda b, *_: (0, 0, 0)),
                  pl.BlockSpec(bwin_new.shape, lambda b, *_: (0, 0, 0))],
        out_specs=hspec,
        scratch_shapes=[pltpu.VMEM((N_KV_HEADS * ntop, 2, HEAD_DIM, PAGE_SIZE), F32), pltpu.SemaphoreType.DMA(())],
    )
    return pl.pallas_call(
        functools.partial(_sample_slc_win_kernel, ntop=ntop),
        grid_spec=grid_spec,
        out_shape=jax.ShapeDtypeStruct(q_hm.shape, F32),
        compiler_params=_cparams("arbitrary"),
        name="nsa_sample_slc_win",
    )(*picks, q_hm, oc, gates_h, cache_t, newt, wint, winnew, bslc, bwin, bwin_new)


def _nsa_sample(q, gates, kv_new, win_buf, cache, page_table, rel_bias, cmp_w1, cmp_b1, cmp_w2, cmp_pos):
    bs, npages = page_table.shape
    pool = cache.shape[0]
    hk = N_KV_HEADS * HEAD_DIM
    past = npages * PAGE_SIZE
    nch = past // CMP_STRIDE
    n_past_blk = past // SEL_BLOCK
    n_sel = n_past_blk + 1
    ntop = min(SEL_TOP, n_sel)
    nsp = -(-n_sel // LANES) * LANES
    wb = win_buf.shape[1]
    assert n_sel >= SEL_TOP and past >= wb
    cache_t = jnp.transpose(cache, (0, 2, 3, 4, 1))
    kvc = _paged_compress(page_table, cache_t.reshape(pool, 4, hk, PAGE_SIZE), cmp_w1, cmp_b1, cmp_w2, cmp_pos)
    tbl = _bias_by_dist(rel_bias)
    tok = np.arange(nch) - 1
    dist_c = np.clip(past - (tok * CMP_STRIDE + CMP_BLOCK - 1), 0, REL_MAX_DIST)
    bias_c = jnp.pad(tbl[:, dist_c].reshape(N_KV_HEADS, GQ, nch), ((0, 0), (0, SUBLANES - GQ), (0, 0)))
    ov = np.zeros((nch, nsp), np.float32)
    ov[1:] = _overlap_matrix(nch - 1, nsp)[CMP_PAD:-CMP_PAD]
    q_hm = q.reshape(bs, N_HEADS, HEAD_DIM)
    oc, idx = _sample_cmp_attention(q_hm, kvc[0], kvc[1], jnp.asarray(ov, BF16), bias_c, n_sel, n_past_blk, ntop)
    idx = idx[:, :N_KV_HEADS, :ntop]
    ip = jnp.minimum(idx, n_past_blk - 1)
    per_page = PAGE_SIZE // SEL_BLOCK
    page = jnp.take_along_axis(page_table, (ip // per_page).reshape(bs, -1), axis=1)
    is_new = (idx >= n_past_blk).astype(I32)
    half = jnp.where(is_new == 1, 0, ip % per_page)
    d0 = (n_past_blk - idx) * SEL_BLOCK
    case = jnp.clip(n_past_blk - idx, 0, 3)
    picks = [a.reshape(-1).astype(I32) for a in (page, half, is_new, d0, case * per_page + half)]
    lanes = np.arange(PAGE_SIZE)
    dist_s = np.stack([np.clip(cs * SEL_BLOCK - (lanes - hf * SEL_BLOCK), 0, REL_MAX_DIST)
                       for cs in range(4) for hf in range(per_page)])
    nvar = dist_s.shape[0]
    bslc = jnp.pad(jnp.transpose(tbl[:, dist_s].reshape(N_KV_HEADS, GQ, nvar, PAGE_SIZE), (0, 2, 1, 3)),
                   ((0, 0), (0, 0), (0, SUBLANES - GQ), (0, 0)))
    dist_w = np.clip(wb - np.arange(wb), 0, REL_MAX_DIST)
    bwin = jnp.pad(tbl[:, dist_w].reshape(N_KV_HEADS, GQ, wb), ((0, 0), (0, SUBLANES - GQ), (0, 0)))
    bwin_new = jnp.pad(tbl[:, 0].reshape(N_KV_HEADS, GQ, 1), ((0, 0), (0, SUBLANES - GQ), (0, 0)))
    newt = jnp.pad(kv_new[:, 2:4][..., None], ((0, 0),) * 4 + ((0, PAGE_SIZE - 1),))
    winnew = jnp.pad(kv_new[:, 4:6].reshape(bs, 1, 2 * hk), ((0, 0), (0, SUBLANES - 1), (0, 0)))
    wint = jnp.transpose(win_buf, (0, 2, 3, 4, 1))
    o = _sample_slc_win(picks, q_hm, oc, gates.reshape(bs, N_HEADS, 3), cache_t, newt, wint, winnew,
                        bslc, bwin, bwin_new, ntop)
    return o.reshape(bs, N_HEADS * HEAD_DIM)


def _mods(mod_l, nb, mr):
    return [m.reshape(nb, mr, -1) for m in jnp.split(mod_l, 6, axis=-1)]


def _s5_layer(x, nb, rb, seq, norm_g, mods, s5w, dsk, w_glu_bf16, h0r, h0i):
    sh1, sc1, g1 = mods[0], mods[1], mods[2]
    d = x.shape[-1]
    ar, ai, wbr, wbi, wcr, wci = s5w
    act, st_r, st_i = _s5_scan(x.reshape(nb, rb, d), norm_g.reshape(1, d), sh1, sc1, dsk.reshape(1, d),
                               wbr, wbi, ar, ai, wcr, wci, h0r, h0i, seq=seq)
    return _glu_residual(act.reshape(-1, d), x, g1, w_glu_bf16, rb), st_r, st_i


def kernel(x_prompt, x_sample, c_prompt, c_sample, state_s5, cache_nsa_kv, cache_win_kv, page_table, w_mod, b_mod, norm_g, s5_lambda_re, s5_lambda_im, s5_log_dt, s5_b_re, s5_b_im, s5_c_re, s5_c_im, s5_d, s5_w_glu, kv_mod_w, kv_mod_b, kv_norm_g, w_kv, cmp_w1, cmp_b1, cmp_w2, cmp_pos, nsa_w_in, nsa_w_out, rel_bias, moe_w_router, moe_b_router, moe_w_gu, moe_b_gu, moe_w_down, moe_b_down, final_norm_g):
    bp, lp, d = x_prompt.shape
    bs, ls, _ = x_sample.shape
    assert ls == 1, "the sample path advances exactly one token per sequence"
    depth = w_mod.shape[0]
    n_a = s5_lambda_re.shape[0]
    hq = N_HEADS * HEAD_DIM
    groups = d // S5_GROUP

    c_all = jnp.concatenate([c_prompt, c_sample], axis=0)
    mod_all = _adaln(c_all, w_mod, b_mod)
    kvmod_all = _adaln(c_all, kv_mod_w[None], kv_mod_b[None])[0]
    s5ws = [_s5_weights(s5_lambda_re[i], s5_lambda_im[i], s5_log_dt[i], s5_b_re[i], s5_b_im[i],
                        s5_c_re[i], s5_c_im[i]) for i in range(n_a)]
    wglus = [s5_w_glu[i].astype(BF16) for i in range(n_a)]
    w_kv_b = w_kv.astype(BF16)
    n_in = nsa_w_in.shape[-1]
    n_in_pad = -(-n_in // LANES) * LANES
    w_ins = [jnp.pad(nsa_w_in[j], ((0, 0), (0, n_in_pad - n_in))).astype(BF16) for j in range(depth - n_a)]
    w_outs = [nsa_w_out[j].astype(BF16) for j in range(depth - n_a)]

    def run(x3, sl, seq, h0_all, mixer):
        if seq:
            nb, rb = x3.shape[0], x3.shape[1]
            mr = 1
        else:
            nb, rb = 1, x3.shape[0]
            mr = rb
        x = x3.reshape(-1, d)
        states = []
        kv = None
        for i in range(depth):
            mods = _mods(mod_all[i, sl], nb, mr)
            if i < n_a:
                ns = s5ws[i][0].shape[-1]
                h0r = h0_all[i][..., 0].reshape(nb, -1, ns)
                h0i = h0_all[i][..., 1].reshape(nb, -1, ns)
                x, sr, si = _s5_layer(x, nb, rb, seq, norm_g[i, 0], mods, s5ws[i], s5_d[i], wglus[i], h0r, h0i)
                states.append(jnp.stack([sr.reshape(-1, groups, S5_STATE), si.reshape(-1, groups, S5_STATE)], axis=-1))
            else:
                j = i - n_a
                if kv is None:
                    kvsh, kvsc = [m.reshape(nb, mr, d) for m in jnp.split(kvmod_all[sl], 2, axis=-1)]
                    kv = _normmod_matmul(x, rb, kv_norm_g.reshape(1, d), kvsh, kvsc, w_kv_b)
                proj = _normmod_matmul(x, rb, norm_g[i, 0].reshape(1, d), mods[0], mods[1], w_ins[j], sig_from=hq)
                o = mixer(proj[:, :hq], proj[:, hq:n_in], kv)
                x = _matmul_residual(o, x, mods[2], w_outs[j], rb)
            x = _moe(x, rb, norm_g[i, 1].reshape(1, d), mods[3], mods[4], mods[5], i, moe_w_router[i], moe_b_router[i],
                     moe_w_gu, moe_b_gu, moe_w_down, moe_b_down)
        return _final_norm(x, final_norm_g, rb), jnp.stack(states), kv

    def prompt_mixer(q, gates, kv):
        o = _nsa_prompt(q.reshape(bp, lp, hq), gates.reshape(bp, lp, -1),
                        kv.reshape(bp, lp, N_BRANCH_KV, N_KV_HEADS, HEAD_DIM), rel_bias, cmp_w1, cmp_b1, cmp_w2, cmp_pos)
        return o.reshape(bp * lp, hq)

    def sample_mixer(q, gates, kv):
        return _nsa_sample(q, gates, kv.reshape(bs, N_BRANCH_KV, N_KV_HEADS, HEAD_DIM), cache_win_kv, cache_nsa_kv,
                           page_table, rel_bias, cmp_w1, cmp_b1, cmp_w2, cmp_pos)

    h0_p = jnp.zeros((n_a, bp, groups, S5_STATE, 2), F32)
    y_p, st_p, kv_p = run(x_prompt, slice(0, bp), True, h0_p, prompt_mixer)
    y_s, st_s, kv_s = run(x_sample.reshape(bs, d), slice(bp, bp + bs), False, state_s5, sample_mixer)
    kv_p = kv_p.reshape(bp, lp, N_BRANCH_KV, N_KV_HEADS, HEAD_DIM)
    kv_s = kv_s.reshape(bs, ls, N_BRANCH_KV, N_KV_HEADS, HEAD_DIM)
    win_p = kv_p[:, -min(WINDOW, lp):, 4:6]
    win_s = jnp.concatenate([cache_win_kv, kv_s[:, :, 4:6]], axis=1)[:, -cache_win_kv.shape[1]:]
    return (y_p.reshape(bp, lp, d), y_s.reshape(bs, ls, d), st_p, st_s, kv_p[:, :, :4], kv_s[:, :, :4], win_p, win_s)
```

```python
import functools
import math

import numpy as np
import jax
import jax.numpy as jnp
from jax import lax
from jax.experimental import pallas as pl
from jax.experimental.pallas import tpu as pltpu

F32 = jnp.float32
BF16 = jnp.bfloat16
I32 = jnp.int32

S5_GROUP = 16
S5_STATE = 64
N_HEADS = 16
HEAD_DIM = 64
N_KV_HEADS = 4
GQ = N_HEADS // N_KV_HEADS
CMP_BLOCK = 32
CMP_STRIDE = 16
CMP_RATIO = CMP_BLOCK // CMP_STRIDE
SEL_BLOCK = 64
SEL_TOP = 16
WINDOW = 512
Q_BLOCK = 128
FORCE_SCORE = 1e4
N_BRANCH_KV = 6
REL_BUCKETS = 32
REL_MAX_DIST = 128
N_EXPERTS = 32
TOP_K = 4
SWIGLU_LIMIT = 7.0
SWIGLU_ALPHA = 1.702
RMS_EPS = 1e-5
PAGE_SIZE = 128

LANES = 128
SUBLANES = 8
VMEM_LIMIT = 56 * 1024 * 1024

MOE_ROWS = 256
FAR_GROUP = 4
S5_LANES = 256
S5_STATES = S5_LANES // S5_GROUP * S5_STATE
NEG = -1e30


def _cparams(*sem):
    return pltpu.CompilerParams(dimension_semantics=sem, vmem_limit_bytes=VMEM_LIMIT)


def _normmod(x, g, sh, sc):
    y = x * lax.rsqrt(jnp.mean(x * x, axis=-1, keepdims=True) + RMS_EPS)
    return (y * g) * (1.0 + sc) + sh


def _dot(a, b):
    return jnp.dot(a, b, preferred_element_type=F32)


def _dot_nt(a, b):
    return lax.dot_general(a, b, (((1,), (1,)), ((), ())), preferred_element_type=F32)


def _split(x):
    hi = x.astype(BF16)
    lo = (x - hi.astype(F32)).astype(BF16)
    return hi, lo


def _mod_spec(m, rb, tm):
    nb, mr, d = m.shape
    if mr == 1:
        return pl.BlockSpec((1, 1, d), lambda i, *_: ((i * tm) // rb, 0, 0))
    assert mr == rb and rb % tm == 0
    per = rb // tm
    return pl.BlockSpec((1, tm, d), lambda i, *_: (i // per, i % per, 0))


def _row_tile(rb, want):
    t = math.gcd(rb, want)
    return t


def _adaln_kernel(c_ref, w_ref, b_ref, o_ref):
    c = c_ref[...]
    a = (c * jax.nn.sigmoid(c)).astype(BF16)
    o_ref[0] = _dot(a, w_ref[0].astype(BF16)) + b_ref[0]


def _adaln(c, w, b):
    n, d = c.shape
    nl, _, nn = w.shape
    npad = -(-n // SUBLANES) * SUBLANES
    cp = jnp.pad(c, ((0, npad - n), (0, 0)))
    tn = math.gcd(nn, 1024)
    out = pl.pallas_call(
        _adaln_kernel,
        grid=(nl, nn // tn),
        in_specs=[pl.BlockSpec((npad, d), lambda l, j: (0, 0)),
                  pl.BlockSpec((1, d, tn), lambda l, j: (l, 0, j)),
                  pl.BlockSpec((1, 1, tn), lambda l, j: (l, 0, j))],
        out_specs=pl.BlockSpec((1, npad, tn), lambda l, j: (l, 0, j)),
        out_shape=jax.ShapeDtypeStruct((nl, npad, nn), F32),
        compiler_params=_cparams("parallel", "parallel"),
        name="adaln",
    )(cp, w, b.reshape(nl, 1, nn))
    return out[:, :n]


def _s5_disc_kernel(lr_ref, li_ref, ldt_ref, br_ref, bi_ref, ar_ref, ai_ref, bbr_ref, bbi_ref):
    lr, li = lr_ref[...], li_ref[...]
    dt = jnp.exp(ldt_ref[...])
    mag = jnp.exp(dt * lr)
    ar, ai = mag * jnp.cos(dt * li), mag * jnp.sin(dt * li)
    den = lr * lr + li * li
    nr, ni = ar - 1.0, ai
    fr = (nr * lr + ni * li) / den
    fi = (ni * lr - nr * li) / den
    br, bi = br_ref[...], bi_ref[...]
    ar_ref[...] = ar
    ai_ref[...] = ai
    bbr_ref[...] = fr * br - fi * bi
    bbi_ref[...] = fr * bi + fi * br


def _s5_discretize(lam_re, lam_im, log_dt, b_re, b_im):
    g, p = lam_re.shape
    c = b_re.shape[-1]
    full3 = lambda s: pl.BlockSpec(s, lambda: (0, 0, 0))
    return pl.pallas_call(
        _s5_disc_kernel,
        in_specs=[full3((g, p, 1)), full3((g, p, 1)), full3((g, 1, 1)), full3((g, p, c)), full3((g, p, c))],
        out_specs=[full3((g, p, 1)), full3((g, p, 1)), full3((g, p, c)), full3((g, p, c))],
        out_shape=[jax.ShapeDtypeStruct((g, p, 1), F32)] * 2 + [jax.ShapeDtypeStruct((g, p, c), F32)] * 2,
        name="s5_discretize",
    )(lam_re.reshape(g, p, 1), lam_im.reshape(g, p, 1), log_dt.reshape(g, 1, 1), b_re, b_im)


def _s5_kernel(x_ref, g_ref, sh_ref, sc_ref, d_ref, wbr_ref, wbi_ref, ar_ref, ai_ref, wcr_ref, wci_ref,
               h0r_ref, h0i_ref, y_ref, str_ref, sti_ref, bur, bui, sre, sim, car_r, car_i, *, seq, rows, nj):
    c = pl.program_id(1)
    h = _normmod(x_ref[0], g_ref[...], sh_ref[0], sc_ref[0])
    dsk = d_ref[...]
    for j in range(nj):
        lo, hi = j * S5_LANES, (j + 1) * S5_LANES
        slo, shi = j * S5_STATES, (j + 1) * S5_STATES
        hj = h[:, lo:hi]
        hb = hj.astype(BF16)
        ar, ai = ar_ref[:, slo:shi], ai_ref[:, slo:shi]
        if seq:
            @pl.when(c == 0)
            def _():
                car_r[j] = h0r_ref[0, :, slo:shi]
                car_i[j] = h0i_ref[0, :, slo:shi]
            bur[...] = _dot(hb, wbr_ref[j])
            bui[...] = _dot(hb, wbi_ref[j])

            def step(t, carry):
                sr, si = carry
                br = bur[pl.ds(t, 1), :]
                bi = bui[pl.ds(t, 1), :]
                nr = ar * sr - ai * si + br
                ni = ar * si + ai * sr + bi
                sre[pl.ds(t, 1), :] = nr
                sim[pl.ds(t, 1), :] = ni
                return nr, ni

            sr, si = lax.fori_loop(0, rows, step, (car_r[j], car_i[j]), unroll=8)
            car_r[j] = sr
            car_i[j] = si
            str_ref[0, :, slo:shi] = sr
            sti_ref[0, :, slo:shi] = si
            s_r, s_i = sre[...], sim[...]
        else:
            h0r, h0i = h0r_ref[0, :, slo:shi], h0i_ref[0, :, slo:shi]
            s_r = ar * h0r - ai * h0i + _dot(hb, wbr_ref[j])
            s_i = ar * h0i + ai * h0r + _dot(hb, wbi_ref[j])
            str_ref[0, :, slo:shi] = s_r
            sti_ref[0, :, slo:shi] = s_i
        y = _dot(s_r.astype(BF16), wcr_ref[j]) - _dot(s_i.astype(BF16), wci_ref[j])
        y_ref[0, :, lo:hi] = jax.nn.gelu(y + dsk[:, lo:hi] * hj)


def _s5_scan(x3, g, sh, sc, dsk, wbr, wbi, ar, ai, wcr, wci, h0r, h0i, *, seq):
    nb, rb, d = x3.shape
    ns = ar.shape[-1]
    nj = d // S5_LANES
    rows = _row_tile(rb, 256)
    hr = 1 if seq else rows
    mrow = sh.shape[1]
    mspec = (pl.BlockSpec((1, 1, d), lambda b, c: (b, 0, 0)) if mrow == 1
             else pl.BlockSpec((1, rows, d), lambda b, c: (b, c, 0)))
    hspec = (pl.BlockSpec((1, 1, ns), lambda b, c: (b, 0, 0)) if seq
             else pl.BlockSpec((1, rows, ns), lambda b, c: (b, c, 0)))
    const2 = lambda s: pl.BlockSpec(s, lambda b, c: (0, 0))
    const3 = lambda s: pl.BlockSpec(s, lambda b, c: (0, 0, 0))
    kern = functools.partial(_s5_kernel, seq=seq, rows=rows, nj=nj)
    return pl.pallas_call(
        kern,
        grid=(nb, rb // rows),
        in_specs=[pl.BlockSpec((1, rows, d), lambda b, c: (b, c, 0)), const2((1, d)), mspec, mspec, const2((1, d)),
                  const3(wbr.shape), const3(wbi.shape), const2((1, ns)), const2((1, ns)),
                  const3(wcr.shape), const3(wci.shape), hspec, hspec],
        out_specs=[pl.BlockSpec((1, rows, d), lambda b, c: (b, c, 0)), hspec, hspec],
        out_shape=[jax.ShapeDtypeStruct((nb, rb, d), F32),
                   jax.ShapeDtypeStruct(h0r.shape, F32), jax.ShapeDtypeStruct(h0i.shape, F32)],
        scratch_shapes=[pltpu.VMEM((rows, S5_STATES), F32)] * 4 + [pltpu.VMEM((nj, 1, S5_STATES), F32)] * 2,
        compiler_params=_cparams("parallel", "arbitrary"),
        name="s5_scan" if seq else "s5_step",
    )(x3, g, sh, sc, dsk, wbr, wbi, ar, ai, wcr, wci, h0r, h0i)


def _s5_weights(lam_re, lam_im, log_dt, b_re, b_im, c_re, c_im):
    ar, ai, bbr, bbi = _s5_discretize(lam_re, lam_im, log_dt, b_re, b_im)
    g, p, c = bbr.shape
    gs = S5_LANES // c
    nj = g // gs
    eye = jnp.eye(gs, dtype=F32)

    def wb(bb):
        w = jnp.einsum('jgpc,gh->jgchp', bb.reshape(nj, gs, p, c), eye)
        return w.reshape(nj, gs * c, gs * p).astype(BF16)

    def wc(cc):
        w = jnp.einsum('jgcp,gh->jgphc', cc.reshape(nj, gs, c, p), eye)
        return w.reshape(nj, gs * p, gs * c).astype(BF16)

    return (ar.reshape(1, g * p), ai.reshape(1, g * p), wb(bbr), wb(bbi), wc(c_re), wc(c_im))


def _glu_kernel(a_ref, x_ref, g1_ref, w_ref, o_ref, *, d):
    gu = _dot(a_ref[...].astype(BF16), w_ref[...])
    o_ref[...] = x_ref[...] + g1_ref[0] * (gu[:, :d] * jax.nn.sigmoid(gu[:, d:]))


def _glu_residual(a, x, g1, w_bf16, rb):
    t, d = x.shape
    tm = _row_tile(rb, 512)
    return pl.pallas_call(
        functools.partial(_glu_kernel, d=d),
        grid=(t // tm,),
        in_specs=[pl.BlockSpec((tm, d), lambda i: (i, 0)), pl.BlockSpec((tm, d), lambda i: (i, 0)),
                  _mod_spec(g1, rb, tm), pl.BlockSpec(w_bf16.shape, lambda i: (0, 0))],
        out_specs=pl.BlockSpec((tm, d), lambda i: (i, 0)),
        out_shape=jax.ShapeDtypeStruct((t, d), F32),
        compiler_params=_cparams("parallel"),
        name="glu_residual",
    )(a, x, g1, w_bf16)


def _router_kernel(x_ref, g_ref, sh_ref, sc_ref, w_ref, b_ref, ti_ref, tw_ref, rk_ref, cnt_ref, carry, *, tm):
    i = pl.program_id(0)

    @pl.when(i == 0)
    def _():
        carry[...] = jnp.zeros_like(carry)

    h = _normmod(x_ref[...], g_ref[...], sh_ref[0], sc_ref[0])
    h_hi, h_lo = _split(h)
    w_hi, w_lo = _split(w_ref[...])
    logits = _dot(h_hi, w_hi) + (_dot(h_lo, w_hi) + _dot(h_hi, w_lo)) + b_ref[...]
    lane = lax.broadcasted_iota(I32, (tm, LANES), 1)
    lane_f = lane.astype(F32)
    work = jnp.where(lane < N_EXPERTS, logits, -jnp.inf)
    vals, sels, idxs = [], [], []
    for _ in range(TOP_K):
        m = jnp.max(work, axis=-1, keepdims=True)
        idx = jnp.min(jnp.where(work == m, lane_f, float(LANES)), axis=-1, keepdims=True)
        sel = lane_f == idx
        work = jnp.where(sel, -jnp.inf, work)
        vals.append(m)
        sels.append(sel)
        idxs.append(idx)
    es = [jnp.exp(v - vals[0]) for v in vals]
    tot = es[0] + es[1] + es[2] + es[3]
    oh = jnp.zeros((tm, LANES), F32)
    for sel in sels:
        oh = oh + sel.astype(F32)
    r_io = lax.broadcasted_iota(I32, (tm, tm), 0)
    c_io = lax.broadcasted_iota(I32, (tm, tm), 1)
    lstrict = (c_io < r_io).astype(BF16)
    base = carry[...] + _dot(lstrict, oh.astype(BF16))
    ti = jnp.zeros((tm, LANES), F32)
    tw = jnp.zeros((tm, LANES), F32)
    rk = jnp.zeros((tm, LANES), F32)
    for k in range(TOP_K):
        rank = jnp.sum(jnp.where(sels[k], base, 0.0), axis=-1, keepdims=True)
        ti = jnp.where(lane == k, idxs[k], ti)
        tw = jnp.where(lane == k, es[k] / tot, tw)
        rk = jnp.where(lane == k, rank, rk)
    ti_ref[...] = ti.astype(I32)
    tw_ref[...] = tw
    rk_ref[...] = rk.astype(I32)
    carry[...] = carry[...] + jnp.sum(oh, axis=0, keepdims=True)
    cnt_ref[...] = carry[...]


def _router(x, rb, g, sh, sc, w_pad, b_pad):
    t, d = x.shape
    tm = _row_tile(rb, 512)
    row = lambda i: (i, 0)
    return pl.pallas_call(
        functools.partial(_router_kernel, tm=tm),
        grid=(t // tm,),
        in_specs=[pl.BlockSpec((tm, d), row), pl.BlockSpec((1, d), lambda i: (0, 0)),
                  _mod_spec(sh, rb, tm), _mod_spec(sc, rb, tm),
                  pl.BlockSpec((d, LANES), lambda i: (0, 0)), pl.BlockSpec((1, LANES), lambda i: (0, 0))],
        out_specs=[pl.BlockSpec((tm, LANES), row)] * 3 + [pl.BlockSpec((1, LANES), lambda i: (0, 0))],
        out_shape=[jax.ShapeDtypeStruct((t, LANES), I32), jax.ShapeDtypeStruct((t, LANES), F32),
                   jax.ShapeDtypeStruct((t, LANES), I32), jax.ShapeDtypeStruct((1, LANES), F32)],
        scratch_shapes=[pltpu.VMEM((1, LANES), F32)],
        compiler_params=_cparams("arbitrary"),
        name="moe_router",
    )(x, g, sh, sc, w_pad, b_pad)


def _row_copy(src, dst, sem):
    return pltpu.make_async_copy(src, dst, sem)


def _dispatch_kernel(dest_ref, x_ref, g_ref, sh_ref, sc_ref, xs_in_ref, xs_ref, hbuf, sem, *, tm):
    del xs_in_ref
    hbuf[...] = _normmod(x_ref[...], g_ref[...], sh_ref[0], sc_ref[0])

    def start(r, _):
        for k in range(TOP_K):
            d = dest_ref[r * TOP_K + k]
            _row_copy(hbuf.at[pl.ds(r, 1)], xs_ref.at[pl.ds(d, 1)], sem).start(priority=k % 2)
        return 0

    lax.fori_loop(0, tm, start, 0)

    def wait(r, _):
        for k in range(TOP_K):
            _row_copy(hbuf.at[pl.ds(0, 1)], xs_ref.at[pl.ds(0, 1)], sem).wait()
        return 0

    lax.fori_loop(0, tm, wait, 0)


def _dispatch(dest, x, rb, g, sh, sc, n_rows):
    t, d = x.shape
    tm = _row_tile(rb, 256)
    zeros = jnp.zeros((n_rows, d), F32)
    return pl.pallas_call(
        functools.partial(_dispatch_kernel, tm=tm),
        grid=(t // tm,),
        in_specs=[pl.BlockSpec((tm * TOP_K,), lambda i: (i,), memory_space=pltpu.SMEM),
                  pl.BlockSpec((tm, d), lambda i: (i, 0)), pl.BlockSpec((1, d), lambda i: (0, 0)),
                  _mod_spec(sh, rb, tm), _mod_spec(sc, rb, tm),
                  pl.BlockSpec(memory_space=pl.ANY)],
        out_specs=pl.BlockSpec(memory_space=pl.ANY),
        out_shape=jax.ShapeDtypeStruct((n_rows, d), F32),
        scratch_shapes=[pltpu.VMEM((tm, d), F32), pltpu.SemaphoreType.DMA(())],
        input_output_aliases={5: 0},
        compiler_params=_cparams("arbitrary"),
        name="moe_dispatch",
    )(dest, x, g, sh, sc, zeros)


def _expert_kernel(be_ref, xs_ref, wgu_ref, bgu_ref, wd_ref, bd_ref, ys_ref, wgu_s, wd_s, *, de):
    i = pl.program_id(0)
    changed = jnp.logical_or(i == 0, be_ref[i] != be_ref[jnp.maximum(i - 1, 0)])

    @pl.when(changed)
    def _():
        wgu_s[...] = wgu_ref[0].astype(BF16)
        wd_s[...] = wd_ref[0].astype(BF16)

    gu = _dot(xs_ref[...].astype(BF16), wgu_s[...]) + bgu_ref[0]
    gate = jnp.minimum(gu[:, :de], SWIGLU_LIMIT)
    up = jnp.clip(gu[:, de:], -SWIGLU_LIMIT, SWIGLU_LIMIT)
    act = (up + 1.0) * gate * jax.nn.sigmoid(SWIGLU_ALPHA * gate)
    ys_ref[...] = _dot(act.astype(BF16), wd_s[...]) + bd_ref[0]


def _experts(blk_e, xs, layer, w_gu, b_gu, w_down, b_down):
    n_rows, d = xs.shape
    nl, ne, _, de2 = w_gu.shape
    de = de2 // 2
    n_blk = n_rows // MOE_ROWS
    grid_spec = pltpu.PrefetchScalarGridSpec(
        num_scalar_prefetch=1,
        grid=(n_blk,),
        in_specs=[pl.BlockSpec((MOE_ROWS, d), lambda i, be: (i, 0)),
                  pl.BlockSpec((None, 1, d, de2), lambda i, be: (layer, be[i], 0, 0)),
                  pl.BlockSpec((None, 1, 1, de2), lambda i, be: (layer, be[i], 0, 0)),
                  pl.BlockSpec((None, 1, de, d), lambda i, be: (layer, be[i], 0, 0)),
                  pl.BlockSpec((None, 1, 1, d), lambda i, be: (layer, be[i], 0, 0))],
        out_specs=pl.BlockSpec((MOE_ROWS, d), lambda i, be: (i, 0)),
        scratch_shapes=[pltpu.VMEM((d, de2), BF16), pltpu.VMEM((de, d), BF16)],
    )
    return pl.pallas_call(
        functools.partial(_expert_kernel, de=de),
        grid_spec=grid_spec,
        out_shape=jax.ShapeDtypeStruct((n_rows, d), F32),
        compiler_params=_cparams("arbitrary"),
        name="moe_experts",
    )(blk_e, xs, w_gu, b_gu.reshape(nl, ne, 1, de2), w_down, b_down.reshape(nl, ne, 1, d))


def _combine_kernel(dest_ref, tw_ref, x_ref, g2_ref, ys_ref, o_ref, gbuf, sem, *, tm):
    def start(r, _):
        for k in range(TOP_K):
            d = dest_ref[r * TOP_K + k]
            _row_copy(ys_ref.at[pl.ds(d, 1)], gbuf.at[k, pl.ds(r, 1)], sem).start(priority=k % 2)
        return 0

    lax.fori_loop(0, tm, start, 0)

    def wait(r, _):
        for k in range(TOP_K):
            _row_copy(ys_ref.at[pl.ds(0, 1)], gbuf.at[k, pl.ds(0, 1)], sem).wait()
        return 0

    lax.fori_loop(0, tm, wait, 0)
    tw = tw_ref[...]
    y = tw[:, 0:1] * gbuf[0]
    for k in range(1, TOP_K):
        y = y + tw[:, k:k + 1] * gbuf[k]
    o_ref[...] = x_ref[...] + g2_ref[0] * y


def _combine(dest, tw, x, rb, g2, ys):
    t, d = x.shape
    tm = _row_tile(rb, 256)
    return pl.pallas_call(
        functools.partial(_combine_kernel, tm=tm),
        grid=(t // tm,),
        in_specs=[pl.BlockSpec((tm * TOP_K,), lambda i: (i,), memory_space=pltpu.SMEM),
                  pl.BlockSpec((tm, LANES), lambda i: (i, 0)), pl.BlockSpec((tm, d), lambda i: (i, 0)),
                  _mod_spec(g2, rb, tm), pl.BlockSpec(memory_space=pl.ANY)],
        out_specs=pl.BlockSpec((tm, d), lambda i: (i, 0)),
        out_shape=jax.ShapeDtypeStruct((t, d), F32),
        scratch_shapes=[pltpu.VMEM((TOP_K, tm, d), F32), pltpu.SemaphoreType.DMA(())],
        compiler_params=_cparams("arbitrary"),
        name="moe_combine",
    )(dest, tw, x, g2, ys)


def _moe(x, rb, g, sh, sc, g2, layer, w_router, b_router, w_gu, b_gu, w_down, b_down):
    t, d = x.shape
    ne = w_router.shape[1]
    w_pad = jnp.pad(w_router, ((0, 0), (0, LANES - ne)))
    b_pad = jnp.pad(b_router, (0, LANES - ne)).reshape(1, LANES)
    ti, tw, rk, cnt = _router(x, rb, g, sh, sc, w_pad, b_pad)
    counts = cnt[0, :ne].astype(I32)
    padded = (counts + MOE_ROWS - 1) // MOE_ROWS * MOE_ROWS
    pend = jnp.cumsum(padded)
    pstart = pend - padded
    top_i = ti[:, :TOP_K]
    start_of = jnp.sum(jnp.where(top_i[..., None] == jnp.arange(ne, dtype=I32), pstart, 0), axis=-1)
    dest = (start_of + rk[:, :TOP_K]).reshape(-1).astype(I32)
    n_blk = -(-(t * TOP_K) // MOE_ROWS) + ne
    first_row = jnp.arange(n_blk, dtype=I32)[:, None] * MOE_ROWS
    blk_e = jnp.minimum(jnp.sum((pend[None, :] <= first_row).astype(I32), axis=1), ne - 1)
    xs = _dispatch(dest, x, rb, g, sh, sc, n_blk * MOE_ROWS)
    ys = _experts(blk_e, xs, layer, w_gu, b_gu, w_down, b_down)
    return _combine(dest, tw, x, rb, g2, ys)


def _nm_matmul_kernel(x_ref, g_ref, sh_ref, sc_ref, w_ref, o_ref, *, sig_from):
    h = _normmod(x_ref[...], g_ref[...], sh_ref[0], sc_ref[0])
    o = _dot(h.astype(BF16), w_ref[...])
    if sig_from is not None:
        col = lax.broadcasted_iota(I32, o.shape, 1)
        o = jnp.where(col >= sig_from, jax.nn.sigmoid(o), o)
    o_ref[...] = o


def _normmod_matmul(x, rb, g, sh, sc, w_bf16, sig_from=None):
    t, d = x.shape
    n = w_bf16.shape[1]
    tm = _row_tile(rb, 512)
    return pl.pallas_call(
        functools.partial(_nm_matmul_kernel, sig_from=sig_from),
        grid=(t // tm,),
        in_specs=[pl.BlockSpec((tm, d), lambda i: (i, 0)), pl.BlockSpec((1, d), lambda i: (0, 0)),
                  _mod_spec(sh, rb, tm), _mod_spec(sc, rb, tm), pl.BlockSpec((d, n), lambda i: (0, 0))],
        out_specs=pl.BlockSpec((tm, n), lambda i: (i, 0)),
        out_shape=jax.ShapeDtypeStruct((t, n), F32),
        compiler_params=_cparams("parallel"),
        name="normmod_matmul",
    )(x, g, sh, sc, w_bf16)


def _mm_residual_kernel(a_ref, x_ref, g1_ref, w_ref, o_ref):
    o_ref[...] = x_ref[...] + g1_ref[0] * _dot(a_ref[...].astype(BF16), w_ref[...])


def _matmul_residual(a, x, g1, w_bf16, rb):
    t, d = x.shape
    k = a.shape[1]
    tm = _row_tile(rb, 512)
    return pl.pallas_call(
        _mm_residual_kernel,
        grid=(t // tm,),
        in_specs=[pl.BlockSpec((tm, k), lambda i: (i, 0)), pl.BlockSpec((tm, d), lambda i: (i, 0)),
                  _mod_spec(g1, rb, tm), pl.BlockSpec((k, d), lambda i: (0, 0))],
        out_specs=pl.BlockSpec((tm, d), lambda i: (i, 0)),
        out_shape=jax.ShapeDtypeStruct((t, d), F32),
        compiler_params=_cparams("parallel"),
        name="matmul_residual",
    )(a, x, g1, w_bf16)


def _final_norm_kernel(x_ref, g_ref, o_ref):
    x = x_ref[...]
    o_ref[...] = x * lax.rsqrt(jnp.mean(x * x, axis=-1, keepdims=True) + RMS_EPS) * g_ref[...]


def _final_norm(x, g, rb):
    t, d = x.shape
    tm = _row_tile(rb, 512)
    return pl.pallas_call(
        _final_norm_kernel,
        grid=(t // tm,),
        in_specs=[pl.BlockSpec((tm, d), lambda i: (i, 0)), pl.BlockSpec((1, d), lambda i: (0, 0))],
        out_specs=pl.BlockSpec((tm, d), lambda i: (i, 0)),
        out_shape=jax.ShapeDtypeStruct((t, d), F32),
        compiler_params=_cparams("parallel"),
        name="final_norm",
    )(x, g.reshape(1, d))


def _rel_bucket_table():
    n = np.arange(REL_MAX_DIST + 1)
    max_exact = REL_BUCKETS // 2
    nf = np.maximum(n, 1).astype(np.float32)
    large = max_exact + (np.log(nf / max_exact) / math.log(REL_MAX_DIST / max_exact)
                         * (REL_BUCKETS - max_exact)).astype(np.int32)
    return np.where(n < max_exact, n, np.minimum(large, REL_BUCKETS - 1)).astype(np.int32)


def _bias_by_dist(rel_bias):
    return rel_bias.astype(F32)[_rel_bucket_table()].T


def _stack_heads(tab):
    h, r, c = tab.shape
    return tab.reshape(N_KV_HEADS, GQ * r, c)


def _dist_tile(rows, cols, offset, col_step=1):
    d = offset + np.arange(rows)[:, None] - col_step * np.arange(cols)[None, :]
    return np.clip(d, 0, REL_MAX_DIST)


CMP_NEAR = 32
CMP_PAD = 16


def _overlap_matrix(n_cmp, n_sel_pad):
    ci = np.arange(n_cmp)[:, None] * CMP_STRIDE
    sj = np.arange(n_sel_pad)[None, :] * SEL_BLOCK
    ov = ((ci < sj + SEL_BLOCK) & (ci + CMP_BLOCK > sj)).astype(np.float32)
    return np.pad(ov, ((CMP_PAD, CMP_PAD), (0, 0)))


def _compress_kernel(x_ref, pos_ref, w1_ref, b1_ref, w2_ref, o_ref, *, hid):
    w1 = w1_ref[0]
    parts = _dot(x_ref[0, 0].astype(BF16), w1)
    pt = _dot(pos_ref[0].astype(BF16), w1)
    p0 = parts[:, :hid] + pt[0:1, :hid]
    p1 = parts[:, hid:] + pt[1:2, hid:]
    n = p1.shape[0]
    pre = p0 + pltpu.roll(p1, n - 1, axis=0) + b1_ref[0]
    o_ref[0, 0] = _dot(jax.nn.gelu(pre).astype(BF16), w2_ref[0])


def _compress(xc, cmp_w1, cmp_b1, cmp_w2, cmp_pos):
    _, ng, nch, kdim = xc.shape
    hid = cmp_w1.shape[-1]
    hd = cmp_w2.shape[-1]
    w1 = cmp_w1.reshape(2, CMP_RATIO, kdim, hid)
    w1cat = jnp.concatenate([w1[:, r] for r in range(CMP_RATIO)], axis=-1).astype(BF16)
    pos = jnp.pad(cmp_pos.reshape(2, CMP_RATIO, kdim), ((0, 0), (0, SUBLANES - CMP_RATIO), (0, 0)))
    return pl.pallas_call(
        functools.partial(_compress_kernel, hid=hid),
        grid=(2, ng),
        in_specs=[pl.BlockSpec((1, 1, nch, kdim), lambda c, n: (c, n, 0, 0)),
                  pl.BlockSpec((1, SUBLANES, kdim), lambda c, n: (c, 0, 0)),
                  pl.BlockSpec((1, kdim, CMP_RATIO * hid), lambda c, n: (c, 0, 0)),
                  pl.BlockSpec((1, 1, hid), lambda c, n: (c, 0, 0)),
                  pl.BlockSpec((1, hid, hd), lambda c, n: (c, 0, 0))],
        out_specs=pl.BlockSpec((1, 1, nch, hd), lambda c, n: (c, n, 0, 0)),
        out_shape=jax.ShapeDtypeStruct((2, ng, nch, hd), F32),
        compiler_params=_cparams("parallel", "parallel"),
        name="nsa_compress",
    )(xc, pos, w1cat, cmp_b1.reshape(2, 1, hid), cmp_w2.astype(BF16))


def _dot3_nt(a, b):
    a_hi, a_lo = _split(a)
    b_hi, b_lo = _split(b)
    return _dot_nt(a_hi, b_hi) + (_dot_nt(a_lo, b_hi) + _dot_nt(a_hi, b_lo))


def _eye(n):
    r = lax.broadcasted_iota(I32, (n, n), 0)
    c = lax.broadcasted_iota(I32, (n, n), 1)
    return (r == c).astype(BF16)


def _transpose_bf16(x, eye):
    return _dot_nt(eye, x).astype(BF16)


def _group_q_transposed(q):
    full = _transpose_bf16(q, _eye(GQ * HEAD_DIM))
    return jnp.concatenate([full[g * HEAD_DIM:(g + 1) * HEAD_DIM, :] for g in range(GQ)], axis=1)


def _top_blocks_cols(score, ntop):
    row_f = lax.broadcasted_iota(I32, score.shape, 0).astype(F32)
    sel = jnp.zeros(score.shape, F32)
    for _ in range(ntop):
        m = jnp.max(score, axis=0, keepdims=True)
        idx = jnp.min(jnp.where(score == m, row_f, float(score.shape[0])), axis=0, keepdims=True)
        hit = row_f == idx
        sel = jnp.where(jnp.logical_and(hit, m > -jnp.inf), 1.0, sel)
        score = jnp.where(hit, -jnp.inf, score)
    return sel


def _cmp_attn_kernel(q_ref, kc_ref, vc_ref, vct_ref, ov_ref, ovt_ref, bnear_ref, bfar_ref, oct_ref, selt_ref,
                     *, nch, ntop, qblk):
    qb = pl.program_id(2)
    cols = GQ * qblk
    scale = HEAD_DIM ** -0.5
    c = pl.multiple_of(qb * (qblk // CMP_STRIDE), SUBLANES)
    eye = _eye(HEAD_DIM)
    q_hi, q_lo = _split(q_ref[...] * scale)
    qt_hi = _group_q_transposed(q_hi)
    qt_lo = _group_q_transposed(q_lo)

    def logits(k):
        k_hi, k_lo = _split(k)
        return _dot(k_hi, qt_hi) + (_dot(k_lo, qt_hi) + _dot(k_hi, qt_lo))

    s_far = logits(kc_ref[0, 0, CMP_PAD:CMP_PAD + nch, :]) + bfar_ref[0]
    s_near = logits(kc_ref[0, 0, pl.ds(c, CMP_NEAR), :]) + bnear_ref[0]
    mask_far = lax.broadcasted_iota(I32, (nch, cols), 0) < c - CMP_PAD
    rel = lax.broadcasted_iota(I32, (CMP_NEAR, cols), 0)
    i_io = lax.broadcasted_iota(I32, (CMP_NEAR, cols), 1) & (qblk - 1)
    mask_near = jnp.logical_and(CMP_STRIDE * (rel - CMP_PAD) + CMP_BLOCK - 1 <= i_io, rel + c - CMP_PAD >= 0)
    sf = jnp.where(mask_far, s_far, NEG)
    sn = jnp.where(mask_near, s_near, NEG)
    m = jnp.maximum(jnp.max(sf, axis=0, keepdims=True), jnp.max(sn, axis=0, keepdims=True))
    pf = jnp.where(mask_far, jnp.exp(sf - m), 0.0)
    pn = jnp.where(mask_near, jnp.exp(sn - m), 0.0)
    l = jnp.sum(pf, axis=0, keepdims=True) + jnp.sum(pn, axis=0, keepdims=True)
    linv = 1.0 / jnp.where(l > 0, l, 1.0)
    vnt = _transpose_bf16(vc_ref[0, 0, pl.ds(c, CMP_NEAR), :].astype(BF16), eye)
    o = _dot(vct_ref[0, 0].astype(BF16), pf.astype(BF16)) + _dot(vnt, pn.astype(BF16))
    oct_ref[0, 0, 0] = o * linv
    pfn = pf * linv
    pnn = pn * linv
    pgf = pfn[:, 0:qblk]
    pgn = pnn[:, 0:qblk]
    for g in range(1, GQ):
        pgf = pgf + pfn[:, g * qblk:(g + 1) * qblk]
        pgn = pgn + pnn[:, g * qblk:(g + 1) * qblk]
    f_hi, f_lo = _split(pgf)
    n_hi, n_lo = _split(pgn)
    nsp = ovt_ref.shape[0]
    ovt = ovt_ref[...]
    ovnt = _transpose_bf16(ov_ref[pl.ds(c, CMP_NEAR), :], _eye(nsp))
    imp = (_dot(ovt, f_hi) + _dot(ovnt, n_hi)) + (_dot(ovt, f_lo) + _dot(ovnt, n_lo))
    s_io = lax.broadcasted_iota(I32, (nsp, qblk), 0)
    t = qb * qblk + lax.broadcasted_iota(I32, (nsp, qblk), 1)
    cur = lax.shift_right_logical(t, int(math.log2(SEL_BLOCK)))
    forced = jnp.logical_or(s_io == 0, jnp.logical_or(s_io == cur, s_io == cur - 1))
    score = jnp.where(s_io * SEL_BLOCK <= t, imp + jnp.where(forced, FORCE_SCORE, 0.0), -jnp.inf)
    selt_ref[0, 0, 0] = jnp.where(_top_blocks_cols(score, ntop) > 0.5, 0.0, NEG)


def _cmp_attention(q2d, b, t, kc_pad, vc_pad, vct, ov, ovt, bnear, bfar, nch, ntop):
    hd = HEAD_DIM
    qblk = math.gcd(t, Q_BLOCK)
    nqb = t // qblk
    nsp = ov.shape[1]
    cols = GQ * qblk
    kern = functools.partial(_cmp_attn_kernel, nch=nch, ntop=ntop, qblk=qblk)
    kvspec = pl.BlockSpec((1, 1, kc_pad.shape[2], hd), lambda bb, k, i: (bb, k, 0, 0))
    return pl.pallas_call(
        kern,
        grid=(b, N_KV_HEADS, nqb),
        in_specs=[pl.BlockSpec((qblk, GQ * hd), lambda bb, k, i: (bb * nqb + i, k)), kvspec, kvspec,
                  pl.BlockSpec((1, 1, hd, nch), lambda bb, k, i: (bb, k, 0, 0)),
                  pl.BlockSpec(ov.shape, lambda bb, k, i: (0, 0)), pl.BlockSpec(ovt.shape, lambda bb, k, i: (0, 0)),
                  pl.BlockSpec((1,) + bnear.shape[1:], lambda bb, k, i: (k, 0, 0)),
                  pl.BlockSpec((1,) + bfar.shape[1:], lambda bb, k, i: (k, 0, 0))],
        out_specs=[pl.BlockSpec((1, 1, 1, hd, cols), lambda bb, k, i: (bb, k, i, 0, 0)),
                   pl.BlockSpec((1, 1, 1, nsp, qblk), lambda bb, k, i: (bb, k, i, 0, 0))],
        out_shape=[jax.ShapeDtypeStruct((b, N_KV_HEADS, nqb, hd, cols), F32),
                   jax.ShapeDtypeStruct((b, N_KV_HEADS, nqb, nsp, qblk), F32)],
        compiler_params=_cparams("parallel", "parallel", "parallel"),
        name="nsa_cmp_attention",
    )(q2d, kc_pad, vc_pad, vct, ov, ovt, bnear, bfar)


def _slc_win_kernel(q_ref, selt_ref, oct_ref, gate_ref, sk_ref, svt_ref, wk_ref, wvt_ref, bt_ref, bfar_ref, o_ref,
                    m_s, l_s, acc_s, s_a, s_b, *, qblk):
    qb = pl.program_id(2)
    cols = GQ * qblk
    scale = HEAD_DIM ** -0.5
    qt = _group_q_transposed((q_ref[...] * scale).astype(BF16))
    bfar = bfar_ref[0]
    per = qblk // SEL_BLOCK

    def neg_unless(cond):
        return jnp.where(cond, 0.0, NEG)

    def sel_add(kt, extra):
        pieces = [jnp.broadcast_to(selt_ref[0, 0, 0, pl.ds(per * kt + e, 1), :] + extra, (SEL_BLOCK, qblk))
                  for e in range(per)]
        mk = jnp.concatenate(pieces, axis=0)
        return jnp.concatenate([mk] * GQ, axis=1)

    def reset():
        m_s[...] = jnp.full(m_s.shape, NEG, F32)
        l_s[...] = jnp.zeros(l_s.shape, F32)
        acc_s[...] = jnp.zeros(acc_s.shape, F32)

    def update(pieces, vts):
        update_cat(jnp.concatenate(pieces, axis=0), jnp.concatenate(vts, axis=1))

    def update_cat(s, vt):
        m_prev = m_s[...]
        m_new = jnp.maximum(m_prev, jnp.max(s, axis=0, keepdims=True))
        alpha = jnp.exp(m_prev - m_new)
        p = jnp.exp(s - m_new)
        l_s[...] = alpha * l_s[...] + jnp.sum(p, axis=0, keepdims=True)
        acc_s[...] = alpha * acc_s[...] + _dot(vt, p.astype(BF16))
        m_s[...] = m_new

    def result():
        return acc_s[...] / l_s[...]

    def logits(k_ref, kt):
        return _dot(k_ref[0, 0, kt], qt)

    reset()
    n_far = jnp.maximum(qb - 1, 0)
    last = sk_ref.shape[2] - 1

    n_groups = (n_far + FAR_GROUP - 1) // FAR_GROUP

    def stage_logits(g, buf):
        for e in range(FAR_GROUP):
            kt = g * FAR_GROUP + e
            ktc = jnp.minimum(kt, last)
            buf[e * qblk:(e + 1) * qblk, :] = logits(sk_ref, ktc) + (sel_add(ktc, neg_unless(kt < n_far)) + bfar)

    def consume(g, buf):
        vts = [svt_ref[0, 0, jnp.minimum(g * FAR_GROUP + e, last)] for e in range(FAR_GROUP)]
        update_cat(buf[...], jnp.concatenate(vts, axis=1))

    stage_logits(0, s_a)

    def far(j, carry):
        stage_logits(2 * j + 1, s_b)
        consume(2 * j, s_a)
        stage_logits(2 * j + 2, s_a)
        consume(2 * j + 1, s_b)
        return carry

    lax.fori_loop(0, (n_groups + 1) // 2, far, 0)
    kt1 = jnp.maximum(qb - 1, 0)
    update([logits(sk_ref, kt1) + (sel_add(kt1, neg_unless(qb >= 1)) + bt_ref[0, 1]),
            logits(sk_ref, qb) + (sel_add(qb, 0.0) + bt_ref[0, 0])],
           [svt_ref[0, 0, kt1], svt_ref[0, 0, qb]])
    o_s = result()

    reset()
    nwin = WINDOW // qblk
    pieces, vts = [], []
    for u in range(nwin + 1):
        kt = qb - nwin + u
        ktc = jnp.maximum(kt, 0)
        delta = (nwin - u) * qblk
        if delta == 0:
            bias = bt_ref[0, 0]
        elif delta == qblk:
            bias = bt_ref[0, 1] + neg_unless(kt >= 0)
        elif u == 0:
            bias = bt_ref[0, 2] + neg_unless(kt >= 0)
        else:
            bias = bfar + neg_unless(kt >= 0)
        pieces.append(logits(wk_ref, ktc) + bias)
        vts.append(wvt_ref[0, 0, ktc])
    update(pieces, vts)
    o_w = result()

    gt = gate_ref[0, 0, 0]
    ot = gt[0:1] * oct_ref[0, 0, 0] + gt[1:2] * o_s + gt[2:3] * o_w
    o_ref[0, 0, 0] = ot.astype(BF16)


def _slc_win_attention(q2d, b, t, selt, oct, gates_blk, sk, svt, wk, wvt, bt, bfar):
    hd = HEAD_DIM
    qblk = math.gcd(t, Q_BLOCK)
    nqb = t // qblk
    nsp = selt.shape[-2]
    cols = GQ * qblk
    kspec = pl.BlockSpec((1, 1, nqb, qblk, hd), lambda bb, k, i: (bb, k, 0, 0, 0))
    vspec = pl.BlockSpec((1, 1, nqb, hd, qblk), lambda bb, k, i: (bb, k, 0, 0, 0))
    qspec = pl.BlockSpec((qblk, GQ * hd), lambda bb, k, i: (bb * nqb + i, k))
    return pl.pallas_call(
        functools.partial(_slc_win_kernel, qblk=qblk),
        grid=(b, N_KV_HEADS, nqb),
        in_specs=[qspec, pl.BlockSpec((1, 1, 1, nsp, qblk), lambda bb, k, i: (bb, k, i, 0, 0)),
                  pl.BlockSpec((1, 1, 1, hd, cols), lambda bb, k, i: (bb, k, i, 0, 0)),
                  pl.BlockSpec((1, 1, 1, 3, cols), lambda bb, k, i: (bb, k, i, 0, 0)),
                  kspec, vspec, kspec, vspec,
                  pl.BlockSpec((1,) + bt.shape[1:], lambda bb, k, i: (k, 0, 0, 0)),
                  pl.BlockSpec((1,) + bfar.shape[1:], lambda bb, k, i: (k, 0, 0))],
        out_specs=pl.BlockSpec((1, 1, 1, hd, cols), lambda bb, k, i: (bb, k, i, 0, 0)),
        out_shape=jax.ShapeDtypeStruct((b, N_KV_HEADS, nqb, hd, cols), BF16),
        scratch_shapes=[pltpu.VMEM((1, cols), F32), pltpu.VMEM((1, cols), F32), pltpu.VMEM((hd, cols), F32),
                        pltpu.VMEM((FAR_GROUP * qblk, cols), F32), pltpu.VMEM((FAR_GROUP * qblk, cols), F32)],
        compiler_params=_cparams("parallel", "parallel", "parallel"),
        name="nsa_slc_win_attention",
    )(q2d, selt, oct, gates_blk, sk, svt, wk, wvt, bt, bfar)


def _cols_table(tab):
    h, nq, nk = tab.shape
    return jnp.transpose(tab.reshape(N_KV_HEADS, GQ, nq, nk), (0, 3, 1, 2)).reshape(N_KV_HEADS, nk, GQ * nq)


def _nsa_prompt(proj, kv, rel_bias, cmp_w1, cmp_b1, cmp_w2, cmp_pos):
    b, t = kv.shape[:2]
    hq = N_HEADS * HEAD_DIM
    gates = proj[:, hq:hq + 3 * N_HEADS]
    qblk = math.gcd(t, Q_BLOCK)
    assert qblk == Q_BLOCK and t % Q_BLOCK == 0 and WINDOW % Q_BLOCK == 0 and WINDOW - qblk + 1 >= REL_MAX_DIST
    nqb = t // qblk
    nch = t // CMP_STRIDE
    n_sel = t // SEL_BLOCK
    nsp = -(-n_sel // LANES) * LANES
    kv_hm = jnp.transpose(kv, (2, 0, 3, 1, 4))
    xc = kv_hm[:2].reshape(2, b * N_KV_HEADS, nch, CMP_STRIDE * HEAD_DIM)
    kvc = _compress(xc, cmp_w1, cmp_b1, cmp_w2, cmp_pos).reshape(2, b, N_KV_HEADS, nch, HEAD_DIM)
    vct = jnp.swapaxes(kvc[1], -1, -2)
    kvc = jnp.pad(kvc, ((0, 0), (0, 0), (0, 0), (CMP_PAD, CMP_PAD), (0, 0)))
    tbl = _bias_by_dist(rel_bias)
    bfar = _cols_table(jnp.broadcast_to(tbl[:, REL_MAX_DIST][:, None, None], (N_HEADS, qblk, 1)))
    bnear = _cols_table(tbl[:, _dist_tile(qblk, CMP_NEAR, CMP_STRIDE * CMP_PAD - CMP_BLOCK + 1, CMP_STRIDE)])
    qi = np.arange(qblk)[:, None]
    kj = np.arange(qblk)[None, :]
    far_tile = jnp.broadcast_to(tbl[:, REL_MAX_DIST][:, None, None], (N_HEADS, qblk, qblk))
    bt = jnp.stack([_cols_table(jnp.where(kj <= qi, tbl[:, _dist_tile(qblk, qblk, 0)], NEG)),
                    _cols_table(tbl[:, _dist_tile(qblk, qblk, qblk)]),
                    _cols_table(jnp.where(kj >= qi, far_tile, NEG))], axis=1)
    ov_np = _overlap_matrix(nch, nsp)
    ov = jnp.asarray(ov_np, BF16)
    ovt = jnp.asarray(ov_np[CMP_PAD:CMP_PAD + nch].T, BF16)
    gates_blk = jnp.transpose(gates.reshape(b, nqb, qblk, N_KV_HEADS, GQ, 3), (0, 3, 1, 5, 4, 2))
    gates_blk = gates_blk.reshape(b, N_KV_HEADS, nqb, 3, GQ * qblk)
    oct, selt = _cmp_attention(proj, b, t, kvc[0], kvc[1], vct, ov, ovt, bnear, bfar, nch, min(SEL_TOP, n_sel))
    kvb = kv_hm[2:].astype(BF16).reshape(4, b, N_KV_HEADS, nqb, qblk, HEAD_DIM)
    sk, wk = kvb[0], kvb[2]
    svt, wvt = jnp.swapaxes(kvb[1], -1, -2), jnp.swapaxes(kvb[3], -1, -2)
    o_t = _slc_win_attention(proj, b, t, selt, oct, gates_blk, sk, svt, wk, wvt, bt, bfar)
    o_t = o_t.reshape(b, N_KV_HEADS, nqb, HEAD_DIM, GQ, qblk)
    return jnp.transpose(o_t, (0, 2, 5, 1, 4, 3)).reshape(b, t, N_HEADS * HEAD_DIM)


CMP_TAP_PAIRS = CMP_STRIDE // 2
PAGE_CHUNKS = PAGE_SIZE // CMP_STRIDE


def _paged_compress_kernel(pt_ref, cache_ref, wp_ref, w1_ref, pos_ref, b1_ref, w2_ref, o_ref, buft, buf, carry, sem,
                           *, pg, npages, hid):
    b, g = pl.program_id(0), pl.program_id(1)
    m = pg * PAGE_CHUNKS
    hk = N_KV_HEADS * HEAD_DIM
    pairs = hk // LANES

    def page_copy(p):
        page = pt_ref[b * npages + g * pg + p]
        return pltpu.make_async_copy(cache_ref.at[page, pl.ds(0, 2)], buft.at[p], sem)

    for p in range(pg):
        page_copy(p).start()

    @pl.when(g == 0)
    def _():
        carry[...] = jnp.zeros_like(carry)

    for p in range(pg):
        page_copy(p).wait()

    def to_rows(p, carry_):
        r0 = pl.multiple_of(p * PAGE_SIZE, PAGE_SIZE)
        for comp in range(2):
            for hp in range(pairs):
                plane = buft[p, comp, hp * LANES:(hp + 1) * LANES, :]
                buf[comp * pairs + hp, pl.ds(r0, PAGE_SIZE), :] = plane.T
        return carry_

    lax.fori_loop(0, pg, to_rows, 0)

    row0 = lax.broadcasted_iota(I32, (m, hid), 0) == 0
    for comp in range(2):
        pt = _dot(pos_ref[comp].astype(BF16), w1_ref[comp])
        for hp in range(N_KV_HEADS // 2):
            lg = (comp * hk + hp * 2 * HEAD_DIM) // LANES
            acc = jnp.zeros((m, 2 * CMP_RATIO * hid), F32)
            for u in range(CMP_TAP_PAIRS):
                taps = [buf[lg, pl.ds(2 * u + e, m, stride=CMP_STRIDE), :] for e in range(2)]
                acc = acc + _dot(jnp.concatenate(taps, axis=1).astype(BF16), wp_ref[comp, u])
            for e in range(2):
                kvh = 2 * hp + e
                base = e * CMP_RATIO * hid
                p0 = acc[:, base:base + hid] + pt[0:1, :hid]
                p1 = acc[:, base + hid:base + 2 * hid] + pt[1:2, hid:]
                prev0 = jnp.where(row0, carry[comp, kvh], pltpu.roll(p0, 1, axis=0))
                carry[comp, kvh] = p0[m - 1:m, :]
                pre = prev0 + p1 + b1_ref[comp]
                o_ref[comp, 0, kvh] = _dot(jax.nn.gelu(pre).astype(BF16), w2_ref[comp])


def _paged_compress(page_table, cache_t, cmp_w1, cmp_b1, cmp_w2, cmp_pos):
    bs, npages = page_table.shape
    hid = cmp_w1.shape[-1]
    kdim = CMP_STRIDE * HEAD_DIM
    pg = math.gcd(npages, 32)
    m = pg * PAGE_CHUNKS
    w1 = cmp_w1.reshape(2, CMP_RATIO, kdim, hid)
    w1cat = jnp.concatenate([w1[:, r] for r in range(CMP_RATIO)], axis=-1)
    n1 = CMP_RATIO * hid
    wt = w1cat.reshape(2, CMP_TAP_PAIRS, 2, HEAD_DIM, n1)
    zero = jnp.zeros_like(wt)
    wp = jnp.concatenate([jnp.concatenate([wt, zero], axis=-1), jnp.concatenate([zero, wt], axis=-1)], axis=3)
    wp = wp.reshape(2, CMP_TAP_PAIRS, 4 * HEAD_DIM, 2 * n1).astype(BF16)
    pos = jnp.pad(cmp_pos.reshape(2, CMP_RATIO, kdim), ((0, 0), (0, SUBLANES - CMP_RATIO), (0, 0)))
    nch = npages * PAGE_CHUNKS
    grid_spec = pltpu.PrefetchScalarGridSpec(
        num_scalar_prefetch=1,
        grid=(bs, npages // pg),
        in_specs=[pl.BlockSpec(memory_space=pl.ANY),
                  pl.BlockSpec(wp.shape, lambda b, g, pt: (0, 0, 0, 0)),
                  pl.BlockSpec((2, kdim, n1), lambda b, g, pt: (0, 0, 0)),
                  pl.BlockSpec((2, SUBLANES, kdim), lambda b, g, pt: (0, 0, 0)),
                  pl.BlockSpec((2, 1, hid), lambda b, g, pt: (0, 0, 0)),
                  pl.BlockSpec((2, hid, HEAD_DIM), lambda b, g, pt: (0, 0, 0))],
        out_specs=pl.BlockSpec((2, 1, N_KV_HEADS, m, HEAD_DIM), lambda b, g, pt: (0, b, 0, g, 0)),
        scratch_shapes=[pltpu.VMEM((pg, 2, N_KV_HEADS * HEAD_DIM, PAGE_SIZE), F32),
                        pltpu.VMEM((2 * N_KV_HEADS * HEAD_DIM // LANES, pg * PAGE_SIZE, LANES), F32),
                        pltpu.VMEM((2, N_KV_HEADS, 1, hid), F32), pltpu.SemaphoreType.DMA(())],
    )
    return pl.pallas_call(
        functools.partial(_paged_compress_kernel, pg=pg, npages=npages, hid=hid),
        grid_spec=grid_spec,
        out_shape=jax.ShapeDtypeStruct((2, bs, N_KV_HEADS, nch, HEAD_DIM), F32),
        compiler_params=_cparams("parallel", "arbitrary"),
        name="nsa_paged_compress",
    )(page_table.reshape(-1), cache_t, wp, w1cat.astype(BF16), pos, cmp_b1.reshape(2, 1, hid), cmp_w2.astype(BF16))


def _rows8(x):
    return jnp.concatenate([x, jnp.zeros((SUBLANES - x.shape[0], x.shape[1]), x.dtype)], axis=0)


def _sample_cmp_kernel(q_ref, kc_ref, vc_ref, ov_ref, bias_ref, oc_ref, idx_ref, *, nch, n_sel, n_past_blk, ntop):
    scale = HEAD_DIM ** -0.5
    nsp = ov_ref.shape[1]
    row = lax.broadcasted_iota(I32, (SUBLANES, nch), 0)
    lane = lax.broadcasted_iota(I32, (SUBLANES, nch), 1)
    mask = jnp.logical_and(lane >= 1, row < GQ)
    imp = jnp.zeros((SUBLANES, nsp), F32)
    row_s = lax.broadcasted_iota(I32, (SUBLANES, nsp), 0)
    for kvh in range(N_KV_HEADS):
        q8 = _rows8(q_ref[0, kvh * GQ:(kvh + 1) * GQ, :])
        s = jnp.where(mask, _dot3_nt(q8, kc_ref[0, kvh]) * scale + bias_ref[kvh], NEG)
        mx = jnp.max(s, axis=-1, keepdims=True)
        p = jnp.where(mask, jnp.exp(s - mx), 0.0)
        l = jnp.sum(p, axis=-1, keepdims=True)
        linv = 1.0 / jnp.where(l > 0, l, 1.0)
        o = _dot(p.astype(BF16), vc_ref[0, kvh].astype(BF16)) * linv
        oc_ref[0, kvh * GQ:(kvh + 1) * GQ, :] = o[0:GQ]
        p_hi, p_lo = _split(p * linv)
        contrib = _dot(p_hi, ov_ref[...]) + _dot(p_lo, ov_ref[...])
        imp = jnp.where(row_s == kvh, jnp.sum(contrib, axis=0, keepdims=True), imp)
    s_io = lax.broadcasted_iota(I32, (SUBLANES, nsp), 1)
    t = n_past_blk * SEL_BLOCK
    cur = t // SEL_BLOCK
    forced = jnp.logical_or(s_io == 0, jnp.logical_or(s_io == cur, s_io == cur - 1))
    valid = jnp.logical_and(s_io * SEL_BLOCK <= t, s_io < n_sel)
    score = jnp.where(valid, imp + jnp.where(forced, FORCE_SCORE, 0.0), -jnp.inf)
    lane_f = s_io.astype(F32)
    out_lane = lax.broadcasted_iota(I32, (SUBLANES, LANES), 1)
    picks = jnp.zeros((SUBLANES, LANES), F32)
    for it in range(ntop):
        mx = jnp.max(score, axis=-1, keepdims=True)
        idx = jnp.min(jnp.where(score == mx, lane_f, float(nsp)), axis=-1, keepdims=True)
        picks = jnp.where(out_lane == it, idx, picks)
        score = jnp.where(lane_f == idx, -jnp.inf, score)
    idx_ref[0] = picks.astype(I32)


def _sample_cmp_attention(q_hm, kc, vc, ov, bias, n_sel, n_past_blk, ntop):
    bs = q_hm.shape[0]
    nch = kc.shape[2]
    kvspec = pl.BlockSpec((1, N_KV_HEADS, nch, HEAD_DIM), lambda b: (b, 0, 0, 0))
    return pl.pallas_call(
        functools.partial(_sample_cmp_kernel, nch=nch, n_sel=n_sel, n_past_blk=n_past_blk, ntop=ntop),
        grid=(bs,),
        in_specs=[pl.BlockSpec((1, N_HEADS, HEAD_DIM), lambda b: (b, 0, 0)), kvspec, kvspec,
                  pl.BlockSpec(ov.shape, lambda b: (0, 0)), pl.BlockSpec(bias.shape, lambda b: (0, 0, 0))],
        out_specs=[pl.BlockSpec((1, N_HEADS, HEAD_DIM), lambda b: (b, 0, 0)),
                   pl.BlockSpec((1, SUBLANES, LANES), lambda b: (b, 0, 0))],
        out_shape=[jax.ShapeDtypeStruct(q_hm.shape, F32), jax.ShapeDtypeStruct((bs, SUBLANES, LANES), I32)],
        compiler_params=_cparams("parallel"),
        name="nsa_sample_cmp_attention",
    )(q_hm, kc, vc, ov, bias)


def _sample_slc_win_kernel(page_ref, half_ref, new_ref, d0_ref, var_ref,
                           q_ref, oc_ref, gate_ref, cache_ref, newt_ref, wint_ref, winnew_ref,
                           bslc_ref, bwin_ref, bwin_new_ref, o_ref, buf, sem, *, ntop):
    b = pl.program_id(0)
    scale = HEAD_DIM ** -0.5
    hk = N_KV_HEADS * HEAD_DIM
    nslots = N_KV_HEADS * ntop

    def plane_copies(slot):
        i = b * nslots + slot
        kvh = slot // ntop
        return [pltpu.make_async_copy(cache_ref.at[page_ref[i], 2 + e, kvh], buf.at[slot, e], sem) for e in range(2)]

    for slot in range(nslots):
        is_new = new_ref[b * nslots + slot] == 1

        @pl.when(jnp.logical_not(is_new))
        def _():
            for cp in plane_copies(slot):
                cp.start()

        @pl.when(is_new)
        def _():
            buf[slot] = newt_ref[0, :, slot // ntop]

    for slot in range(nslots):
        @pl.when(new_ref[b * nslots + slot] == 0)
        def _():
            for cp in plane_copies(slot):
                cp.wait()

    lane = lax.broadcasted_iota(I32, (SUBLANES, PAGE_SIZE), 1)
    gt = gate_ref[0]
    for kvh in range(N_KV_HEADS):
        hs = slice(kvh * GQ, (kvh + 1) * GQ)
        ks = slice(kvh * HEAD_DIM, (kvh + 1) * HEAD_DIM)
        vsl = slice(hk + kvh * HEAD_DIM, hk + (kvh + 1) * HEAD_DIM)
        q8f = _rows8(q_ref[0, hs, :])
        q8 = q8f.astype(BF16)
        ss, masks = [], []
        for j in range(ntop):
            slot = kvh * ntop + j
            i = b * nslots + slot
            s = _dot(q8, buf[slot, 0].astype(BF16)) * scale + bslc_ref[kvh, var_ref[i]]
            off = lane - half_ref[i] * SEL_BLOCK
            mask = jnp.logical_and(jnp.logical_and(off >= 0, off < SEL_BLOCK), off <= d0_ref[i])
            ss.append(jnp.where(mask, s, NEG))
            masks.append(mask)
        mx = ss[0].max(axis=-1, keepdims=True)
        for s in ss[1:]:
            mx = jnp.maximum(mx, s.max(axis=-1, keepdims=True))
        l = jnp.zeros((SUBLANES, 1), F32)
        acc = jnp.zeros((SUBLANES, HEAD_DIM), F32)
        for j in range(ntop):
            p = jnp.where(masks[j], jnp.exp(ss[j] - mx), 0.0)
            l = l + jnp.sum(p, axis=-1, keepdims=True)
            acc = acc + _dot_nt(p.astype(BF16), buf[kvh * ntop + j, 1].astype(BF16))
        o_s = acc / jnp.where(l > 0, l, 1.0)
        sw = _dot(q8, wint_ref[0, 0, kvh].astype(BF16)) * scale + bwin_ref[kvh]
        s_new = jnp.sum(q8f * winnew_ref[0, 0:1, ks], axis=-1, keepdims=True) * scale + bwin_new_ref[kvh]
        mw = jnp.maximum(sw.max(axis=-1, keepdims=True), s_new)
        pw = jnp.exp(sw - mw)
        p_new = jnp.exp(s_new - mw)
        lw = jnp.sum(pw, axis=-1, keepdims=True) + p_new
        o_w = (_dot_nt(pw.astype(BF16), wint_ref[0, 1, kvh].astype(BF16)) + p_new * winnew_ref[0, 0:1, vsl]) / lw
        g3 = gt[hs, :]
        o_ref[0, hs, :] = g3[:, 0:1] * oc_ref[0, hs, :] + g3[:, 1:2] * o_s[0:GQ] + g3[:, 2:3] * o_w[0:GQ]


def _sample_slc_win(picks, q_hm, oc, gates_h, cache_t, newt, wint, winnew, bslc, bwin, bwin_new, ntop):
    bs = q_hm.shape[0]
    hspec = pl.BlockSpec((1, N_HEADS, HEAD_DIM), lambda b, *_: (b, 0, 0))
    grid_spec = pltpu.PrefetchScalarGridSpec(
        num_scalar_prefetch=5,
        grid=(bs,),
        in_specs=[hspec, hspec, pl.BlockSpec((1, N_HEADS, 3), lambda b, *_: (b, 0, 0)),
                  pl.BlockSpec(memory_space=pl.ANY),
                  pl.BlockSpec((1,) + newt.shape[1:], lambda b, *_: (b, 0, 0, 0, 0)),
                  pl.BlockSpec((1,) + wint.shape[1:], lambda b, *_: (b, 0, 0, 0, 0)),
                  pl.BlockSpec((1,) + winnew.shape[1:], lambda b, *_: (b, 0, 0)),
                  pl.BlockSpec(bslc.shape, lambda b, *_: (0, 0, 0, 0)),
                  pl.BlockSpec(bwin.shape, lambda b, *_: (0, 0, 0)),
                  pl.BlockSpec(bwin_new.shape, lambda b, *_: (0, 0, 0))],
        out_specs=hspec,
        scratch_shapes=[pltpu.VMEM((N_KV_HEADS * ntop, 2, HEAD_DIM, PAGE_SIZE), F32), pltpu.SemaphoreType.DMA(())],
    )
    return pl.pallas_call(
        functools.partial(_sample_slc_win_kernel, ntop=ntop),
        grid_spec=grid_spec,
        out_shape=jax.ShapeDtypeStruct(q_hm.shape, F32),
        compiler_params=_cparams("arbitrary"),
        name="nsa_sample_slc_win",
    )(*picks, q_hm, oc, gates_h, cache_t, newt, wint, winnew, bslc, bwin, bwin_new)


def _nsa_sample(q, gates, kv_new, win_buf, cache, page_table, rel_bias, cmp_w1, cmp_b1, cmp_w2, cmp_pos):
    bs, npages = page_table.shape
    pool = cache.shape[0]
    hk = N_KV_HEADS * HEAD_DIM
    past = npages * PAGE_SIZE
    nch = past // CMP_STRIDE
    n_past_blk = past // SEL_BLOCK
    n_sel = n_past_blk + 1
    ntop = min(SEL_TOP, n_sel)
    nsp = -(-n_sel // LANES) * LANES
    wb = win_buf.shape[1]
    assert n_sel >= SEL_TOP and past >= wb
    cache_t = jnp.transpose(cache, (0, 2, 3, 4, 1))
    kvc = _paged_compress(page_table, cache_t.reshape(pool, 4, hk, PAGE_SIZE), cmp_w1, cmp_b1, cmp_w2, cmp_pos)
    tbl = _bias_by_dist(rel_bias)
    tok = np.arange(nch) - 1
    dist_c = np.clip(past - (tok * CMP_STRIDE + CMP_BLOCK - 1), 0, REL_MAX_DIST)
    bias_c = jnp.pad(tbl[:, dist_c].reshape(N_KV_HEADS, GQ, nch), ((0, 0), (0, SUBLANES - GQ), (0, 0)))
    ov = np.zeros((nch, nsp), np.float32)
    ov[1:] = _overlap_matrix(nch - 1, nsp)[CMP_PAD:-CMP_PAD]
    q_hm = q.reshape(bs, N_HEADS, HEAD_DIM)
    oc, idx = _sample_cmp_attention(q_hm, kvc[0], kvc[1], jnp.asarray(ov, BF16), bias_c, n_sel, n_past_blk, ntop)
    idx = idx[:, :N_KV_HEADS, :ntop]
    ip = jnp.minimum(idx, n_past_blk - 1)
    per_page = PAGE_SIZE // SEL_BLOCK
    page = jnp.take_along_axis(page_table, (ip // per_page).reshape(bs, -1), axis=1)
    is_new = (idx >= n_past_blk).astype(I32)
    half = jnp.where(is_new == 1, 0, ip % per_page)
    d0 = (n_past_blk - idx) * SEL_BLOCK
    case = jnp.clip(n_past_blk - idx, 0, 3)
    picks = [a.reshape(-1).astype(I32) for a in (page, half, is_new, d0, case * per_page + half)]
    lanes = np.arange(PAGE_SIZE)
    dist_s = np.stack([np.clip(cs * SEL_BLOCK - (lanes - hf * SEL_BLOCK), 0, REL_MAX_DIST)
                       for cs in range(4) for hf in range(per_page)])
    nvar = dist_s.shape[0]
    bslc = jnp.pad(jnp.transpose(tbl[:, dist_s].reshape(N_KV_HEADS, GQ, nvar, PAGE_SIZE), (0, 2, 1, 3)),
                   ((0, 0), (0, 0), (0, SUBLANES - GQ), (0, 0)))
    dist_w = np.clip(wb - np.arange(wb), 0, REL_MAX_DIST)
    bwin = jnp.pad(tbl[:, dist_w].reshape(N_KV_HEADS, GQ, wb), ((0, 0), (0, SUBLANES - GQ), (0, 0)))
    bwin_new = jnp.pad(tbl[:, 0].reshape(N_KV_HEADS, GQ, 1), ((0, 0), (0, SUBLANES - GQ), (0, 0)))
    newt = jnp.pad(kv_new[:, 2:4][..., None], ((0, 0),) * 4 + ((0, PAGE_SIZE - 1),))
    winnew = jnp.pad(kv_new[:, 4:6].reshape(bs, 1, 2 * hk), ((0, 0), (0, SUBLANES - 1), (0, 0)))
    wint = jnp.transpose(win_buf, (0, 2, 3, 4, 1))
    o = _sample_slc_win(picks, q_hm, oc, gates.reshape(bs, N_HEADS, 3), cache_t, newt, wint, winnew,
                        bslc, bwin, bwin_new, ntop)
    return o.reshape(bs, N_HEADS * HEAD_DIM)


def _mods(mod_l, nb, mr):
    return [m.reshape(nb, mr, -1) for m in jnp.split(mod_l, 6, axis=-1)]


def _s5_layer(x, nb, rb, seq, norm_g, mods, s5w, dsk, w_glu_bf16, h0r, h0i):
    sh1, sc1, g1 = mods[0], mods[1], mods[2]
    d = x.shape[-1]
    ar, ai, wbr, wbi, wcr, wci = s5w
    act, st_r, st_i = _s5_scan(x.reshape(nb, rb, d), norm_g.reshape(1, d), sh1, sc1, dsk.reshape(1, d),
                               wbr, wbi, ar, ai, wcr, wci, h0r, h0i, seq=seq)
    return _glu_residual(act.reshape(-1, d), x, g1, w_glu_bf16, rb), st_r, st_i


def kernel(x_prompt, x_sample, c_prompt, c_sample, state_s5, cache_nsa_kv, cache_win_kv, page_table, w_mod, b_mod, norm_g, s5_lambda_re, s5_lambda_im, s5_log_dt, s5_b_re, s5_b_im, s5_c_re, s5_c_im, s5_d, s5_w_glu, kv_mod_w, kv_mod_b, kv_norm_g, w_kv, cmp_w1, cmp_b1, cmp_w2, cmp_pos, nsa_w_in, nsa_w_out, rel_bias, moe_w_router, moe_b_router, moe_w_gu, moe_b_gu, moe_w_down, moe_b_down, final_norm_g):
    bp, lp, d = x_prompt.shape
    bs, ls, _ = x_sample.shape
    assert ls == 1, "the sample path advances exactly one token per sequence"
    depth = w_mod.shape[0]
    n_a = s5_lambda_re.shape[0]
    hq = N_HEADS * HEAD_DIM
    groups = d // S5_GROUP

    c_all = jnp.concatenate([c_prompt, c_sample], axis=0)
    mod_all = _adaln(c_all, w_mod, b_mod)
    kvmod_all = _adaln(c_all, kv_mod_w[None], kv_mod_b[None])[0]
    s5ws = [_s5_weights(s5_lambda_re[i], s5_lambda_im[i], s5_log_dt[i], s5_b_re[i], s5_b_im[i],
                        s5_c_re[i], s5_c_im[i]) for i in range(n_a)]
    wglus = [s5_w_glu[i].astype(BF16) for i in range(n_a)]
    w_kv_b = w_kv.astype(BF16)
    n_in = nsa_w_in.shape[-1]
    n_in_pad = -(-n_in // LANES) * LANES
    w_ins = [jnp.pad(nsa_w_in[j], ((0, 0), (0, n_in_pad - n_in))).astype(BF16) for j in range(depth - n_a)]
    w_outs = [nsa_w_out[j].astype(BF16) for j in range(depth - n_a)]

    def run(x3, sl, seq, h0_all, mixer):
        if seq:
            nb, rb = x3.shape[0], x3.shape[1]
            mr = 1
        else:
            nb, rb = 1, x3.shape[0]
            mr = rb
        x = x3.reshape(-1, d)
        states = []
        kv = None
        for i in range(depth):
            mods = _mods(mod_all[i, sl], nb, mr)
            if i < n_a:
                ns = s5ws[i][0].shape[-1]
                h0r = h0_all[i][..., 0].reshape(nb, -1, ns)
                h0i = h0_all[i][..., 1].reshape(nb, -1, ns)
                x, sr, si = _s5_layer(x, nb, rb, seq, norm_g[i, 0], mods, s5ws[i], s5_d[i], wglus[i], h0r, h0i)
                states.append(jnp.stack([sr.reshape(-1, groups, S5_STATE), si.reshape(-1, groups, S5_STATE)], axis=-1))
            else:
                j = i - n_a
                if kv is None:
                    kvsh, kvsc = [m.reshape(nb, mr, d) for m in jnp.split(kvmod_all[sl], 2, axis=-1)]
                    kv = _normmod_matmul(x, rb, kv_norm_g.reshape(1, d), kvsh, kvsc, w_kv_b)
                proj = _normmod_matmul(x, rb, norm_g[i, 0].reshape(1, d), mods[0], mods[1], w_ins[j], sig_from=hq)
                o = mixer(proj, kv)
                x = _matmul_residual(o, x, mods[2], w_outs[j], rb)
            x = _moe(x, rb, norm_g[i, 1].reshape(1, d), mods[3], mods[4], mods[5], i, moe_w_router[i], moe_b_router[i],
                     moe_w_gu, moe_b_gu, moe_w_down, moe_b_down)
        return _final_norm(x, final_norm_g, rb), jnp.stack(states), kv

    def prompt_mixer(proj, kv):
        o = _nsa_prompt(proj, kv.reshape(bp, lp, N_BRANCH_KV, N_KV_HEADS, HEAD_DIM), rel_bias,
                        cmp_w1, cmp_b1, cmp_w2, cmp_pos)
        return o.reshape(bp * lp, hq)

    def sample_mixer(proj, kv):
        return _nsa_sample(proj[:, :hq], proj[:, hq:n_in], kv.reshape(bs, N_BRANCH_KV, N_KV_HEADS, HEAD_DIM),
                           cache_win_kv, cache_nsa_kv, page_table, rel_bias, cmp_w1, cmp_b1, cmp_w2, cmp_pos)

    h0_p = jnp.zeros((n_a, bp, groups, S5_STATE, 2), F32)
    y_p, st_p, kv_p = run(x_prompt, slice(0, bp), True, h0_p, prompt_mixer)
    y_s, st_s, kv_s = run(x_sample.reshape(bs, d), slice(bp, bp + bs), False, state_s5, sample_mixer)
    kv_p = kv_p.reshape(bp, lp, N_BRANCH_KV, N_KV_HEADS, HEAD_DIM)
    kv_s = kv_s.reshape(bs, ls, N_BRANCH_KV, N_KV_HEADS, HEAD_DIM)
    win_p = kv_p[:, -min(WINDOW, lp):, 4:6]
    win_s = jnp.concatenate([cache_win_kv, kv_s[:, :, 4:6]], axis=1)[:, -cache_win_kv.shape[1]:]
    return (y_p.reshape(bp, lp, d), y_s.reshape(bs, ls, d), st_p, st_s, kv_p[:, :, :4], kv_s[:, :, :4], win_p, win_s)
```

```python
import functools
import math

import numpy as np
import jax
import jax.numpy as jnp
from jax import lax
from jax.experimental import pallas as pl
from jax.experimental.pallas import tpu as pltpu

F32 = jnp.float32
BF16 = jnp.bfloat16
I32 = jnp.int32

S5_GROUP = 16
S5_STATE = 64
N_HEADS = 16
HEAD_DIM = 64
N_KV_HEADS = 4
GQ = N_HEADS // N_KV_HEADS
CMP_BLOCK = 32
CMP_STRIDE = 16
CMP_RATIO = CMP_BLOCK // CMP_STRIDE
SEL_BLOCK = 64
SEL_TOP = 16
WINDOW = 512
Q_BLOCK = 128
FORCE_SCORE = 1e4
N_BRANCH_KV = 6
REL_BUCKETS = 32
REL_MAX_DIST = 128
N_EXPERTS = 32
TOP_K = 4
SWIGLU_LIMIT = 7.0
SWIGLU_ALPHA = 1.702
RMS_EPS = 1e-5
PAGE_SIZE = 128

LANES = 128
SUBLANES = 8
VMEM_LIMIT = 56 * 1024 * 1024

MOE_ROWS = 256
FAR_GROUP = 4
S5_LANES = 256
S5_STATES = S5_LANES // S5_GROUP * S5_STATE
NEG = -1e30


def _cparams(*sem):
    return pltpu.CompilerParams(dimension_semantics=sem, vmem_limit_bytes=VMEM_LIMIT)


def _normmod(x, g, sh, sc):
    y = x * lax.rsqrt(jnp.mean(x * x, axis=-1, keepdims=True) + RMS_EPS)
    return (y * g) * (1.0 + sc) + sh


def _dot(a, b):
    return jnp.dot(a, b, preferred_element_type=F32)


def _dot_nt(a, b):
    return lax.dot_general(a, b, (((1,), (1,)), ((), ())), preferred_element_type=F32)


def _split(x):
    hi = x.astype(BF16)
    lo = (x - hi.astype(F32)).astype(BF16)
    return hi, lo


def _mod_spec(m, rb, tm):
    nb, mr, d = m.shape
    if mr == 1:
        return pl.BlockSpec((1, 1, d), lambda i, *_: ((i * tm) // rb, 0, 0))
    assert mr == rb and rb % tm == 0
    per = rb // tm
    return pl.BlockSpec((1, tm, d), lambda i, *_: (i // per, i % per, 0))


def _row_tile(rb, want):
    t = math.gcd(rb, want)
    return t


def _adaln_kernel(c_ref, w_ref, b_ref, o_ref):
    c = c_ref[...]
    a = (c * jax.nn.sigmoid(c)).astype(BF16)
    o_ref[0] = _dot(a, w_ref[0].astype(BF16)) + b_ref[0]


def _adaln(c, w, b):
    n, d = c.shape
    nl, _, nn = w.shape
    npad = -(-n // SUBLANES) * SUBLANES
    cp = jnp.pad(c, ((0, npad - n), (0, 0)))
    tn = math.gcd(nn, 1024)
    out = pl.pallas_call(
        _adaln_kernel,
        grid=(nl, nn // tn),
        in_specs=[pl.BlockSpec((npad, d), lambda l, j: (0, 0)),
                  pl.BlockSpec((1, d, tn), lambda l, j: (l, 0, j)),
                  pl.BlockSpec((1, 1, tn), lambda l, j: (l, 0, j))],
        out_specs=pl.BlockSpec((1, npad, tn), lambda l, j: (l, 0, j)),
        out_shape=jax.ShapeDtypeStruct((nl, npad, nn), F32),
        compiler_params=_cparams("parallel", "parallel"),
        name="adaln",
    )(cp, w, b.reshape(nl, 1, nn))
    return out[:, :n]


def _s5_disc_kernel(lr_ref, li_ref, ldt_ref, br_ref, bi_ref, ar_ref, ai_ref, bbr_ref, bbi_ref):
    lr, li = lr_ref[...], li_ref[...]
    dt = jnp.exp(ldt_ref[...])
    mag = jnp.exp(dt * lr)
    ar, ai = mag * jnp.cos(dt * li), mag * jnp.sin(dt * li)
    den = lr * lr + li * li
    nr, ni = ar - 1.0, ai
    fr = (nr * lr + ni * li) / den
    fi = (ni * lr - nr * li) / den
    br, bi = br_ref[...], bi_ref[...]
    ar_ref[...] = ar
    ai_ref[...] = ai
    bbr_ref[...] = fr * br - fi * bi
    bbi_ref[...] = fr * bi + fi * br


def _s5_discretize(lam_re, lam_im, log_dt, b_re, b_im):
    g, p = lam_re.shape
    c = b_re.shape[-1]
    full3 = lambda s: pl.BlockSpec(s, lambda: (0, 0, 0))
    return pl.pallas_call(
        _s5_disc_kernel,
        in_specs=[full3((g, p, 1)), full3((g, p, 1)), full3((g, 1, 1)), full3((g, p, c)), full3((g, p, c))],
        out_specs=[full3((g, p, 1)), full3((g, p, 1)), full3((g, p, c)), full3((g, p, c))],
        out_shape=[jax.ShapeDtypeStruct((g, p, 1), F32)] * 2 + [jax.ShapeDtypeStruct((g, p, c), F32)] * 2,
        name="s5_discretize",
    )(lam_re.reshape(g, p, 1), lam_im.reshape(g, p, 1), log_dt.reshape(g, 1, 1), b_re, b_im)


def _s5_kernel(x_ref, g_ref, sh_ref, sc_ref, d_ref, wbr_ref, wbi_ref, ar_ref, ai_ref, wcr_ref, wci_ref,
               h0r_ref, h0i_ref, y_ref, str_ref, sti_ref, bur, bui, sre, sim, car_r, car_i, *, seq, rows, nj):
    c = pl.program_id(1)
    h = _normmod(x_ref[0], g_ref[...], sh_ref[0], sc_ref[0])
    dsk = d_ref[...]
    for j in range(nj):
        lo, hi = j * S5_LANES, (j + 1) * S5_LANES
        slo, shi = j * S5_STATES, (j + 1) * S5_STATES
        hj = h[:, lo:hi]
        hb = hj.astype(BF16)
        ar, ai = ar_ref[:, slo:shi], ai_ref[:, slo:shi]
        if seq:
            @pl.when(c == 0)
            def _():
                car_r[j] = h0r_ref[0, :, slo:shi]
                car_i[j] = h0i_ref[0, :, slo:shi]
            bur[...] = _dot(hb, wbr_ref[j])
            bui[...] = _dot(hb, wbi_ref[j])

            def step(t, carry):
                sr, si = carry
                br = bur[pl.ds(t, 1), :]
                bi = bui[pl.ds(t, 1), :]
                nr = ar * sr - ai * si + br
                ni = ar * si + ai * sr + bi
                sre[pl.ds(t, 1), :] = nr
                sim[pl.ds(t, 1), :] = ni
                return nr, ni

            sr, si = lax.fori_loop(0, rows, step, (car_r[j], car_i[j]), unroll=8)
            car_r[j] = sr
            car_i[j] = si
            str_ref[0, :, slo:shi] = sr
            sti_ref[0, :, slo:shi] = si
            s_r, s_i = sre[...], sim[...]
        else:
            h0r, h0i = h0r_ref[0, :, slo:shi], h0i_ref[0, :, slo:shi]
            s_r = ar * h0r - ai * h0i + _dot(hb, wbr_ref[j])
            s_i = ar * h0i + ai * h0r + _dot(hb, wbi_ref[j])
            str_ref[0, :, slo:shi] = s_r
            sti_ref[0, :, slo:shi] = s_i
        y = _dot(s_r.astype(BF16), wcr_ref[j]) - _dot(s_i.astype(BF16), wci_ref[j])
        y_ref[0, :, lo:hi] = jax.nn.gelu(y + dsk[:, lo:hi] * hj)


def _s5_scan(x3, g, sh, sc, dsk, wbr, wbi, ar, ai, wcr, wci, h0r, h0i, *, seq):
    nb, rb, d = x3.shape
    ns = ar.shape[-1]
    nj = d // S5_LANES
    rows = _row_tile(rb, 256)
    hr = 1 if seq else rows
    mrow = sh.shape[1]
    mspec = (pl.BlockSpec((1, 1, d), lambda b, c: (b, 0, 0)) if mrow == 1
             else pl.BlockSpec((1, rows, d), lambda b, c: (b, c, 0)))
    hspec = (pl.BlockSpec((1, 1, ns), lambda b, c: (b, 0, 0)) if seq
             else pl.BlockSpec((1, rows, ns), lambda b, c: (b, c, 0)))
    const2 = lambda s: pl.BlockSpec(s, lambda b, c: (0, 0))
    const3 = lambda s: pl.BlockSpec(s, lambda b, c: (0, 0, 0))
    kern = functools.partial(_s5_kernel, seq=seq, rows=rows, nj=nj)
    return pl.pallas_call(
        kern,
        grid=(nb, rb // rows),
        in_specs=[pl.BlockSpec((1, rows, d), lambda b, c: (b, c, 0)), const2((1, d)), mspec, mspec, const2((1, d)),
                  const3(wbr.shape), const3(wbi.shape), const2((1, ns)), const2((1, ns)),
                  const3(wcr.shape), const3(wci.shape), hspec, hspec],
        out_specs=[pl.BlockSpec((1, rows, d), lambda b, c: (b, c, 0)), hspec, hspec],
        out_shape=[jax.ShapeDtypeStruct((nb, rb, d), F32),
                   jax.ShapeDtypeStruct(h0r.shape, F32), jax.ShapeDtypeStruct(h0i.shape, F32)],
        scratch_shapes=[pltpu.VMEM((rows, S5_STATES), F32)] * 4 + [pltpu.VMEM((nj, 1, S5_STATES), F32)] * 2,
        compiler_params=_cparams("parallel", "arbitrary"),
        name="s5_scan" if seq else "s5_step",
    )(x3, g, sh, sc, dsk, wbr, wbi, ar, ai, wcr, wci, h0r, h0i)


def _s5_weights(lam_re, lam_im, log_dt, b_re, b_im, c_re, c_im):
    ar, ai, bbr, bbi = _s5_discretize(lam_re, lam_im, log_dt, b_re, b_im)
    g, p, c = bbr.shape
    gs = S5_LANES // c
    nj = g // gs
    eye = jnp.eye(gs, dtype=F32)

    def wb(bb):
        w = jnp.einsum('jgpc,gh->jgchp', bb.reshape(nj, gs, p, c), eye)
        return w.reshape(nj, gs * c, gs * p).astype(BF16)

    def wc(cc):
        w = jnp.einsum('jgcp,gh->jgphc', cc.reshape(nj, gs, c, p), eye)
        return w.reshape(nj, gs * p, gs * c).astype(BF16)

    return (ar.reshape(1, g * p), ai.reshape(1, g * p), wb(bbr), wb(bbi), wc(c_re), wc(c_im))


def _glu_kernel(a_ref, x_ref, g1_ref, w_ref, o_ref, *, d):
    gu = _dot(a_ref[...].astype(BF16), w_ref[...])
    o_ref[...] = x_ref[...] + g1_ref[0] * (gu[:, :d] * jax.nn.sigmoid(gu[:, d:]))


def _glu_residual(a, x, g1, w_bf16, rb):
    t, d = x.shape
    tm = _row_tile(rb, 512)
    return pl.pallas_call(
        functools.partial(_glu_kernel, d=d),
        grid=(t // tm,),
        in_specs=[pl.BlockSpec((tm, d), lambda i: (i, 0)), pl.BlockSpec((tm, d), lambda i: (i, 0)),
                  _mod_spec(g1, rb, tm), pl.BlockSpec(w_bf16.shape, lambda i: (0, 0))],
        out_specs=pl.BlockSpec((tm, d), lambda i: (i, 0)),
        out_shape=jax.ShapeDtypeStruct((t, d), F32),
        compiler_params=_cparams("parallel"),
        name="glu_residual",
    )(a, x, g1, w_bf16)


def _router_kernel(x_ref, g_ref, sh_ref, sc_ref, w_ref, b_ref, ti_ref, tw_ref, rk_ref, cnt_ref, carry, *, tm):
    i = pl.program_id(0)

    @pl.when(i == 0)
    def _():
        carry[...] = jnp.zeros_like(carry)

    h = _normmod(x_ref[...], g_ref[...], sh_ref[0], sc_ref[0])
    h_hi, h_lo = _split(h)
    w_hi, w_lo = _split(w_ref[...])
    logits = _dot(h_hi, w_hi) + (_dot(h_lo, w_hi) + _dot(h_hi, w_lo)) + b_ref[...]
    lane = lax.broadcasted_iota(I32, (tm, LANES), 1)
    lane_f = lane.astype(F32)
    work = jnp.where(lane < N_EXPERTS, logits, -jnp.inf)
    vals, sels, idxs = [], [], []
    for _ in range(TOP_K):
        m = jnp.max(work, axis=-1, keepdims=True)
        idx = jnp.min(jnp.where(work == m, lane_f, float(LANES)), axis=-1, keepdims=True)
        sel = lane_f == idx
        work = jnp.where(sel, -jnp.inf, work)
        vals.append(m)
        sels.append(sel)
        idxs.append(idx)
    es = [jnp.exp(v - vals[0]) for v in vals]
    tot = es[0] + es[1] + es[2] + es[3]
    oh = jnp.zeros((tm, LANES), F32)
    for sel in sels:
        oh = oh + sel.astype(F32)
    r_io = lax.broadcasted_iota(I32, (tm, tm), 0)
    c_io = lax.broadcasted_iota(I32, (tm, tm), 1)
    lstrict = (c_io < r_io).astype(BF16)
    base = carry[...] + _dot(lstrict, oh.astype(BF16))
    ti = jnp.zeros((tm, LANES), F32)
    tw = jnp.zeros((tm, LANES), F32)
    rk = jnp.zeros((tm, LANES), F32)
    for k in range(TOP_K):
        rank = jnp.sum(jnp.where(sels[k], base, 0.0), axis=-1, keepdims=True)
        ti = jnp.where(lane == k, idxs[k], ti)
        tw = jnp.where(lane == k, es[k] / tot, tw)
        rk = jnp.where(lane == k, rank, rk)
    ti_ref[...] = ti.astype(I32)
    tw_ref[...] = tw
    rk_ref[...] = rk.astype(I32)
    carry[...] = carry[...] + jnp.sum(oh, axis=0, keepdims=True)
    cnt_ref[...] = carry[...]


def _router(x, rb, g, sh, sc, w_pad, b_pad):
    t, d = x.shape
    tm = _row_tile(rb, 512)
    row = lambda i: (i, 0)
    return pl.pallas_call(
        functools.partial(_router_kernel, tm=tm),
        grid=(t // tm,),
        in_specs=[pl.BlockSpec((tm, d), row), pl.BlockSpec((1, d), lambda i: (0, 0)),
                  _mod_spec(sh, rb, tm), _mod_spec(sc, rb, tm),
                  pl.BlockSpec((d, LANES), lambda i: (0, 0)), pl.BlockSpec((1, LANES), lambda i: (0, 0))],
        out_specs=[pl.BlockSpec((tm, LANES), row)] * 3 + [pl.BlockSpec((1, LANES), lambda i: (0, 0))],
        out_shape=[jax.ShapeDtypeStruct((t, LANES), I32), jax.ShapeDtypeStruct((t, LANES), F32),
                   jax.ShapeDtypeStruct((t, LANES), I32), jax.ShapeDtypeStruct((1, LANES), F32)],
        scratch_shapes=[pltpu.VMEM((1, LANES), F32)],
        compiler_params=_cparams("arbitrary"),
        name="moe_router",
    )(x, g, sh, sc, w_pad, b_pad)


def _row_copy(src, dst, sem):
    return pltpu.make_async_copy(src, dst, sem)


def _dispatch_kernel(dest_ref, x_ref, g_ref, sh_ref, sc_ref, xs_in_ref, xs_ref, hbuf, sem, *, tm):
    del xs_in_ref
    hbuf[...] = _normmod(x_ref[...], g_ref[...], sh_ref[0], sc_ref[0])

    def start(r, _):
        for k in range(TOP_K):
            d = dest_ref[r * TOP_K + k]
            _row_copy(hbuf.at[pl.ds(r, 1)], xs_ref.at[pl.ds(d, 1)], sem).start(priority=k % 2)
        return 0

    lax.fori_loop(0, tm, start, 0)

    def wait(r, _):
        for k in range(TOP_K):
            _row_copy(hbuf.at[pl.ds(0, 1)], xs_ref.at[pl.ds(0, 1)], sem).wait()
        return 0

    lax.fori_loop(0, tm, wait, 0)


def _dispatch(dest, x, rb, g, sh, sc, n_rows):
    t, d = x.shape
    tm = _row_tile(rb, 256)
    zeros = jnp.zeros((n_rows, d), F32)
    return pl.pallas_call(
        functools.partial(_dispatch_kernel, tm=tm),
        grid=(t // tm,),
        in_specs=[pl.BlockSpec((tm * TOP_K,), lambda i: (i,), memory_space=pltpu.SMEM),
                  pl.BlockSpec((tm, d), lambda i: (i, 0)), pl.BlockSpec((1, d), lambda i: (0, 0)),
                  _mod_spec(sh, rb, tm), _mod_spec(sc, rb, tm),
                  pl.BlockSpec(memory_space=pl.ANY)],
        out_specs=pl.BlockSpec(memory_space=pl.ANY),
        out_shape=jax.ShapeDtypeStruct((n_rows, d), F32),
        scratch_shapes=[pltpu.VMEM((tm, d), F32), pltpu.SemaphoreType.DMA(())],
        input_output_aliases={5: 0},
        compiler_params=_cparams("arbitrary"),
        name="moe_dispatch",
    )(dest, x, g, sh, sc, zeros)


def _expert_kernel(be_ref, xs_ref, wgu_ref, bgu_ref, wd_ref, bd_ref, ys_ref, wgu_s, wd_s, *, de):
    i = pl.program_id(0)
    changed = jnp.logical_or(i == 0, be_ref[i] != be_ref[jnp.maximum(i - 1, 0)])

    @pl.when(changed)
    def _():
        wgu_s[...] = wgu_ref[0].astype(BF16)
        wd_s[...] = wd_ref[0].astype(BF16)

    gu = _dot(xs_ref[...].astype(BF16), wgu_s[...]) + bgu_ref[0]
    gate = jnp.minimum(gu[:, :de], SWIGLU_LIMIT)
    up = jnp.clip(gu[:, de:], -SWIGLU_LIMIT, SWIGLU_LIMIT)
    act = (up + 1.0) * gate * jax.nn.sigmoid(SWIGLU_ALPHA * gate)
    ys_ref[...] = _dot(act.astype(BF16), wd_s[...]) + bd_ref[0]


def _experts(blk_e, xs, layer, w_gu, b_gu, w_down, b_down):
    n_rows, d = xs.shape
    nl, ne, _, de2 = w_gu.shape
    de = de2 // 2
    n_blk = n_rows // MOE_ROWS
    grid_spec = pltpu.PrefetchScalarGridSpec(
        num_scalar_prefetch=1,
        grid=(n_blk,),
        in_specs=[pl.BlockSpec((MOE_ROWS, d), lambda i, be: (i, 0)),
                  pl.BlockSpec((None, 1, d, de2), lambda i, be: (layer, be[i], 0, 0)),
                  pl.BlockSpec((None, 1, 1, de2), lambda i, be: (layer, be[i], 0, 0)),
                  pl.BlockSpec((None, 1, de, d), lambda i, be: (layer, be[i], 0, 0)),
                  pl.BlockSpec((None, 1, 1, d), lambda i, be: (layer, be[i], 0, 0))],
        out_specs=pl.BlockSpec((MOE_ROWS, d), lambda i, be: (i, 0)),
        scratch_shapes=[pltpu.VMEM((d, de2), BF16), pltpu.VMEM((de, d), BF16)],
    )
    return pl.pallas_call(
        functools.partial(_expert_kernel, de=de),
        grid_spec=grid_spec,
        out_shape=jax.ShapeDtypeStruct((n_rows, d), F32),
        compiler_params=_cparams("arbitrary"),
        name="moe_experts",
    )(blk_e, xs, w_gu, b_gu.reshape(nl, ne, 1, de2), w_down, b_down.reshape(nl, ne, 1, d))


def _combine_kernel(dest_ref, tw_ref, x_ref, g2_ref, ys_ref, o_ref, gbuf, sem, *, tm):
    def start(r, _):
        for k in range(TOP_K):
            d = dest_ref[r * TOP_K + k]
            _row_copy(ys_ref.at[pl.ds(d, 1)], gbuf.at[k, pl.ds(r, 1)], sem).start(priority=k % 2)
        return 0

    lax.fori_loop(0, tm, start, 0)

    def wait(r, _):
        for k in range(TOP_K):
            _row_copy(ys_ref.at[pl.ds(0, 1)], gbuf.at[k, pl.ds(0, 1)], sem).wait()
        return 0

    lax.fori_loop(0, tm, wait, 0)
    tw = tw_ref[...]
    y = tw[:, 0:1] * gbuf[0]
    for k in range(1, TOP_K):
        y = y + tw[:, k:k + 1] * gbuf[k]
    o_ref[...] = x_ref[...] + g2_ref[0] * y


def _combine(dest, tw, x, rb, g2, ys):
    t, d = x.shape
    tm = _row_tile(rb, 256)
    return pl.pallas_call(
        functools.partial(_combine_kernel, tm=tm),
        grid=(t // tm,),
        in_specs=[pl.BlockSpec((tm * TOP_K,), lambda i: (i,), memory_space=pltpu.SMEM),
                  pl.BlockSpec((tm, LANES), lambda i: (i, 0)), pl.BlockSpec((tm, d), lambda i: (i, 0)),
                  _mod_spec(g2, rb, tm), pl.BlockSpec(memory_space=pl.ANY)],
        out_specs=pl.BlockSpec((tm, d), lambda i: (i, 0)),
        out_shape=jax.ShapeDtypeStruct((t, d), F32),
        scratch_shapes=[pltpu.VMEM((TOP_K, tm, d), F32), pltpu.SemaphoreType.DMA(())],
        compiler_params=_cparams("arbitrary"),
        name="moe_combine",
    )(dest, tw, x, g2, ys)


def _moe(x, rb, g, sh, sc, g2, layer, w_router, b_router, w_gu, b_gu, w_down, b_down):
    t, d = x.shape
    ne = w_router.shape[1]
    w_pad = jnp.pad(w_router, ((0, 0), (0, LANES - ne)))
    b_pad = jnp.pad(b_router, (0, LANES - ne)).reshape(1, LANES)
    ti, tw, rk, cnt = _router(x, rb, g, sh, sc, w_pad, b_pad)
    counts = cnt[0, :ne].astype(I32)
    padded = (counts + MOE_ROWS - 1) // MOE_ROWS * MOE_ROWS
    pend = jnp.cumsum(padded)
    pstart = pend - padded
    top_i = ti[:, :TOP_K]
    start_of = jnp.sum(jnp.where(top_i[..., None] == jnp.arange(ne, dtype=I32), pstart, 0), axis=-1)
    dest = (start_of + rk[:, :TOP_K]).reshape(-1).astype(I32)
    n_blk = -(-(t * TOP_K) // MOE_ROWS) + ne
    first_row = jnp.arange(n_blk, dtype=I32)[:, None] * MOE_ROWS
    blk_e = jnp.minimum(jnp.sum((pend[None, :] <= first_row).astype(I32), axis=1), ne - 1)
    xs = _dispatch(dest, x, rb, g, sh, sc, n_blk * MOE_ROWS)
    ys = _experts(blk_e, xs, layer, w_gu, b_gu, w_down, b_down)
    return _combine(dest, tw, x, rb, g2, ys)


def _nm_matmul_kernel(x_ref, g_ref, sh_ref, sc_ref, w_ref, o_ref, *, sig_from):
    h = _normmod(x_ref[...], g_ref[...], sh_ref[0], sc_ref[0])
    o = _dot(h.astype(BF16), w_ref[...])
    if sig_from is not None:
        col = lax.broadcasted_iota(I32, o.shape, 1)
        o = jnp.where(col >= sig_from, jax.nn.sigmoid(o), o)
    o_ref[...] = o


def _normmod_matmul(x, rb, g, sh, sc, w_bf16, sig_from=None):
    t, d = x.shape
    n = w_bf16.shape[1]
    tm = _row_tile(rb, 512)
    return pl.pallas_call(
        functools.partial(_nm_matmul_kernel, sig_from=sig_from),
        grid=(t // tm,),
        in_specs=[pl.BlockSpec((tm, d), lambda i: (i, 0)), pl.BlockSpec((1, d), lambda i: (0, 0)),
                  _mod_spec(sh, rb, tm), _mod_spec(sc, rb, tm), pl.BlockSpec((d, n), lambda i: (0, 0))],
        out_specs=pl.BlockSpec((tm, n), lambda i: (i, 0)),
        out_shape=jax.ShapeDtypeStruct((t, n), F32),
        compiler_params=_cparams("parallel"),
        name="normmod_matmul",
    )(x, g, sh, sc, w_bf16)


def _mm_residual_kernel(a_ref, x_ref, g1_ref, w_ref, o_ref):
    o_ref[...] = x_ref[...] + g1_ref[0] * _dot(a_ref[...].astype(BF16), w_ref[...])


def _matmul_residual(a, x, g1, w_bf16, rb):
    t, d = x.shape
    k = a.shape[1]
    tm = _row_tile(rb, 512)
    return pl.pallas_call(
        _mm_residual_kernel,
        grid=(t // tm,),
        in_specs=[pl.BlockSpec((tm, k), lambda i: (i, 0)), pl.BlockSpec((tm, d), lambda i: (i, 0)),
                  _mod_spec(g1, rb, tm), pl.BlockSpec((k, d), lambda i: (0, 0))],
        out_specs=pl.BlockSpec((tm, d), lambda i: (i, 0)),
        out_shape=jax.ShapeDtypeStruct((t, d), F32),
        compiler_params=_cparams("parallel"),
        name="matmul_residual",
    )(a, x, g1, w_bf16)


def _final_norm_kernel(x_ref, g_ref, o_ref):
    x = x_ref[...]
    o_ref[...] = x * lax.rsqrt(jnp.mean(x * x, axis=-1, keepdims=True) + RMS_EPS) * g_ref[...]


def _final_norm(x, g, rb):
    t, d = x.shape
    tm = _row_tile(rb, 512)
    return pl.pallas_call(
        _final_norm_kernel,
        grid=(t // tm,),
        in_specs=[pl.BlockSpec((tm, d), lambda i: (i, 0)), pl.BlockSpec((1, d), lambda i: (0, 0))],
        out_specs=pl.BlockSpec((tm, d), lambda i: (i, 0)),
        out_shape=jax.ShapeDtypeStruct((t, d), F32),
        compiler_params=_cparams("parallel"),
        name="final_norm",
    )(x, g.reshape(1, d))


def _rel_bucket_table():
    n = np.arange(REL_MAX_DIST + 1)
    max_exact = REL_BUCKETS // 2
    nf = np.maximum(n, 1).astype(np.float32)
    large = max_exact + (np.log(nf / max_exact) / math.log(REL_MAX_DIST / max_exact)
                         * (REL_BUCKETS - max_exact)).astype(np.int32)
    return np.where(n < max_exact, n, np.minimum(large, REL_BUCKETS - 1)).astype(np.int32)


def _bias_by_dist(rel_bias):
    return rel_bias.astype(F32)[_rel_bucket_table()].T


def _stack_heads(tab):
    h, r, c = tab.shape
    return tab.reshape(N_KV_HEADS, GQ * r, c)


def _dist_tile(rows, cols, offset, col_step=1):
    d = offset + np.arange(rows)[:, None] - col_step * np.arange(cols)[None, :]
    return np.clip(d, 0, REL_MAX_DIST)


CMP_NEAR = 32
CMP_PAD = 16


def _overlap_matrix(n_cmp, n_sel_pad):
    ci = np.arange(n_cmp)[:, None] * CMP_STRIDE
    sj = np.arange(n_sel_pad)[None, :] * SEL_BLOCK
    ov = ((ci < sj + SEL_BLOCK) & (ci + CMP_BLOCK > sj)).astype(np.float32)
    return np.pad(ov, ((CMP_PAD, CMP_PAD), (0, 0)))


def _compress_kernel(x_ref, pos_ref, w1_ref, b1_ref, w2_ref, o_ref, *, hid):
    w1 = w1_ref[0]
    parts = _dot(x_ref[0, 0].astype(BF16), w1)
    pt = _dot(pos_ref[0].astype(BF16), w1)
    p0 = parts[:, :hid] + pt[0:1, :hid]
    p1 = parts[:, hid:] + pt[1:2, hid:]
    n = p1.shape[0]
    pre = p0 + pltpu.roll(p1, n - 1, axis=0) + b1_ref[0]
    o_ref[0, 0] = _dot(jax.nn.gelu(pre).astype(BF16), w2_ref[0])


def _compress(xc, cmp_w1, cmp_b1, cmp_w2, cmp_pos):
    _, ng, nch, kdim = xc.shape
    hid = cmp_w1.shape[-1]
    hd = cmp_w2.shape[-1]
    w1 = cmp_w1.reshape(2, CMP_RATIO, kdim, hid)
    w1cat = jnp.concatenate([w1[:, r] for r in range(CMP_RATIO)], axis=-1).astype(BF16)
    pos = jnp.pad(cmp_pos.reshape(2, CMP_RATIO, kdim), ((0, 0), (0, SUBLANES - CMP_RATIO), (0, 0)))
    return pl.pallas_call(
        functools.partial(_compress_kernel, hid=hid),
        grid=(2, ng),
        in_specs=[pl.BlockSpec((1, 1, nch, kdim), lambda c, n: (c, n, 0, 0)),
                  pl.BlockSpec((1, SUBLANES, kdim), lambda c, n: (c, 0, 0)),
                  pl.BlockSpec((1, kdim, CMP_RATIO * hid), lambda c, n: (c, 0, 0)),
                  pl.BlockSpec((1, 1, hid), lambda c, n: (c, 0, 0)),
                  pl.BlockSpec((1, hid, hd), lambda c, n: (c, 0, 0))],
        out_specs=pl.BlockSpec((1, 1, nch, hd), lambda c, n: (c, n, 0, 0)),
        out_shape=jax.ShapeDtypeStruct((2, ng, nch, hd), F32),
        compiler_params=_cparams("parallel", "parallel"),
        name="nsa_compress",
    )(xc, pos, w1cat, cmp_b1.reshape(2, 1, hid), cmp_w2.astype(BF16))


def _dot3_nt(a, b):
    a_hi, a_lo = _split(a)
    b_hi, b_lo = _split(b)
    return _dot_nt(a_hi, b_hi) + (_dot_nt(a_lo, b_hi) + _dot_nt(a_hi, b_lo))


def _eye(n):
    r = lax.broadcasted_iota(I32, (n, n), 0)
    c = lax.broadcasted_iota(I32, (n, n), 1)
    return (r == c).astype(BF16)


def _transpose_bf16(x, eye):
    return _dot_nt(eye, x).astype(BF16)


def _group_q_transposed(q):
    full = _transpose_bf16(q, _eye(GQ * HEAD_DIM))
    return jnp.concatenate([full[g * HEAD_DIM:(g + 1) * HEAD_DIM, :] for g in range(GQ)], axis=1)


def _top_blocks_cols(score, ntop):
    row_f = lax.broadcasted_iota(I32, score.shape, 0).astype(F32)
    sel = jnp.zeros(score.shape, F32)
    for _ in range(ntop):
        m = jnp.max(score, axis=0, keepdims=True)
        idx = jnp.min(jnp.where(score == m, row_f, float(score.shape[0])), axis=0, keepdims=True)
        hit = row_f == idx
        sel = jnp.where(jnp.logical_and(hit, m > -jnp.inf), 1.0, sel)
        score = jnp.where(hit, -jnp.inf, score)
    return sel


def _cmp_attn_kernel(q_ref, kc_ref, vc_ref, vct_ref, ov_ref, ovt_ref, bnear_ref, bfar_ref, oct_ref, selt_ref,
                     *, nch, ntop, qblk):
    qb = pl.program_id(2)
    cols = GQ * qblk
    scale = HEAD_DIM ** -0.5
    c = pl.multiple_of(qb * (qblk // CMP_STRIDE), SUBLANES)
    eye = _eye(HEAD_DIM)
    q_hi, q_lo = _split(q_ref[...] * scale)
    qt_hi = _group_q_transposed(q_hi)
    qt_lo = _group_q_transposed(q_lo)

    def logits(k):
        k_hi, k_lo = _split(k)
        return _dot(k_hi, qt_hi) + (_dot(k_lo, qt_hi) + _dot(k_hi, qt_lo))

    s_far = logits(kc_ref[0, 0, CMP_PAD:CMP_PAD + nch, :]) + bfar_ref[0]
    s_near = logits(kc_ref[0, 0, pl.ds(c, CMP_NEAR), :]) + bnear_ref[0]
    mask_far = lax.broadcasted_iota(I32, (nch, cols), 0) < c - CMP_PAD
    rel = lax.broadcasted_iota(I32, (CMP_NEAR, cols), 0)
    i_io = lax.broadcasted_iota(I32, (CMP_NEAR, cols), 1) & (qblk - 1)
    mask_near = jnp.logical_and(CMP_STRIDE * (rel - CMP_PAD) + CMP_BLOCK - 1 <= i_io, rel + c - CMP_PAD >= 0)
    sf = jnp.where(mask_far, s_far, NEG)
    sn = jnp.where(mask_near, s_near, NEG)
    m = jnp.maximum(jnp.max(sf, axis=0, keepdims=True), jnp.max(sn, axis=0, keepdims=True))
    pf = jnp.where(mask_far, jnp.exp(sf - m), 0.0)
    pn = jnp.where(mask_near, jnp.exp(sn - m), 0.0)
    l = jnp.sum(pf, axis=0, keepdims=True) + jnp.sum(pn, axis=0, keepdims=True)
    linv = 1.0 / jnp.where(l > 0, l, 1.0)
    vnt = _transpose_bf16(vc_ref[0, 0, pl.ds(c, CMP_NEAR), :].astype(BF16), eye)
    o = _dot(vct_ref[0, 0].astype(BF16), pf.astype(BF16)) + _dot(vnt, pn.astype(BF16))
    oct_ref[0, 0, 0] = o * linv
    pfn = pf * linv
    pnn = pn * linv
    pgf = pfn[:, 0:qblk]
    pgn = pnn[:, 0:qblk]
    for g in range(1, GQ):
        pgf = pgf + pfn[:, g * qblk:(g + 1) * qblk]
        pgn = pgn + pnn[:, g * qblk:(g + 1) * qblk]
    f_hi, f_lo = _split(pgf)
    n_hi, n_lo = _split(pgn)
    nsp = ovt_ref.shape[0]
    ovt = ovt_ref[...]
    ovnt = _transpose_bf16(ov_ref[pl.ds(c, CMP_NEAR), :], _eye(nsp))
    imp = (_dot(ovt, f_hi) + _dot(ovnt, n_hi)) + (_dot(ovt, f_lo) + _dot(ovnt, n_lo))
    s_io = lax.broadcasted_iota(I32, (nsp, qblk), 0)
    t = qb * qblk + lax.broadcasted_iota(I32, (nsp, qblk), 1)
    cur = lax.shift_right_logical(t, int(math.log2(SEL_BLOCK)))
    forced = jnp.logical_or(s_io == 0, jnp.logical_or(s_io == cur, s_io == cur - 1))
    score = jnp.where(s_io * SEL_BLOCK <= t, imp + jnp.where(forced, FORCE_SCORE, 0.0), -jnp.inf)
    selt_ref[0, 0, 0] = jnp.where(_top_blocks_cols(score, ntop) > 0.5, 0.0, NEG)


def _cmp_attention(q2d, b, t, kc_pad, vc_pad, vct, ov, ovt, bnear, bfar, nch, ntop):
    hd = HEAD_DIM
    qblk = math.gcd(t, Q_BLOCK)
    nqb = t // qblk
    nsp = ov.shape[1]
    cols = GQ * qblk
    kern = functools.partial(_cmp_attn_kernel, nch=nch, ntop=ntop, qblk=qblk)
    kvspec = pl.BlockSpec((1, 1, kc_pad.shape[2], hd), lambda bb, k, i: (bb, k, 0, 0))
    return pl.pallas_call(
        kern,
        grid=(b, N_KV_HEADS, nqb),
        in_specs=[pl.BlockSpec((qblk, GQ * hd), lambda bb, k, i: (bb * nqb + i, k)), kvspec, kvspec,
                  pl.BlockSpec((1, 1, hd, nch), lambda bb, k, i: (bb, k, 0, 0)),
                  pl.BlockSpec(ov.shape, lambda bb, k, i: (0, 0)), pl.BlockSpec(ovt.shape, lambda bb, k, i: (0, 0)),
                  pl.BlockSpec((1,) + bnear.shape[1:], lambda bb, k, i: (k, 0, 0)),
                  pl.BlockSpec((1,) + bfar.shape[1:], lambda bb, k, i: (k, 0, 0))],
        out_specs=[pl.BlockSpec((1, 1, 1, hd, cols), lambda bb, k, i: (bb, k, i, 0, 0)),
                   pl.BlockSpec((1, 1, 1, nsp, qblk), lambda bb, k, i: (bb, k, i, 0, 0))],
        out_shape=[jax.ShapeDtypeStruct((b, N_KV_HEADS, nqb, hd, cols), F32),
                   jax.ShapeDtypeStruct((b, N_KV_HEADS, nqb, nsp, qblk), F32)],
        compiler_params=_cparams("parallel", "parallel", "parallel"),
        name="nsa_cmp_attention",
    )(q2d, kc_pad, vc_pad, vct, ov, ovt, bnear, bfar)


def _slc_win_kernel(q_ref, selt_ref, oct_ref, gate_ref, sk_ref, svt_ref, wk_ref, wvt_ref, bt_ref, bfar_ref, o_ref,
                    m_s, l_s, acc_s, m_w, l_w, acc_w, s_a, s_b, *, qblk):
    qb = pl.program_id(2)
    cols = GQ * qblk
    scale = HEAD_DIM ** -0.5
    qt = _group_q_transposed((q_ref[...] * scale).astype(BF16))
    bfar = bfar_ref[0]
    per = qblk // SEL_BLOCK

    def neg_unless(cond):
        return jnp.where(cond, 0.0, NEG)

    def sel_add(kt, extra):
        pieces = [jnp.broadcast_to(selt_ref[0, 0, 0, pl.ds(per * kt + e, 1), :] + extra, (SEL_BLOCK, qblk))
                  for e in range(per)]
        mk = jnp.concatenate(pieces, axis=0)
        return jnp.concatenate([mk] * GQ, axis=1)

    slc_state = (m_s, l_s, acc_s)
    win_state = (m_w, l_w, acc_w)

    def reset(state):
        m_r, l_r, acc_r = state
        m_r[...] = jnp.full(m_r.shape, NEG, F32)
        l_r[...] = jnp.zeros(l_r.shape, F32)
        acc_r[...] = jnp.zeros(acc_r.shape, F32)

    def update(state, pieces, vts):
        update_cat(state, jnp.concatenate(pieces, axis=0), jnp.concatenate(vts, axis=1))

    def update_cat(state, s, vt):
        m_r, l_r, acc_r = state
        m_prev = m_r[...]
        m_new = jnp.maximum(m_prev, jnp.max(s, axis=0, keepdims=True))
        alpha = jnp.exp(m_prev - m_new)
        p = jnp.exp(s - m_new)
        l_r[...] = alpha * l_r[...] + jnp.sum(p, axis=0, keepdims=True)
        acc_r[...] = alpha * acc_r[...] + _dot(vt, p.astype(BF16))
        m_r[...] = m_new

    def result(state):
        return state[2][...] / state[1][...]

    def logits(k_ref, kt):
        return _dot(k_ref[0, 0, kt], qt)

    reset(slc_state)
    reset(win_state)
    n_far = jnp.maximum(qb - 1, 0)
    last = sk_ref.shape[2] - 1

    n_groups = (n_far + FAR_GROUP - 1) // FAR_GROUP

    def stage_logits(g, buf):
        for e in range(FAR_GROUP):
            kt = g * FAR_GROUP + e
            ktc = jnp.minimum(kt, last)
            buf[e * qblk:(e + 1) * qblk, :] = logits(sk_ref, ktc) + (sel_add(ktc, neg_unless(kt < n_far)) + bfar)

    def consume(g, buf):
        vts = [svt_ref[0, 0, jnp.minimum(g * FAR_GROUP + e, last)] for e in range(FAR_GROUP)]
        update_cat(slc_state, buf[...], jnp.concatenate(vts, axis=1))

    stage_logits(0, s_a)

    def far(j, carry):
        stage_logits(2 * j + 1, s_b)
        consume(2 * j, s_a)
        stage_logits(2 * j + 2, s_a)
        consume(2 * j + 1, s_b)
        return carry

    lax.fori_loop(0, (n_groups + 1) // 2, far, 0)
    kt1 = jnp.maximum(qb - 1, 0)
    near = [logits(sk_ref, kt1) + (sel_add(kt1, neg_unless(qb >= 1)) + bt_ref[0, 1]),
            logits(sk_ref, qb) + (sel_add(qb, 0.0) + bt_ref[0, 0])]

    nwin = WINDOW // qblk
    pieces, vts = [], []
    for u in range(nwin + 1):
        kt = qb - nwin + u
        ktc = jnp.maximum(kt, 0)
        delta = (nwin - u) * qblk
        if delta == 0:
            bias = bt_ref[0, 0]
        elif delta == qblk:
            bias = bt_ref[0, 1] + neg_unless(kt >= 0)
        elif u == 0:
            bias = bt_ref[0, 2] + neg_unless(kt >= 0)
        else:
            bias = bfar + neg_unless(kt >= 0)
        pieces.append(logits(wk_ref, ktc) + bias)
        vts.append(wvt_ref[0, 0, ktc])
    update(slc_state, near, [svt_ref[0, 0, kt1], svt_ref[0, 0, qb]])
    update(win_state, pieces, vts)
    o_s = result(slc_state)
    o_w = result(win_state)

    gt = gate_ref[0, 0, 0]
    ot = gt[0:1] * oct_ref[0, 0, 0] + gt[1:2] * o_s + gt[2:3] * o_w
    o_ref[0, 0, 0] = ot.astype(BF16)


def _slc_win_attention(q2d, b, t, selt, oct, gates_blk, sk, svt, wk, wvt, bt, bfar):
    hd = HEAD_DIM
    qblk = math.gcd(t, Q_BLOCK)
    nqb = t // qblk
    nsp = selt.shape[-2]
    cols = GQ * qblk
    kspec = pl.BlockSpec((1, 1, nqb, qblk, hd), lambda bb, k, i: (bb, k, 0, 0, 0))
    vspec = pl.BlockSpec((1, 1, nqb, hd, qblk), lambda bb, k, i: (bb, k, 0, 0, 0))
    qspec = pl.BlockSpec((qblk, GQ * hd), lambda bb, k, i: (bb * nqb + i, k))
    return pl.pallas_call(
        functools.partial(_slc_win_kernel, qblk=qblk),
        grid=(b, N_KV_HEADS, nqb),
        in_specs=[qspec, pl.BlockSpec((1, 1, 1, nsp, qblk), lambda bb, k, i: (bb, k, i, 0, 0)),
                  pl.BlockSpec((1, 1, 1, hd, cols), lambda bb, k, i: (bb, k, i, 0, 0)),
                  pl.BlockSpec((1, 1, 1, 3, cols), lambda bb, k, i: (bb, k, i, 0, 0)),
                  kspec, vspec, kspec, vspec,
                  pl.BlockSpec((1,) + bt.shape[1:], lambda bb, k, i: (k, 0, 0, 0)),
                  pl.BlockSpec((1,) + bfar.shape[1:], lambda bb, k, i: (k, 0, 0))],
        out_specs=pl.BlockSpec((1, 1, 1, hd, cols), lambda bb, k, i: (bb, k, i, 0, 0)),
        out_shape=jax.ShapeDtypeStruct((b, N_KV_HEADS, nqb, hd, cols), BF16),
        scratch_shapes=[pltpu.VMEM((1, cols), F32), pltpu.VMEM((1, cols), F32), pltpu.VMEM((hd, cols), F32)] * 2
        + [pltpu.VMEM((FAR_GROUP * qblk, cols), F32), pltpu.VMEM((FAR_GROUP * qblk, cols), F32)],
        compiler_params=_cparams("parallel", "parallel", "parallel"),
        name="nsa_slc_win_attention",
    )(q2d, selt, oct, gates_blk, sk, svt, wk, wvt, bt, bfar)


def _cols_table(tab):
    h, nq, nk = tab.shape
    return jnp.transpose(tab.reshape(N_KV_HEADS, GQ, nq, nk), (0, 3, 1, 2)).reshape(N_KV_HEADS, nk, GQ * nq)


def _nsa_prompt(proj, kv, rel_bias, cmp_w1, cmp_b1, cmp_w2, cmp_pos):
    b, t = kv.shape[:2]
    hq = N_HEADS * HEAD_DIM
    gates = proj[:, hq:hq + 3 * N_HEADS]
    qblk = math.gcd(t, Q_BLOCK)
    assert qblk == Q_BLOCK and t % Q_BLOCK == 0 and WINDOW % Q_BLOCK == 0 and WINDOW - qblk + 1 >= REL_MAX_DIST
    nqb = t // qblk
    nch = t // CMP_STRIDE
    n_sel = t // SEL_BLOCK
    nsp = -(-n_sel // LANES) * LANES
    kv_hm = jnp.transpose(kv, (2, 0, 3, 1, 4))
    xc = kv_hm[:2].reshape(2, b * N_KV_HEADS, nch, CMP_STRIDE * HEAD_DIM)
    kvc = _compress(xc, cmp_w1, cmp_b1, cmp_w2, cmp_pos).reshape(2, b, N_KV_HEADS, nch, HEAD_DIM)
    vct = jnp.swapaxes(kvc[1], -1, -2)
    kvc = jnp.pad(kvc, ((0, 0), (0, 0), (0, 0), (CMP_PAD, CMP_PAD), (0, 0)))
    tbl = _bias_by_dist(rel_bias)
    bfar = _cols_table(jnp.broadcast_to(tbl[:, REL_MAX_DIST][:, None, None], (N_HEADS, qblk, 1)))
    bnear = _cols_table(tbl[:, _dist_tile(qblk, CMP_NEAR, CMP_STRIDE * CMP_PAD - CMP_BLOCK + 1, CMP_STRIDE)])
    qi = np.arange(qblk)[:, None]
    kj = np.arange(qblk)[None, :]
    far_tile = jnp.broadcast_to(tbl[:, REL_MAX_DIST][:, None, None], (N_HEADS, qblk, qblk))
    bt = jnp.stack([_cols_table(jnp.where(kj <= qi, tbl[:, _dist_tile(qblk, qblk, 0)], NEG)),
                    _cols_table(tbl[:, _dist_tile(qblk, qblk, qblk)]),
                    _cols_table(jnp.where(kj >= qi, far_tile, NEG))], axis=1)
    ov_np = _overlap_matrix(nch, nsp)
    ov = jnp.asarray(ov_np, BF16)
    ovt = jnp.asarray(ov_np[CMP_PAD:CMP_PAD + nch].T, BF16)
    gates_blk = jnp.transpose(gates.reshape(b, nqb, qblk, N_KV_HEADS, GQ, 3), (0, 3, 1, 5, 4, 2))
    gates_blk = gates_blk.reshape(b, N_KV_HEADS, nqb, 3, GQ * qblk)
    oct, selt = _cmp_attention(proj, b, t, kvc[0], kvc[1], vct, ov, ovt, bnear, bfar, nch, min(SEL_TOP, n_sel))
    kvb = kv_hm[2:].astype(BF16).reshape(4, b, N_KV_HEADS, nqb, qblk, HEAD_DIM)
    sk, wk = kvb[0], kvb[2]
    svt, wvt = jnp.swapaxes(kvb[1], -1, -2), jnp.swapaxes(kvb[3], -1, -2)
    o_t = _slc_win_attention(proj, b, t, selt, oct, gates_blk, sk, svt, wk, wvt, bt, bfar)
    o_t = o_t.reshape(b, N_KV_HEADS, nqb, HEAD_DIM, GQ, qblk)
    return jnp.transpose(o_t, (0, 2, 5, 1, 4, 3)).reshape(b, t, N_HEADS * HEAD_DIM)


CMP_TAP_PAIRS = CMP_STRIDE // 2
PAGE_CHUNKS = PAGE_SIZE // CMP_STRIDE


def _paged_compress_kernel(pt_ref, cache_ref, wp_ref, w1_ref, pos_ref, b1_ref, w2_ref, o_ref, buft, buf, carry, sem,
                           *, pg, npages, hid):
    g = pl.program_id(1)
    step = pl.program_id(0) * pl.num_programs(1) + g
    total = pl.num_programs(0) * pl.num_programs(1)
    m = pg * PAGE_CHUNKS
    hk = N_KV_HEADS * HEAD_DIM
    pairs = hk // LANES

    def page_copy(s, p, slot):
        return pltpu.make_async_copy(cache_ref.at[pt_ref[s * pg + p], pl.ds(0, 2)], buft.at[slot, p], sem.at[slot])

    @pl.when(step == 0)
    def _():
        for p in range(pg):
            page_copy(0, p, 0).start()

    for slot in range(2):
        @pl.when(jnp.logical_and(step + 1 < total, (step + 1) % 2 == slot))
        def _():
            for p in range(pg):
                page_copy(step + 1, p, slot).start()

    @pl.when(g == 0)
    def _():
        carry[...] = jnp.zeros_like(carry)

    for slot in range(2):
        @pl.when(step % 2 == slot)
        def _():
            for p in range(pg):
                page_copy(step, p, slot).wait()

    cur = step % 2

    def to_rows(p, carry_):
        r0 = pl.multiple_of(p * PAGE_SIZE, PAGE_SIZE)
        for comp in range(2):
            for hp in range(pairs):
                plane = buft[cur, p, comp, hp * LANES:(hp + 1) * LANES, :]
                buf[comp * pairs + hp, pl.ds(r0, PAGE_SIZE), :] = plane.T
        return carry_

    lax.fori_loop(0, pg, to_rows, 0)

    row0 = lax.broadcasted_iota(I32, (m, hid), 0) == 0
    for comp in range(2):
        pt = _dot(pos_ref[comp].astype(BF16), w1_ref[comp])
        for hp in range(N_KV_HEADS // 2):
            lg = (comp * hk + hp * 2 * HEAD_DIM) // LANES
            acc = jnp.zeros((m, 2 * CMP_RATIO * hid), F32)
            for u in range(CMP_TAP_PAIRS):
                taps = [buf[lg, pl.ds(2 * u + e, m, stride=CMP_STRIDE), :] for e in range(2)]
                acc = acc + _dot(jnp.concatenate(taps, axis=1).astype(BF16), wp_ref[comp, u])
            for e in range(2):
                kvh = 2 * hp + e
                base = e * CMP_RATIO * hid
                p0 = acc[:, base:base + hid] + pt[0:1, :hid]
                p1 = acc[:, base + hid:base + 2 * hid] + pt[1:2, hid:]
                prev0 = jnp.where(row0, carry[comp, kvh], pltpu.roll(p0, 1, axis=0))
                carry[comp, kvh] = p0[m - 1:m, :]
                pre = prev0 + p1 + b1_ref[comp]
                o_ref[comp, 0, kvh] = _dot(jax.nn.gelu(pre).astype(BF16), w2_ref[comp])


def _paged_compress(page_table, cache_t, cmp_w1, cmp_b1, cmp_w2, cmp_pos):
    bs, npages = page_table.shape
    hid = cmp_w1.shape[-1]
    kdim = CMP_STRIDE * HEAD_DIM
    pg = math.gcd(npages, 32)
    m = pg * PAGE_CHUNKS
    w1 = cmp_w1.reshape(2, CMP_RATIO, kdim, hid)
    w1cat = jnp.concatenate([w1[:, r] for r in range(CMP_RATIO)], axis=-1)
    n1 = CMP_RATIO * hid
    wt = w1cat.reshape(2, CMP_TAP_PAIRS, 2, HEAD_DIM, n1)
    zero = jnp.zeros_like(wt)
    wp = jnp.concatenate([jnp.concatenate([wt, zero], axis=-1), jnp.concatenate([zero, wt], axis=-1)], axis=3)
    wp = wp.reshape(2, CMP_TAP_PAIRS, 4 * HEAD_DIM, 2 * n1).astype(BF16)
    pos = jnp.pad(cmp_pos.reshape(2, CMP_RATIO, kdim), ((0, 0), (0, SUBLANES - CMP_RATIO), (0, 0)))
    nch = npages * PAGE_CHUNKS
    grid_spec = pltpu.PrefetchScalarGridSpec(
        num_scalar_prefetch=1,
        grid=(bs, npages // pg),
        in_specs=[pl.BlockSpec(memory_space=pl.ANY),
                  pl.BlockSpec(wp.shape, lambda b, g, pt: (0, 0, 0, 0)),
                  pl.BlockSpec((2, kdim, n1), lambda b, g, pt: (0, 0, 0)),
                  pl.BlockSpec((2, SUBLANES, kdim), lambda b, g, pt: (0, 0, 0)),
                  pl.BlockSpec((2, 1, hid), lambda b, g, pt: (0, 0, 0)),
                  pl.BlockSpec((2, hid, HEAD_DIM), lambda b, g, pt: (0, 0, 0))],
        out_specs=pl.BlockSpec((2, 1, N_KV_HEADS, m, HEAD_DIM), lambda b, g, pt: (0, b, 0, g, 0)),
        scratch_shapes=[pltpu.VMEM((2, pg, 2, N_KV_HEADS * HEAD_DIM, PAGE_SIZE), F32),
                        pltpu.VMEM((2 * N_KV_HEADS * HEAD_DIM // LANES, pg * PAGE_SIZE, LANES), F32),
                        pltpu.VMEM((2, N_KV_HEADS, 1, hid), F32), pltpu.SemaphoreType.DMA((2,))],
    )
    return pl.pallas_call(
        functools.partial(_paged_compress_kernel, pg=pg, npages=npages, hid=hid),
        grid_spec=grid_spec,
        out_shape=jax.ShapeDtypeStruct((2, bs, N_KV_HEADS, nch, HEAD_DIM), F32),
        compiler_params=_cparams("arbitrary", "arbitrary"),
        name="nsa_paged_compress",
    )(page_table.reshape(-1), cache_t, wp, w1cat.astype(BF16), pos, cmp_b1.reshape(2, 1, hid), cmp_w2.astype(BF16))


def _rows8(x):
    return jnp.concatenate([x, jnp.zeros((SUBLANES - x.shape[0], x.shape[1]), x.dtype)], axis=0)


def _sample_cmp_kernel(q_ref, kc_ref, vc_ref, ov_ref, bias_ref, oc_ref, idx_ref, *, nch, n_sel, n_past_blk, ntop):
    scale = HEAD_DIM ** -0.5
    nsp = ov_ref.shape[1]
    row = lax.broadcasted_iota(I32, (SUBLANES, nch), 0)
    lane = lax.broadcasted_iota(I32, (SUBLANES, nch), 1)
    mask = jnp.logical_and(lane >= 1, row < GQ)
    imp = jnp.zeros((SUBLANES, nsp), F32)
    row_s = lax.broadcasted_iota(I32, (SUBLANES, nsp), 0)
    for kvh in range(N_KV_HEADS):
        q8 = _rows8(q_ref[0, kvh * GQ:(kvh + 1) * GQ, :])
        s = jnp.where(mask, _dot3_nt(q8, kc_ref[0, kvh]) * scale + bias_ref[kvh], NEG)
        mx = jnp.max(s, axis=-1, keepdims=True)
        p = jnp.where(mask, jnp.exp(s - mx), 0.0)
        l = jnp.sum(p, axis=-1, keepdims=True)
        linv = 1.0 / jnp.where(l > 0, l, 1.0)
        o = _dot(p.astype(BF16), vc_ref[0, kvh].astype(BF16)) * linv
        oc_ref[0, kvh * GQ:(kvh + 1) * GQ, :] = o[0:GQ]
        p_hi, p_lo = _split(p * linv)
        contrib = _dot(p_hi, ov_ref[...]) + _dot(p_lo, ov_ref[...])
        imp = jnp.where(row_s == kvh, jnp.sum(contrib, axis=0, keepdims=True), imp)
    s_io = lax.broadcasted_iota(I32, (SUBLANES, nsp), 1)
    t = n_past_blk * SEL_BLOCK
    cur = t // SEL_BLOCK
    forced = jnp.logical_or(s_io == 0, jnp.logical_or(s_io == cur, s_io == cur - 1))
    valid = jnp.logical_and(s_io * SEL_BLOCK <= t, s_io < n_sel)
    score = jnp.where(valid, imp + jnp.where(forced, FORCE_SCORE, 0.0), -jnp.inf)
    lane_f = s_io.astype(F32)
    out_lane = lax.broadcasted_iota(I32, (SUBLANES, LANES), 1)
    picks = jnp.zeros((SUBLANES, LANES), F32)
    for it in range(ntop):
        mx = jnp.max(score, axis=-1, keepdims=True)
        idx = jnp.min(jnp.where(score == mx, lane_f, float(nsp)), axis=-1, keepdims=True)
        picks = jnp.where(out_lane == it, idx, picks)
        score = jnp.where(lane_f == idx, -jnp.inf, score)
    idx_ref[0] = picks.astype(I32)


def _sample_cmp_attention(q_hm, kc, vc, ov, bias, n_sel, n_past_blk, ntop):
    bs = q_hm.shape[0]
    nch = kc.shape[2]
    kvspec = pl.BlockSpec((1, N_KV_HEADS, nch, HEAD_DIM), lambda b: (b, 0, 0, 0))
    return pl.pallas_call(
        functools.partial(_sample_cmp_kernel, nch=nch, n_sel=n_sel, n_past_blk=n_past_blk, ntop=ntop),
        grid=(bs,),
        in_specs=[pl.BlockSpec((1, N_HEADS, HEAD_DIM), lambda b: (b, 0, 0)), kvspec, kvspec,
                  pl.BlockSpec(ov.shape, lambda b: (0, 0)), pl.BlockSpec(bias.shape, lambda b: (0, 0, 0))],
        out_specs=[pl.BlockSpec((1, N_HEADS, HEAD_DIM), lambda b: (b, 0, 0)),
                   pl.BlockSpec((1, SUBLANES, LANES), lambda b: (b, 0, 0))],
        out_shape=[jax.ShapeDtypeStruct(q_hm.shape, F32), jax.ShapeDtypeStruct((bs, SUBLANES, LANES), I32)],
        compiler_params=_cparams("parallel"),
        name="nsa_sample_cmp_attention",
    )(q_hm, kc, vc, ov, bias)


def _sample_slc_win_kernel(page_ref, half_ref, new_ref, d0_ref, var_ref,
                           q_ref, oc_ref, gate_ref, cache_ref, newt_ref, wint_ref, winnew_ref,
                           bslc_ref, bwin_ref, bwin_new_ref, o_ref, buf, sem, *, ntop):
    b = pl.program_id(0)
    scale = HEAD_DIM ** -0.5
    hk = N_KV_HEADS * HEAD_DIM
    nslots = N_KV_HEADS * ntop

    def plane_copies(slot):
        i = b * nslots + slot
        kvh = slot // ntop
        return [pltpu.make_async_copy(cache_ref.at[page_ref[i], 2 + e, kvh], buf.at[slot, e], sem) for e in range(2)]

    for slot in range(nslots):
        is_new = new_ref[b * nslots + slot] == 1

        @pl.when(jnp.logical_not(is_new))
        def _():
            for cp in plane_copies(slot):
                cp.start()

        @pl.when(is_new)
        def _():
            buf[slot] = newt_ref[0, :, slot // ntop]

    for slot in range(nslots):
        @pl.when(new_ref[b * nslots + slot] == 0)
        def _():
            for cp in plane_copies(slot):
                cp.wait()

    lane = lax.broadcasted_iota(I32, (SUBLANES, PAGE_SIZE), 1)
    gt = gate_ref[0]
    for kvh in range(N_KV_HEADS):
        hs = slice(kvh * GQ, (kvh + 1) * GQ)
        ks = slice(kvh * HEAD_DIM, (kvh + 1) * HEAD_DIM)
        vsl = slice(hk + kvh * HEAD_DIM, hk + (kvh + 1) * HEAD_DIM)
        q8f = _rows8(q_ref[0, hs, :])
        q8 = q8f.astype(BF16)
        ss, masks = [], []
        for j in range(ntop):
            slot = kvh * ntop + j
            i = b * nslots + slot
            s = _dot(q8, buf[slot, 0].astype(BF16)) * scale + bslc_ref[kvh, var_ref[i]]
            off = lane - half_ref[i] * SEL_BLOCK
            mask = jnp.logical_and(jnp.logical_and(off >= 0, off < SEL_BLOCK), off <= d0_ref[i])
            ss.append(jnp.where(mask, s, NEG))
            masks.append(mask)
        mx = ss[0].max(axis=-1, keepdims=True)
        for s in ss[1:]:
            mx = jnp.maximum(mx, s.max(axis=-1, keepdims=True))
        l = jnp.zeros((SUBLANES, 1), F32)
        acc = jnp.zeros((SUBLANES, HEAD_DIM), F32)
        for j in range(ntop):
            p = jnp.where(masks[j], jnp.exp(ss[j] - mx), 0.0)
            l = l + jnp.sum(p, axis=-1, keepdims=True)
            acc = acc + _dot_nt(p.astype(BF16), buf[kvh * ntop + j, 1].astype(BF16))
        o_s = acc / jnp.where(l > 0, l, 1.0)
        sw = _dot(q8, wint_ref[0, 0, kvh].astype(BF16)) * scale + bwin_ref[kvh]
        s_new = jnp.sum(q8f * winnew_ref[0, 0:1, ks], axis=-1, keepdims=True) * scale + bwin_new_ref[kvh]
        mw = jnp.maximum(sw.max(axis=-1, keepdims=True), s_new)
        pw = jnp.exp(sw - mw)
        p_new = jnp.exp(s_new - mw)
        lw = jnp.sum(pw, axis=-1, keepdims=True) + p_new
        o_w = (_dot_nt(pw.astype(BF16), wint_ref[0, 1, kvh].astype(BF16)) + p_new * winnew_ref[0, 0:1, vsl]) / lw
        g3 = gt[hs, :]
        o_ref[0, hs, :] = g3[:, 0:1] * oc_ref[0, hs, :] + g3[:, 1:2] * o_s[0:GQ] + g3[:, 2:3] * o_w[0:GQ]


def _sample_slc_win(picks, q_hm, oc, gates_h, cache_t, newt, wint, winnew, bslc, bwin, bwin_new, ntop):
    bs = q_hm.shape[0]
    hspec = pl.BlockSpec((1, N_HEADS, HEAD_DIM), lambda b, *_: (b, 0, 0))
    grid_spec = pltpu.PrefetchScalarGridSpec(
        num_scalar_prefetch=5,
        grid=(bs,),
        in_specs=[hspec, hspec, pl.BlockSpec((1, N_HEADS, 3), lambda b, *_: (b, 0, 0)),
                  pl.BlockSpec(memory_space=pl.ANY),
                  pl.BlockSpec((1,) + newt.shape[1:], lambda b, *_: (b, 0, 0, 0, 0)),
                  pl.BlockSpec((1,) + wint.shape[1:], lambda b, *_: (b, 0, 0, 0, 0)),
                  pl.BlockSpec((1,) + winnew.shape[1:], lambda b, *_: (b, 0, 0)),
                  pl.BlockSpec(bslc.shape, lambda b, *_: (0, 0, 0, 0)),
                  pl.BlockSpec(bwin.shape, lambda b, *_: (0, 0, 0)),
                  pl.BlockSpec(bwin_new.shape, lambda b, *_: (0, 0, 0))],
        out_specs=hspec,
        scratch_shapes=[pltpu.VMEM((N_KV_HEADS * ntop, 2, HEAD_DIM, PAGE_SIZE), F32), pltpu.SemaphoreType.DMA(())],
    )
    return pl.pallas_call(
        functools.partial(_sample_slc_win_kernel, ntop=ntop),
        grid_spec=grid_spec,
        out_shape=jax.ShapeDtypeStruct(q_hm.shape, F32),
        compiler_params=_cparams("arbitrary"),
        name="nsa_sample_slc_win",
    )(*picks, q_hm, oc, gates_h, cache_t, newt, wint, winnew, bslc, bwin, bwin_new)


def _nsa_sample(q, gates, kv_new, win_buf, cache, page_table, rel_bias, cmp_w1, cmp_b1, cmp_w2, cmp_pos):
    bs, npages = page_table.shape
    pool = cache.shape[0]
    hk = N_KV_HEADS * HEAD_DIM
    past = npages * PAGE_SIZE
    nch = past // CMP_STRIDE
    n_past_blk = past // SEL_BLOCK
    n_sel = n_past_blk + 1
    ntop = min(SEL_TOP, n_sel)
    nsp = -(-n_sel // LANES) * LANES
    wb = win_buf.shape[1]
    assert n_sel >= SEL_TOP and past >= wb
    cache_t = jnp.transpose(cache, (0, 2, 3, 4, 1))
    kvc = _paged_compress(page_table, cache_t.reshape(pool, 4, hk, PAGE_SIZE), cmp_w1, cmp_b1, cmp_w2, cmp_pos)
    tbl = _bias_by_dist(rel_bias)
    tok = np.arange(nch) - 1
    dist_c = np.clip(past - (tok * CMP_STRIDE + CMP_BLOCK - 1), 0, REL_MAX_DIST)
    bias_c = jnp.pad(tbl[:, dist_c].reshape(N_KV_HEADS, GQ, nch), ((0, 0), (0, SUBLANES - GQ), (0, 0)))
    ov = np.zeros((nch, nsp), np.float32)
    ov[1:] = _overlap_matrix(nch - 1, nsp)[CMP_PAD:-CMP_PAD]
    q_hm = q.reshape(bs, N_HEADS, HEAD_DIM)
    oc, idx = _sample_cmp_attention(q_hm, kvc[0], kvc[1], jnp.asarray(ov, BF16), bias_c, n_sel, n_past_blk, ntop)
    idx = idx[:, :N_KV_HEADS, :ntop]
    ip = jnp.minimum(idx, n_past_blk - 1)
    per_page = PAGE_SIZE // SEL_BLOCK
    page = jnp.take_along_axis(page_table, (ip // per_page).reshape(bs, -1), axis=1)
    is_new = (idx >= n_past_blk).astype(I32)
    half = jnp.where(is_new == 1, 0, ip % per_page)
    d0 = (n_past_blk - idx) * SEL_BLOCK
    case = jnp.clip(n_past_blk - idx, 0, 3)
    picks = [a.reshape(-1).astype(I32) for a in (page, half, is_new, d0, case * per_page + half)]
    lanes = np.arange(PAGE_SIZE)
    dist_s = np.stack([np.clip(cs * SEL_BLOCK - (lanes - hf * SEL_BLOCK), 0, REL_MAX_DIST)
                       for cs in range(4) for hf in range(per_page)])
    nvar = dist_s.shape[0]
    bslc = jnp.pad(jnp.transpose(tbl[:, dist_s].reshape(N_KV_HEADS, GQ, nvar, PAGE_SIZE), (0, 2, 1, 3)),
                   ((0, 0), (0, 0), (0, SUBLANES - GQ), (0, 0)))
    dist_w = np.clip(wb - np.arange(wb), 0, REL_MAX_DIST)
    bwin = jnp.pad(tbl[:, dist_w].reshape(N_KV_HEADS, GQ, wb), ((0, 0), (0, SUBLANES - GQ), (0, 0)))
    bwin_new = jnp.pad(tbl[:, 0].reshape(N_KV_HEADS, GQ, 1), ((0, 0), (0, SUBLANES - GQ), (0, 0)))
    newt = jnp.pad(kv_new[:, 2:4][..., None], ((0, 0),) * 4 + ((0, PAGE_SIZE - 1),))
    winnew = jnp.pad(kv_new[:, 4:6].reshape(bs, 1, 2 * hk), ((0, 0), (0, SUBLANES - 1), (0, 0)))
    wint = jnp.transpose(win_buf, (0, 2, 3, 4, 1))
    o = _sample_slc_win(picks, q_hm, oc, gates.reshape(bs, N_HEADS, 3), cache_t, newt, wint, winnew,
                        bslc, bwin, bwin_new, ntop)
    return o.reshape(bs, N_HEADS * HEAD_DIM)


def _mods(mod_l, nb, mr):
    return [m.reshape(nb, mr, -1) for m in jnp.split(mod_l, 6, axis=-1)]


def _s5_layer(x, nb, rb, seq, norm_g, mods, s5w, dsk, w_glu_bf16, h0r, h0i):
    sh1, sc1, g1 = mods[0], mods[1], mods[2]
    d = x.shape[-1]
    ar, ai, wbr, wbi, wcr, wci = s5w
    act, st_r, st_i = _s5_scan(x.reshape(nb, rb, d), norm_g.reshape(1, d), sh1, sc1, dsk.reshape(1, d),
                               wbr, wbi, ar, ai, wcr, wci, h0r, h0i, seq=seq)
    return _glu_residual(act.reshape(-1, d), x, g1, w_glu_bf16, rb), st_r, st_i


def kernel(x_prompt, x_sample, c_prompt, c_sample, state_s5, cache_nsa_kv, cache_win_kv, page_table, w_mod, b_mod, norm_g, s5_lambda_re, s5_lambda_im, s5_log_dt, s5_b_re, s5_b_im, s5_c_re, s5_c_im, s5_d, s5_w_glu, kv_mod_w, kv_mod_b, kv_norm_g, w_kv, cmp_w1, cmp_b1, cmp_w2, cmp_pos, nsa_w_in, nsa_w_out, rel_bias, moe_w_router, moe_b_router, moe_w_gu, moe_b_gu, moe_w_down, moe_b_down, final_norm_g):
    bp, lp, d = x_prompt.shape
    bs, ls, _ = x_sample.shape
    assert ls == 1, "the sample path advances exactly one token per sequence"
    depth = w_mod.shape[0]
    n_a = s5_lambda_re.shape[0]
    hq = N_HEADS * HEAD_DIM
    groups = d // S5_GROUP

    c_all = jnp.concatenate([c_prompt, c_sample], axis=0)
    mod_all = _adaln(c_all, w_mod, b_mod)
    kvmod_all = _adaln(c_all, kv_mod_w[None], kv_mod_b[None])[0]
    s5ws = [_s5_weights(s5_lambda_re[i], s5_lambda_im[i], s5_log_dt[i], s5_b_re[i], s5_b_im[i],
                        s5_c_re[i], s5_c_im[i]) for i in range(n_a)]
    wglus = [s5_w_glu[i].astype(BF16) for i in range(n_a)]
    w_kv_b = w_kv.astype(BF16)
    n_in = nsa_w_in.shape[-1]
    n_in_pad = -(-n_in // LANES) * LANES
    w_ins = [jnp.pad(nsa_w_in[j], ((0, 0), (0, n_in_pad - n_in))).astype(BF16) for j in range(depth - n_a)]
    w_outs = [nsa_w_out[j].astype(BF16) for j in range(depth - n_a)]

    def run(x3, sl, seq, h0_all, mixer):
        if seq:
            nb, rb = x3.shape[0], x3.shape[1]
            mr = 1
        else:
            nb, rb = 1, x3.shape[0]
            mr = rb
        x = x3.reshape(-1, d)
        states = []
        kv = None
        for i in range(depth):
            mods = _mods(mod_all[i, sl], nb, mr)
            if i < n_a:
                ns = s5ws[i][0].shape[-1]
                h0r = h0_all[i][..., 0].reshape(nb, -1, ns)
                h0i = h0_all[i][..., 1].reshape(nb, -1, ns)
                x, sr, si = _s5_layer(x, nb, rb, seq, norm_g[i, 0], mods, s5ws[i], s5_d[i], wglus[i], h0r, h0i)
                states.append(jnp.stack([sr.reshape(-1, groups, S5_STATE), si.reshape(-1, groups, S5_STATE)], axis=-1))
            else:
                j = i - n_a
                if kv is None:
                    kvsh, kvsc = [m.reshape(nb, mr, d) for m in jnp.split(kvmod_all[sl], 2, axis=-1)]
                    kv = _normmod_matmul(x, rb, kv_norm_g.reshape(1, d), kvsh, kvsc, w_kv_b)
                proj = _normmod_matmul(x, rb, norm_g[i, 0].reshape(1, d), mods[0], mods[1], w_ins[j], sig_from=hq)
                o = mixer(proj, kv)
                x = _matmul_residual(o, x, mods[2], w_outs[j], rb)
            x = _moe(x, rb, norm_g[i, 1].reshape(1, d), mods[3], mods[4], mods[5], i, moe_w_router[i], moe_b_router[i],
                     moe_w_gu, moe_b_gu, moe_w_down, moe_b_down)
        return _final_norm(x, final_norm_g, rb), jnp.stack(states), kv

    def prompt_mixer(proj, kv):
        o = _nsa_prompt(proj, kv.reshape(bp, lp, N_BRANCH_KV, N_KV_HEADS, HEAD_DIM), rel_bias,
                        cmp_w1, cmp_b1, cmp_w2, cmp_pos)
        return o.reshape(bp * lp, hq)

    def sample_mixer(proj, kv):
        return _nsa_sample(proj[:, :hq], proj[:, hq:n_in], kv.reshape(bs, N_BRANCH_KV, N_KV_HEADS, HEAD_DIM),
                           cache_win_kv, cache_nsa_kv, page_table, rel_bias, cmp_w1, cmp_b1, cmp_w2, cmp_pos)

    h0_p = jnp.zeros((n_a, bp, groups, S5_STATE, 2), F32)
    y_p, st_p, kv_p = run(x_prompt, slice(0, bp), True, h0_p, prompt_mixer)
    y_s, st_s, kv_s = run(x_sample.reshape(bs, d), slice(bp, bp + bs), False, state_s5, sample_mixer)
    kv_p = kv_p.reshape(bp, lp, N_BRANCH_KV, N_KV_HEADS, HEAD_DIM)
    kv_s = kv_s.reshape(bs, ls, N_BRANCH_KV, N_KV_HEADS, HEAD_DIM)
    win_p = kv_p[:, -min(WINDOW, lp):, 4:6]
    win_s = jnp.concatenate([cache_win_kv, kv_s[:, :, 4:6]], axis=1)[:, -cache_win_kv.shape[1]:]
    return (y_p.reshape(bp, lp, d), y_s.reshape(bs, ls, d), st_p, st_s, kv_p[:, :, :4], kv_s[:, :, :4], win_p, win_s)
```

```python
import functools
import math

import numpy as np
import jax
import jax.numpy as jnp
from jax import lax
from jax.experimental import pallas as pl
from jax.experimental.pallas import tpu as pltpu

F32 = jnp.float32
BF16 = jnp.bfloat16
I32 = jnp.int32

S5_GROUP = 16
S5_STATE = 64
N_HEADS = 16
HEAD_DIM = 64
N_KV_HEADS = 4
GQ = N_HEADS // N_KV_HEADS
CMP_BLOCK = 32
CMP_STRIDE = 16
CMP_RATIO = CMP_BLOCK // CMP_STRIDE
SEL_BLOCK = 64
SEL_TOP = 16
WINDOW = 512
Q_BLOCK = 128
FORCE_SCORE = 1e4
N_BRANCH_KV = 6
REL_BUCKETS = 32
REL_MAX_DIST = 128
N_EXPERTS = 32
TOP_K = 4
SWIGLU_LIMIT = 7.0
SWIGLU_ALPHA = 1.702
RMS_EPS = 1e-5
PAGE_SIZE = 128

LANES = 128
SUBLANES = 8
VMEM_LIMIT = 56 * 1024 * 1024

MOE_ROWS = 256
FAR_GROUP = 4
S5_LANES = 256
S5_STATES = S5_LANES // S5_GROUP * S5_STATE
NEG = -1e30


def _cparams(*sem):
    return pltpu.CompilerParams(dimension_semantics=sem, vmem_limit_bytes=VMEM_LIMIT)


def _normmod(x, g, sh, sc):
    y = x * lax.rsqrt(jnp.mean(x * x, axis=-1, keepdims=True) + RMS_EPS)
    return (y * g) * (1.0 + sc) + sh


def _dot(a, b):
    return jnp.dot(a, b, preferred_element_type=F32)


def _dot_nt(a, b):
    return lax.dot_general(a, b, (((1,), (1,)), ((), ())), preferred_element_type=F32)


def _split(x):
    hi = x.astype(BF16)
    lo = (x - hi.astype(F32)).astype(BF16)
    return hi, lo


def _mod_spec(m, rb, tm):
    nb, mr, d = m.shape
    if mr == 1:
        return pl.BlockSpec((1, 1, d), lambda i, *_: ((i * tm) // rb, 0, 0))
    assert mr == rb and rb % tm == 0
    per = rb // tm
    return pl.BlockSpec((1, tm, d), lambda i, *_: (i // per, i % per, 0))


def _row_tile(rb, want):
    t = math.gcd(rb, want)
    return t


def _adaln_kernel(c_ref, w_ref, b_ref, o_ref):
    c = c_ref[...]
    a = (c * jax.nn.sigmoid(c)).astype(BF16)
    o_ref[0] = _dot(a, w_ref[0].astype(BF16)) + b_ref[0]


def _adaln(c, w, b):
    n, d = c.shape
    nl, _, nn = w.shape
    npad = -(-n // SUBLANES) * SUBLANES
    cp = jnp.pad(c, ((0, npad - n), (0, 0)))
    tn = math.gcd(nn, 1024)
    out = pl.pallas_call(
        _adaln_kernel,
        grid=(nl, nn // tn),
        in_specs=[pl.BlockSpec((npad, d), lambda l, j: (0, 0)),
                  pl.BlockSpec((1, d, tn), lambda l, j: (l, 0, j)),
                  pl.BlockSpec((1, 1, tn), lambda l, j: (l, 0, j))],
        out_specs=pl.BlockSpec((1, npad, tn), lambda l, j: (l, 0, j)),
        out_shape=jax.ShapeDtypeStruct((nl, npad, nn), F32),
        compiler_params=_cparams("parallel", "parallel"),
        name="adaln",
    )(cp, w, b.reshape(nl, 1, nn))
    return out[:, :n]


def _s5_disc_kernel(lr_ref, li_ref, ldt_ref, br_ref, bi_ref, ar_ref, ai_ref, bbr_ref, bbi_ref):
    lr, li = lr_ref[...], li_ref[...]
    dt = jnp.exp(ldt_ref[...])
    mag = jnp.exp(dt * lr)
    ar, ai = mag * jnp.cos(dt * li), mag * jnp.sin(dt * li)
    den = lr * lr + li * li
    nr, ni = ar - 1.0, ai
    fr = (nr * lr + ni * li) / den
    fi = (ni * lr - nr * li) / den
    br, bi = br_ref[...], bi_ref[...]
    ar_ref[...] = ar
    ai_ref[...] = ai
    bbr_ref[...] = fr * br - fi * bi
    bbi_ref[...] = fr * bi + fi * br


def _s5_discretize(lam_re, lam_im, log_dt, b_re, b_im):
    g, p = lam_re.shape
    c = b_re.shape[-1]
    full3 = lambda s: pl.BlockSpec(s, lambda: (0, 0, 0))
    return pl.pallas_call(
        _s5_disc_kernel,
        in_specs=[full3((g, p, 1)), full3((g, p, 1)), full3((g, 1, 1)), full3((g, p, c)), full3((g, p, c))],
        out_specs=[full3((g, p, 1)), full3((g, p, 1)), full3((g, p, c)), full3((g, p, c))],
        out_shape=[jax.ShapeDtypeStruct((g, p, 1), F32)] * 2 + [jax.ShapeDtypeStruct((g, p, c), F32)] * 2,
        name="s5_discretize",
    )(lam_re.reshape(g, p, 1), lam_im.reshape(g, p, 1), log_dt.reshape(g, 1, 1), b_re, b_im)


def _s5_kernel(x_ref, g_ref, sh_ref, sc_ref, d_ref, wbr_ref, wbi_ref, ar_ref, ai_ref, wcr_ref, wci_ref,
               h0r_ref, h0i_ref, y_ref, str_ref, sti_ref, bur, bui, sre, sim, car_r, car_i, *, seq, rows, nj):
    c = pl.program_id(1)
    h = _normmod(x_ref[0], g_ref[...], sh_ref[0], sc_ref[0])
    dsk = d_ref[...]
    for j in range(nj):
        lo, hi = j * S5_LANES, (j + 1) * S5_LANES
        slo, shi = j * S5_STATES, (j + 1) * S5_STATES
        hj = h[:, lo:hi]
        hb = hj.astype(BF16)
        ar, ai = ar_ref[:, slo:shi], ai_ref[:, slo:shi]
        if seq:
            @pl.when(c == 0)
            def _():
                car_r[j] = h0r_ref[0, :, slo:shi]
                car_i[j] = h0i_ref[0, :, slo:shi]
            bur[...] = _dot(hb, wbr_ref[j])
            bui[...] = _dot(hb, wbi_ref[j])

            def step(t, carry):
                sr, si = carry
                br = bur[pl.ds(t, 1), :]
                bi = bui[pl.ds(t, 1), :]
                nr = ar * sr - ai * si + br
                ni = ar * si + ai * sr + bi
                sre[pl.ds(t, 1), :] = nr
                sim[pl.ds(t, 1), :] = ni
                return nr, ni

            sr, si = lax.fori_loop(0, rows, step, (car_r[j], car_i[j]), unroll=8)
            car_r[j] = sr
            car_i[j] = si
            str_ref[0, :, slo:shi] = sr
            sti_ref[0, :, slo:shi] = si
            s_r, s_i = sre[...], sim[...]
        else:
            h0r, h0i = h0r_ref[0, :, slo:shi], h0i_ref[0, :, slo:shi]
            s_r = ar * h0r - ai * h0i + _dot(hb, wbr_ref[j])
            s_i = ar * h0i + ai * h0r + _dot(hb, wbi_ref[j])
            str_ref[0, :, slo:shi] = s_r
            sti_ref[0, :, slo:shi] = s_i
        y = _dot(s_r.astype(BF16), wcr_ref[j]) - _dot(s_i.astype(BF16), wci_ref[j])
        y_ref[0, :, lo:hi] = jax.nn.gelu(y + dsk[:, lo:hi] * hj)


def _s5_scan(x3, g, sh, sc, dsk, wbr, wbi, ar, ai, wcr, wci, h0r, h0i, *, seq):
    nb, rb, d = x3.shape
    ns = ar.shape[-1]
    nj = d // S5_LANES
    rows = _row_tile(rb, 256)
    hr = 1 if seq else rows
    mrow = sh.shape[1]
    mspec = (pl.BlockSpec((1, 1, d), lambda b, c: (b, 0, 0)) if mrow == 1
             else pl.BlockSpec((1, rows, d), lambda b, c: (b, c, 0)))
    hspec = (pl.BlockSpec((1, 1, ns), lambda b, c: (b, 0, 0)) if seq
             else pl.BlockSpec((1, rows, ns), lambda b, c: (b, c, 0)))
    const2 = lambda s: pl.BlockSpec(s, lambda b, c: (0, 0))
    const3 = lambda s: pl.BlockSpec(s, lambda b, c: (0, 0, 0))
    kern = functools.partial(_s5_kernel, seq=seq, rows=rows, nj=nj)
    return pl.pallas_call(
        kern,
        grid=(nb, rb // rows),
        in_specs=[pl.BlockSpec((1, rows, d), lambda b, c: (b, c, 0)), const2((1, d)), mspec, mspec, const2((1, d)),
                  const3(wbr.shape), const3(wbi.shape), const2((1, ns)), const2((1, ns)),
                  const3(wcr.shape), const3(wci.shape), hspec, hspec],
        out_specs=[pl.BlockSpec((1, rows, d), lambda b, c: (b, c, 0)), hspec, hspec],
        out_shape=[jax.ShapeDtypeStruct((nb, rb, d), F32),
                   jax.ShapeDtypeStruct(h0r.shape, F32), jax.ShapeDtypeStruct(h0i.shape, F32)],
        scratch_shapes=[pltpu.VMEM((rows, S5_STATES), F32)] * 4 + [pltpu.VMEM((nj, 1, S5_STATES), F32)] * 2,
        compiler_params=_cparams("parallel", "arbitrary"),
        name="s5_scan" if seq else "s5_step",
    )(x3, g, sh, sc, dsk, wbr, wbi, ar, ai, wcr, wci, h0r, h0i)


def _s5_weights(lam_re, lam_im, log_dt, b_re, b_im, c_re, c_im):
    ar, ai, bbr, bbi = _s5_discretize(lam_re, lam_im, log_dt, b_re, b_im)
    g, p, c = bbr.shape
    gs = S5_LANES // c
    nj = g // gs
    eye = jnp.eye(gs, dtype=F32)

    def wb(bb):
        w = jnp.einsum('jgpc,gh->jgchp', bb.reshape(nj, gs, p, c), eye)
        return w.reshape(nj, gs * c, gs * p).astype(BF16)

    def wc(cc):
        w = jnp.einsum('jgcp,gh->jgphc', cc.reshape(nj, gs, c, p), eye)
        return w.reshape(nj, gs * p, gs * c).astype(BF16)

    return (ar.reshape(1, g * p), ai.reshape(1, g * p), wb(bbr), wb(bbi), wc(c_re), wc(c_im))


def _glu_kernel(a_ref, x_ref, g1_ref, w_ref, o_ref, *, d):
    gu = _dot(a_ref[...].astype(BF16), w_ref[...])
    o_ref[...] = x_ref[...] + g1_ref[0] * (gu[:, :d] * jax.nn.sigmoid(gu[:, d:]))


def _glu_residual(a, x, g1, w_bf16, rb):
    t, d = x.shape
    tm = _row_tile(rb, 512)
    return pl.pallas_call(
        functools.partial(_glu_kernel, d=d),
        grid=(t // tm,),
        in_specs=[pl.BlockSpec((tm, d), lambda i: (i, 0)), pl.BlockSpec((tm, d), lambda i: (i, 0)),
                  _mod_spec(g1, rb, tm), pl.BlockSpec(w_bf16.shape, lambda i: (0, 0))],
        out_specs=pl.BlockSpec((tm, d), lambda i: (i, 0)),
        out_shape=jax.ShapeDtypeStruct((t, d), F32),
        compiler_params=_cparams("parallel"),
        name="glu_residual",
    )(a, x, g1, w_bf16)


def _router_kernel(x_ref, g_ref, sh_ref, sc_ref, w_ref, b_ref, ti_ref, tw_ref, rk_ref, cnt_ref, carry, *, tm):
    i = pl.program_id(0)

    @pl.when(i == 0)
    def _():
        carry[...] = jnp.zeros_like(carry)

    h = _normmod(x_ref[...], g_ref[...], sh_ref[0], sc_ref[0])
    h_hi, h_lo = _split(h)
    w_hi, w_lo = _split(w_ref[...])
    logits = _dot(h_hi, w_hi) + (_dot(h_lo, w_hi) + _dot(h_hi, w_lo)) + b_ref[...]
    lane = lax.broadcasted_iota(I32, (tm, LANES), 1)
    lane_f = lane.astype(F32)
    work = jnp.where(lane < N_EXPERTS, logits, -jnp.inf)
    vals, sels, idxs = [], [], []
    for _ in range(TOP_K):
        m = jnp.max(work, axis=-1, keepdims=True)
        idx = jnp.min(jnp.where(work == m, lane_f, float(LANES)), axis=-1, keepdims=True)
        sel = lane_f == idx
        work = jnp.where(sel, -jnp.inf, work)
        vals.append(m)
        sels.append(sel)
        idxs.append(idx)
    es = [jnp.exp(v - vals[0]) for v in vals]
    tot = es[0] + es[1] + es[2] + es[3]
    oh = jnp.zeros((tm, LANES), F32)
    for sel in sels:
        oh = oh + sel.astype(F32)
    r_io = lax.broadcasted_iota(I32, (tm, tm), 0)
    c_io = lax.broadcasted_iota(I32, (tm, tm), 1)
    lstrict = (c_io < r_io).astype(BF16)
    base = carry[...] + _dot(lstrict, oh.astype(BF16))
    ti = jnp.zeros((tm, LANES), F32)
    tw = jnp.zeros((tm, LANES), F32)
    rk = jnp.zeros((tm, LANES), F32)
    for k in range(TOP_K):
        rank = jnp.sum(jnp.where(sels[k], base, 0.0), axis=-1, keepdims=True)
        ti = jnp.where(lane == k, idxs[k], ti)
        tw = jnp.where(lane == k, es[k] / tot, tw)
        rk = jnp.where(lane == k, rank, rk)
    ti_ref[...] = ti.astype(I32)
    tw_ref[...] = tw
    rk_ref[...] = rk.astype(I32)
    carry[...] = carry[...] + jnp.sum(oh, axis=0, keepdims=True)
    cnt_ref[...] = carry[...]


def _router(x, rb, g, sh, sc, w_pad, b_pad):
    t, d = x.shape
    tm = _row_tile(rb, 512)
    row = lambda i: (i, 0)
    return pl.pallas_call(
        functools.partial(_router_kernel, tm=tm),
        grid=(t // tm,),
        in_specs=[pl.BlockSpec((tm, d), row), pl.BlockSpec((1, d), lambda i: (0, 0)),
                  _mod_spec(sh, rb, tm), _mod_spec(sc, rb, tm),
                  pl.BlockSpec((d, LANES), lambda i: (0, 0)), pl.BlockSpec((1, LANES), lambda i: (0, 0))],
        out_specs=[pl.BlockSpec((tm, LANES), row)] * 3 + [pl.BlockSpec((1, LANES), lambda i: (0, 0))],
        out_shape=[jax.ShapeDtypeStruct((t, LANES), I32), jax.ShapeDtypeStruct((t, LANES), F32),
                   jax.ShapeDtypeStruct((t, LANES), I32), jax.ShapeDtypeStruct((1, LANES), F32)],
        scratch_shapes=[pltpu.VMEM((1, LANES), F32)],
        compiler_params=_cparams("arbitrary"),
        name="moe_router",
    )(x, g, sh, sc, w_pad, b_pad)


def _row_copy(src, dst, sem):
    return pltpu.make_async_copy(src, dst, sem)


def _dispatch_kernel(dest_ref, x_ref, g_ref, sh_ref, sc_ref, xs_in_ref, xs_ref, hbuf, sem, *, tm):
    del xs_in_ref
    hbuf[...] = _normmod(x_ref[...], g_ref[...], sh_ref[0], sc_ref[0])

    def start(r, _):
        for k in range(TOP_K):
            d = dest_ref[r * TOP_K + k]
            _row_copy(hbuf.at[pl.ds(r, 1)], xs_ref.at[pl.ds(d, 1)], sem).start(priority=k % 2)
        return 0

    lax.fori_loop(0, tm, start, 0)

    def wait(r, _):
        for k in range(TOP_K):
            _row_copy(hbuf.at[pl.ds(0, 1)], xs_ref.at[pl.ds(0, 1)], sem).wait()
        return 0

    lax.fori_loop(0, tm, wait, 0)


def _dispatch(dest, x, rb, g, sh, sc, n_rows):
    t, d = x.shape
    tm = _row_tile(rb, 256)
    zeros = jnp.zeros((n_rows, d), F32)
    return pl.pallas_call(
        functools.partial(_dispatch_kernel, tm=tm),
        grid=(t // tm,),
        in_specs=[pl.BlockSpec((tm * TOP_K,), lambda i: (i,), memory_space=pltpu.SMEM),
                  pl.BlockSpec((tm, d), lambda i: (i, 0)), pl.BlockSpec((1, d), lambda i: (0, 0)),
                  _mod_spec(sh, rb, tm), _mod_spec(sc, rb, tm),
                  pl.BlockSpec(memory_space=pl.ANY)],
        out_specs=pl.BlockSpec(memory_space=pl.ANY),
        out_shape=jax.ShapeDtypeStruct((n_rows, d), F32),
        scratch_shapes=[pltpu.VMEM((tm, d), F32), pltpu.SemaphoreType.DMA(())],
        input_output_aliases={5: 0},
        compiler_params=_cparams("arbitrary"),
        name="moe_dispatch",
    )(dest, x, g, sh, sc, zeros)


def _expert_kernel(be_ref, xs_ref, wgu_ref, bgu_ref, wd_ref, bd_ref, ys_ref, wgu_s, wd_s, *, de):
    i = pl.program_id(0)
    changed = jnp.logical_or(i == 0, be_ref[i] != be_ref[jnp.maximum(i - 1, 0)])

    @pl.when(changed)
    def _():
        wgu_s[...] = wgu_ref[0].astype(BF16)
        wd_s[...] = wd_ref[0].astype(BF16)

    gu = _dot(xs_ref[...].astype(BF16), wgu_s[...]) + bgu_ref[0]
    gate = jnp.minimum(gu[:, :de], SWIGLU_LIMIT)
    up = jnp.clip(gu[:, de:], -SWIGLU_LIMIT, SWIGLU_LIMIT)
    act = (up + 1.0) * gate * jax.nn.sigmoid(SWIGLU_ALPHA * gate)
    ys_ref[...] = _dot(act.astype(BF16), wd_s[...]) + bd_ref[0]


def _experts(blk_e, xs, layer, w_gu, b_gu, w_down, b_down):
    n_rows, d = xs.shape
    nl, ne, _, de2 = w_gu.shape
    de = de2 // 2
    n_blk = n_rows // MOE_ROWS
    grid_spec = pltpu.PrefetchScalarGridSpec(
        num_scalar_prefetch=1,
        grid=(n_blk,),
        in_specs=[pl.BlockSpec((MOE_ROWS, d), lambda i, be: (i, 0)),
                  pl.BlockSpec((None, 1, d, de2), lambda i, be: (layer, be[i], 0, 0)),
                  pl.BlockSpec((None, 1, 1, de2), lambda i, be: (layer, be[i], 0, 0)),
                  pl.BlockSpec((None, 1, de, d), lambda i, be: (layer, be[i], 0, 0)),
                  pl.BlockSpec((None, 1, 1, d), lambda i, be: (layer, be[i], 0, 0))],
        out_specs=pl.BlockSpec((MOE_ROWS, d), lambda i, be: (i, 0)),
        scratch_shapes=[pltpu.VMEM((d, de2), BF16), pltpu.VMEM((de, d), BF16)],
    )
    return pl.pallas_call(
        functools.partial(_expert_kernel, de=de),
        grid_spec=grid_spec,
        out_shape=jax.ShapeDtypeStruct((n_rows, d), F32),
        compiler_params=_cparams("arbitrary"),
        name="moe_experts",
    )(blk_e, xs, w_gu, b_gu.reshape(nl, ne, 1, de2), w_down, b_down.reshape(nl, ne, 1, d))


def _combine_kernel(dest_ref, tw_ref, x_ref, g2_ref, ys_ref, o_ref, gbuf, sem, *, tm):
    def start(r, _):
        for k in range(TOP_K):
            d = dest_ref[r * TOP_K + k]
            _row_copy(ys_ref.at[pl.ds(d, 1)], gbuf.at[k, pl.ds(r, 1)], sem).start(priority=k % 2)
        return 0

    lax.fori_loop(0, tm, start, 0)

    def wait(r, _):
        for k in range(TOP_K):
            _row_copy(ys_ref.at[pl.ds(0, 1)], gbuf.at[k, pl.ds(0, 1)], sem).wait()
        return 0

    lax.fori_loop(0, tm, wait, 0)
    tw = tw_ref[...]
    y = tw[:, 0:1] * gbuf[0]
    for k in range(1, TOP_K):
        y = y + tw[:, k:k + 1] * gbuf[k]
    o_ref[...] = x_ref[...] + g2_ref[0] * y


def _combine(dest, tw, x, rb, g2, ys):
    t, d = x.shape
    tm = _row_tile(rb, 256)
    return pl.pallas_call(
        functools.partial(_combine_kernel, tm=tm),
        grid=(t // tm,),
        in_specs=[pl.BlockSpec((tm * TOP_K,), lambda i: (i,), memory_space=pltpu.SMEM),
                  pl.BlockSpec((tm, LANES), lambda i: (i, 0)), pl.BlockSpec((tm, d), lambda i: (i, 0)),
                  _mod_spec(g2, rb, tm), pl.BlockSpec(memory_space=pl.ANY)],
        out_specs=pl.BlockSpec((tm, d), lambda i: (i, 0)),
        out_shape=jax.ShapeDtypeStruct((t, d), F32),
        scratch_shapes=[pltpu.VMEM((TOP_K, tm, d), F32), pltpu.SemaphoreType.DMA(())],
        compiler_params=_cparams("arbitrary"),
        name="moe_combine",
    )(dest, tw, x, g2, ys)


def _moe(x, rb, g, sh, sc, g2, layer, w_router, b_router, w_gu, b_gu, w_down, b_down):
    t, d = x.shape
    ne = w_router.shape[1]
    w_pad = jnp.pad(w_router, ((0, 0), (0, LANES - ne)))
    b_pad = jnp.pad(b_router, (0, LANES - ne)).reshape(1, LANES)
    ti, tw, rk, cnt = _router(x, rb, g, sh, sc, w_pad, b_pad)
    counts = cnt[0, :ne].astype(I32)
    padded = (counts + MOE_ROWS - 1) // MOE_ROWS * MOE_ROWS
    pend = jnp.cumsum(padded)
    pstart = pend - padded
    top_i = ti[:, :TOP_K]
    start_of = jnp.sum(jnp.where(top_i[..., None] == jnp.arange(ne, dtype=I32), pstart, 0), axis=-1)
    dest = (start_of + rk[:, :TOP_K]).reshape(-1).astype(I32)
    n_blk = -(-(t * TOP_K) // MOE_ROWS) + ne
    first_row = jnp.arange(n_blk, dtype=I32)[:, None] * MOE_ROWS
    blk_e = jnp.minimum(jnp.sum((pend[None, :] <= first_row).astype(I32), axis=1), ne - 1)
    xs = _dispatch(dest, x, rb, g, sh, sc, n_blk * MOE_ROWS)
    ys = _experts(blk_e, xs, layer, w_gu, b_gu, w_down, b_down)
    return _combine(dest, tw, x, rb, g2, ys)


def _nm_matmul_kernel(x_ref, g_ref, sh_ref, sc_ref, w_ref, o_ref, *, sig_from):
    h = _normmod(x_ref[...], g_ref[...], sh_ref[0], sc_ref[0])
    o = _dot(h.astype(BF16), w_ref[...])
    if sig_from is not None:
        col = lax.broadcasted_iota(I32, o.shape, 1)
        o = jnp.where(col >= sig_from, jax.nn.sigmoid(o), o)
    o_ref[...] = o


def _normmod_matmul(x, rb, g, sh, sc, w_bf16, sig_from=None):
    t, d = x.shape
    n = w_bf16.shape[1]
    tm = _row_tile(rb, 512)
    return pl.pallas_call(
        functools.partial(_nm_matmul_kernel, sig_from=sig_from),
        grid=(t // tm,),
        in_specs=[pl.BlockSpec((tm, d), lambda i: (i, 0)), pl.BlockSpec((1, d), lambda i: (0, 0)),
                  _mod_spec(sh, rb, tm), _mod_spec(sc, rb, tm), pl.BlockSpec((d, n), lambda i: (0, 0))],
        out_specs=pl.BlockSpec((tm, n), lambda i: (i, 0)),
        out_shape=jax.ShapeDtypeStruct((t, n), F32),
        compiler_params=_cparams("parallel"),
        name="normmod_matmul",
    )(x, g, sh, sc, w_bf16)


def _mm_residual_kernel(a_ref, x_ref, g1_ref, w_ref, o_ref):
    o_ref[...] = x_ref[...] + g1_ref[0] * _dot(a_ref[...].astype(BF16), w_ref[...])


def _matmul_residual(a, x, g1, w_bf16, rb):
    t, d = x.shape
    k = a.shape[1]
    tm = _row_tile(rb, 512)
    return pl.pallas_call(
        _mm_residual_kernel,
        grid=(t // tm,),
        in_specs=[pl.BlockSpec((tm, k), lambda i: (i, 0)), pl.BlockSpec((tm, d), lambda i: (i, 0)),
                  _mod_spec(g1, rb, tm), pl.BlockSpec((k, d), lambda i: (0, 0))],
        out_specs=pl.BlockSpec((tm, d), lambda i: (i, 0)),
        out_shape=jax.ShapeDtypeStruct((t, d), F32),
        compiler_params=_cparams("parallel"),
        name="matmul_residual",
    )(a, x, g1, w_bf16)


def _final_norm_kernel(x_ref, g_ref, o_ref):
    x = x_ref[...]
    o_ref[...] = x * lax.rsqrt(jnp.mean(x * x, axis=-1, keepdims=True) + RMS_EPS) * g_ref[...]


def _final_norm(x, g, rb):
    t, d = x.shape
    tm = _row_tile(rb, 512)
    return pl.pallas_call(
        _final_norm_kernel,
        grid=(t // tm,),
        in_specs=[pl.BlockSpec((tm, d), lambda i: (i, 0)), pl.BlockSpec((1, d), lambda i: (0, 0))],
        out_specs=pl.BlockSpec((tm, d), lambda i: (i, 0)),
        out_shape=jax.ShapeDtypeStruct((t, d), F32),
        compiler_params=_cparams("parallel"),
        name="final_norm",
    )(x, g.reshape(1, d))


def _rel_bucket_table():
    n = np.arange(REL_MAX_DIST + 1)
    max_exact = REL_BUCKETS // 2
    nf = np.maximum(n, 1).astype(np.float32)
    large = max_exact + (np.log(nf / max_exact) / math.log(REL_MAX_DIST / max_exact)
                         * (REL_BUCKETS - max_exact)).astype(np.int32)
    return np.where(n < max_exact, n, np.minimum(large, REL_BUCKETS - 1)).astype(np.int32)


def _bias_by_dist(rel_bias):
    return rel_bias.astype(F32)[_rel_bucket_table()].T


def _stack_heads(tab):
    h, r, c = tab.shape
    return tab.reshape(N_KV_HEADS, GQ * r, c)


def _dist_tile(rows, cols, offset, col_step=1):
    d = offset + np.arange(rows)[:, None] - col_step * np.arange(cols)[None, :]
    return np.clip(d, 0, REL_MAX_DIST)


CMP_NEAR = 32
CMP_PAD = 16


def _overlap_matrix(n_cmp, n_sel_pad):
    ci = np.arange(n_cmp)[:, None] * CMP_STRIDE
    sj = np.arange(n_sel_pad)[None, :] * SEL_BLOCK
    ov = ((ci < sj + SEL_BLOCK) & (ci + CMP_BLOCK > sj)).astype(np.float32)
    return np.pad(ov, ((CMP_PAD, CMP_PAD), (0, 0)))


def _compress_kernel(x_ref, pos_ref, w1_ref, b1_ref, w2_ref, o_ref, *, hid):
    w1 = w1_ref[0]
    parts = _dot(x_ref[0, 0].astype(BF16), w1)
    pt = _dot(pos_ref[0].astype(BF16), w1)
    p0 = parts[:, :hid] + pt[0:1, :hid]
    p1 = parts[:, hid:] + pt[1:2, hid:]
    n = p1.shape[0]
    pre = p0 + pltpu.roll(p1, n - 1, axis=0) + b1_ref[0]
    o_ref[0, 0] = _dot(jax.nn.gelu(pre).astype(BF16), w2_ref[0])


def _compress(xc, cmp_w1, cmp_b1, cmp_w2, cmp_pos):
    _, ng, nch, kdim = xc.shape
    hid = cmp_w1.shape[-1]
    hd = cmp_w2.shape[-1]
    w1 = cmp_w1.reshape(2, CMP_RATIO, kdim, hid)
    w1cat = jnp.concatenate([w1[:, r] for r in range(CMP_RATIO)], axis=-1).astype(BF16)
    pos = jnp.pad(cmp_pos.reshape(2, CMP_RATIO, kdim), ((0, 0), (0, SUBLANES - CMP_RATIO), (0, 0)))
    return pl.pallas_call(
        functools.partial(_compress_kernel, hid=hid),
        grid=(2, ng),
        in_specs=[pl.BlockSpec((1, 1, nch, kdim), lambda c, n: (c, n, 0, 0)),
                  pl.BlockSpec((1, SUBLANES, kdim), lambda c, n: (c, 0, 0)),
                  pl.BlockSpec((1, kdim, CMP_RATIO * hid), lambda c, n: (c, 0, 0)),
                  pl.BlockSpec((1, 1, hid), lambda c, n: (c, 0, 0)),
                  pl.BlockSpec((1, hid, hd), lambda c, n: (c, 0, 0))],
        out_specs=pl.BlockSpec((1, 1, nch, hd), lambda c, n: (c, n, 0, 0)),
        out_shape=jax.ShapeDtypeStruct((2, ng, nch, hd), F32),
        compiler_params=_cparams("parallel", "parallel"),
        name="nsa_compress",
    )(xc, pos, w1cat, cmp_b1.reshape(2, 1, hid), cmp_w2.astype(BF16))


def _dot3_nt(a, b):
    a_hi, a_lo = _split(a)
    b_hi, b_lo = _split(b)
    return _dot_nt(a_hi, b_hi) + (_dot_nt(a_lo, b_hi) + _dot_nt(a_hi, b_lo))


def _eye(n):
    r = lax.broadcasted_iota(I32, (n, n), 0)
    c = lax.broadcasted_iota(I32, (n, n), 1)
    return (r == c).astype(BF16)


def _transpose_bf16(x, eye):
    return _dot_nt(eye, x).astype(BF16)


def _group_q_transposed(q):
    full = _transpose_bf16(q, _eye(GQ * HEAD_DIM))
    return jnp.concatenate([full[g * HEAD_DIM:(g + 1) * HEAD_DIM, :] for g in range(GQ)], axis=1)


def _top_blocks_cols(score, ntop):
    row_f = lax.broadcasted_iota(I32, score.shape, 0).astype(F32)
    sel = jnp.zeros(score.shape, F32)
    for _ in range(ntop):
        m = jnp.max(score, axis=0, keepdims=True)
        idx = jnp.min(jnp.where(score == m, row_f, float(score.shape[0])), axis=0, keepdims=True)
        hit = row_f == idx
        sel = jnp.where(jnp.logical_and(hit, m > -jnp.inf), 1.0, sel)
        score = jnp.where(hit, -jnp.inf, score)
    return sel


def _cmp_attn_kernel(q_ref, kc_ref, vc_ref, vct_ref, ov_ref, ovt_ref, bnear_ref, bfar_ref, oct_ref, selt_ref,
                     *, nch, ntop, qblk):
    qb = pl.program_id(2)
    cols = GQ * qblk
    scale = HEAD_DIM ** -0.5
    c = pl.multiple_of(qb * (qblk // CMP_STRIDE), SUBLANES)
    eye = _eye(HEAD_DIM)
    q_hi, q_lo = _split(q_ref[...] * scale)
    qt_hi = _group_q_transposed(q_hi)
    qt_lo = _group_q_transposed(q_lo)

    def logits(k):
        k_hi, k_lo = _split(k)
        return _dot(k_hi, qt_hi) + (_dot(k_lo, qt_hi) + _dot(k_hi, qt_lo))

    s_far = logits(kc_ref[0, 0, CMP_PAD:CMP_PAD + nch, :]) + bfar_ref[0]
    s_near = logits(kc_ref[0, 0, pl.ds(c, CMP_NEAR), :]) + bnear_ref[0]
    mask_far = lax.broadcasted_iota(I32, (nch, cols), 0) < c - CMP_PAD
    rel = lax.broadcasted_iota(I32, (CMP_NEAR, cols), 0)
    i_io = lax.broadcasted_iota(I32, (CMP_NEAR, cols), 1) & (qblk - 1)
    mask_near = jnp.logical_and(CMP_STRIDE * (rel - CMP_PAD) + CMP_BLOCK - 1 <= i_io, rel + c - CMP_PAD >= 0)
    sf = jnp.where(mask_far, s_far, NEG)
    sn = jnp.where(mask_near, s_near, NEG)
    m = jnp.maximum(jnp.max(sf, axis=0, keepdims=True), jnp.max(sn, axis=0, keepdims=True))
    pf = jnp.where(mask_far, jnp.exp(sf - m), 0.0)
    pn = jnp.where(mask_near, jnp.exp(sn - m), 0.0)
    l = jnp.sum(pf, axis=0, keepdims=True) + jnp.sum(pn, axis=0, keepdims=True)
    linv = 1.0 / jnp.where(l > 0, l, 1.0)
    vnt = _transpose_bf16(vc_ref[0, 0, pl.ds(c, CMP_NEAR), :].astype(BF16), eye)
    o = _dot(vct_ref[0, 0].astype(BF16), pf.astype(BF16)) + _dot(vnt, pn.astype(BF16))
    oct_ref[0, 0, 0] = o * linv
    pfn = pf * linv
    pnn = pn * linv
    pgf = pfn[:, 0:qblk]
    pgn = pnn[:, 0:qblk]
    for g in range(1, GQ):
        pgf = pgf + pfn[:, g * qblk:(g + 1) * qblk]
        pgn = pgn + pnn[:, g * qblk:(g + 1) * qblk]
    f_hi, f_lo = _split(pgf)
    n_hi, n_lo = _split(pgn)
    nsp = ovt_ref.shape[0]
    ovt = ovt_ref[...]
    ovnt = _transpose_bf16(ov_ref[pl.ds(c, CMP_NEAR), :], _eye(nsp))
    imp = (_dot(ovt, f_hi) + _dot(ovnt, n_hi)) + (_dot(ovt, f_lo) + _dot(ovnt, n_lo))
    s_io = lax.broadcasted_iota(I32, (nsp, qblk), 0)
    t = qb * qblk + lax.broadcasted_iota(I32, (nsp, qblk), 1)
    cur = lax.shift_right_logical(t, int(math.log2(SEL_BLOCK)))
    forced = jnp.logical_or(s_io == 0, jnp.logical_or(s_io == cur, s_io == cur - 1))
    score = jnp.where(s_io * SEL_BLOCK <= t, imp + jnp.where(forced, FORCE_SCORE, 0.0), -jnp.inf)
    selt_ref[0, 0, 0] = jnp.where(_top_blocks_cols(score, ntop) > 0.5, 0.0, NEG)


def _cmp_attention(q2d, b, t, kc_pad, vc_pad, vct, ov, ovt, bnear, bfar, nch, ntop):
    hd = HEAD_DIM
    qblk = math.gcd(t, Q_BLOCK)
    nqb = t // qblk
    nsp = ov.shape[1]
    cols = GQ * qblk
    kern = functools.partial(_cmp_attn_kernel, nch=nch, ntop=ntop, qblk=qblk)
    kvspec = pl.BlockSpec((1, 1, kc_pad.shape[2], hd), lambda bb, k, i: (bb, k, 0, 0))
    return pl.pallas_call(
        kern,
        grid=(b, N_KV_HEADS, nqb),
        in_specs=[pl.BlockSpec((qblk, GQ * hd), lambda bb, k, i: (bb * nqb + i, k)), kvspec, kvspec,
                  pl.BlockSpec((1, 1, hd, nch), lambda bb, k, i: (bb, k, 0, 0)),
                  pl.BlockSpec(ov.shape, lambda bb, k, i: (0, 0)), pl.BlockSpec(ovt.shape, lambda bb, k, i: (0, 0)),
                  pl.BlockSpec((1,) + bnear.shape[1:], lambda bb, k, i: (k, 0, 0)),
                  pl.BlockSpec((1,) + bfar.shape[1:], lambda bb, k, i: (k, 0, 0))],
        out_specs=[pl.BlockSpec((1, 1, 1, hd, cols), lambda bb, k, i: (bb, k, i, 0, 0)),
                   pl.BlockSpec((1, 1, 1, nsp, qblk), lambda bb, k, i: (bb, k, i, 0, 0))],
        out_shape=[jax.ShapeDtypeStruct((b, N_KV_HEADS, nqb, hd, cols), F32),
                   jax.ShapeDtypeStruct((b, N_KV_HEADS, nqb, nsp, qblk), F32)],
        compiler_params=_cparams("parallel", "parallel", "parallel"),
        name="nsa_cmp_attention",
    )(q2d, kc_pad, vc_pad, vct, ov, ovt, bnear, bfar)


def _slc_win_kernel(q_ref, selt_ref, oct_ref, gate_ref, sk_ref, svt_ref, wk_ref, wvt_ref, bt_ref, bfar_ref, o_ref,
                    m_s, l_s, acc_s, m_w, l_w, acc_w, s_a, s_b, *, qblk):
    qb = pl.program_id(2)
    cols = GQ * qblk
    scale = HEAD_DIM ** -0.5
    qt = _group_q_transposed((q_ref[...] * scale).astype(BF16))
    bfar = bfar_ref[0]
    per = qblk // SEL_BLOCK

    def neg_unless(cond):
        return jnp.where(cond, 0.0, NEG)

    def sel_add(kt, extra):
        pieces = [jnp.broadcast_to(selt_ref[0, 0, 0, pl.ds(per * kt + e, 1), :] + extra, (SEL_BLOCK, qblk))
                  for e in range(per)]
        mk = jnp.concatenate(pieces, axis=0)
        return jnp.concatenate([mk] * GQ, axis=1)

    slc_state = (m_s, l_s, acc_s)
    win_state = (m_w, l_w, acc_w)

    def reset(state):
        m_r, l_r, acc_r = state
        m_r[...] = jnp.full(m_r.shape, NEG, F32)
        l_r[...] = jnp.zeros(l_r.shape, F32)
        acc_r[...] = jnp.zeros(acc_r.shape, F32)

    def update(state, pieces, vts):
        update_cat(state, jnp.concatenate(pieces, axis=0), jnp.concatenate(vts, axis=1))

    def update_cat(state, s, vt):
        m_r, l_r, acc_r = state
        m_prev = m_r[...]
        m_new = jnp.maximum(m_prev, jnp.max(s, axis=0, keepdims=True))
        alpha = jnp.exp(m_prev - m_new)
        p = jnp.exp(s - m_new)
        l_r[...] = alpha * l_r[...] + jnp.sum(p, axis=0, keepdims=True)
        acc_r[...] = alpha * acc_r[...] + _dot(vt, p.astype(BF16))
        m_r[...] = m_new

    def result(state):
        return state[2][...] / state[1][...]

    def logits(k_ref, kt):
        return _dot(k_ref[0, 0, kt], qt)

    reset(slc_state)
    reset(win_state)
    n_far = jnp.maximum(qb - 1, 0)
    last = sk_ref.shape[2] - 1

    n_groups = (n_far + FAR_GROUP - 1) // FAR_GROUP

    def stage_logits(g, buf):
        for e in range(FAR_GROUP):
            kt = g * FAR_GROUP + e
            ktc = jnp.minimum(kt, last)
            buf[e * qblk:(e + 1) * qblk, :] = logits(sk_ref, ktc) + (sel_add(ktc, neg_unless(kt < n_far)) + bfar)

    def consume(g, buf):
        vts = [svt_ref[0, 0, jnp.minimum(g * FAR_GROUP + e, last)] for e in range(FAR_GROUP)]
        update_cat(slc_state, buf[...], jnp.concatenate(vts, axis=1))

    stage_logits(0, s_a)

    def far(j, carry):
        stage_logits(2 * j + 1, s_b)
        consume(2 * j, s_a)
        stage_logits(2 * j + 2, s_a)
        consume(2 * j + 1, s_b)
        return carry

    lax.fori_loop(0, (n_groups + 1) // 2, far, 0)
    kt1 = jnp.maximum(qb - 1, 0)
    near = [logits(sk_ref, kt1) + (sel_add(kt1, neg_unless(qb >= 1)) + bt_ref[0, 1]),
            logits(sk_ref, qb) + (sel_add(qb, 0.0) + bt_ref[0, 0])]

    nwin = WINDOW // qblk
    pieces, vts = [], []
    for u in range(nwin + 1):
        kt = qb - nwin + u
        ktc = jnp.maximum(kt, 0)
        delta = (nwin - u) * qblk
        if delta == 0:
            bias = bt_ref[0, 0]
        elif delta == qblk:
            bias = bt_ref[0, 1] + neg_unless(kt >= 0)
        elif u == 0:
            bias = bt_ref[0, 2] + neg_unless(kt >= 0)
        else:
            bias = bfar + neg_unless(kt >= 0)
        pieces.append(logits(wk_ref, ktc) + bias)
        vts.append(wvt_ref[0, 0, ktc])
    update(slc_state, near, [svt_ref[0, 0, kt1], svt_ref[0, 0, qb]])
    update(win_state, pieces, vts)
    o_s = result(slc_state)
    o_w = result(win_state)

    gt = gate_ref[0, 0, 0]
    ot = gt[0:1] * oct_ref[0, 0, 0] + gt[1:2] * o_s + gt[2:3] * o_w
    o_ref[0, 0, 0] = ot.astype(BF16)


def _slc_win_attention(q2d, b, t, selt, oct, gates_blk, sk, svt, wk, wvt, bt, bfar):
    hd = HEAD_DIM
    qblk = math.gcd(t, Q_BLOCK)
    nqb = t // qblk
    nsp = selt.shape[-2]
    cols = GQ * qblk
    kspec = pl.BlockSpec((1, 1, nqb, qblk, hd), lambda bb, k, i: (bb, k, 0, 0, 0))
    vspec = pl.BlockSpec((1, 1, nqb, hd, qblk), lambda bb, k, i: (bb, k, 0, 0, 0))
    qspec = pl.BlockSpec((qblk, GQ * hd), lambda bb, k, i: (bb * nqb + i, k))
    return pl.pallas_call(
        functools.partial(_slc_win_kernel, qblk=qblk),
        grid=(b, N_KV_HEADS, nqb),
        in_specs=[qspec, pl.BlockSpec((1, 1, 1, nsp, qblk), lambda bb, k, i: (bb, k, i, 0, 0)),
                  pl.BlockSpec((1, 1, 1, hd, cols), lambda bb, k, i: (bb, k, i, 0, 0)),
                  pl.BlockSpec((1, 1, 1, 3, cols), lambda bb, k, i: (bb, k, i, 0, 0)),
                  kspec, vspec, kspec, vspec,
                  pl.BlockSpec((1,) + bt.shape[1:], lambda bb, k, i: (k, 0, 0, 0)),
                  pl.BlockSpec((1,) + bfar.shape[1:], lambda bb, k, i: (k, 0, 0))],
        out_specs=pl.BlockSpec((1, 1, 1, hd, cols), lambda bb, k, i: (bb, k, i, 0, 0)),
        out_shape=jax.ShapeDtypeStruct((b, N_KV_HEADS, nqb, hd, cols), BF16),
        scratch_shapes=[pltpu.VMEM((1, cols), F32), pltpu.VMEM((1, cols), F32), pltpu.VMEM((hd, cols), F32)] * 2
        + [pltpu.VMEM((FAR_GROUP * qblk, cols), F32), pltpu.VMEM((FAR_GROUP * qblk, cols), F32)],
        compiler_params=_cparams("parallel", "parallel", "parallel"),
        name="nsa_slc_win_attention",
    )(q2d, selt, oct, gates_blk, sk, svt, wk, wvt, bt, bfar)


def _cols_table(tab):
    h, nq, nk = tab.shape
    return jnp.transpose(tab.reshape(N_KV_HEADS, GQ, nq, nk), (0, 3, 1, 2)).reshape(N_KV_HEADS, nk, GQ * nq)


def _nsa_prompt(proj, kv, rel_bias, cmp_w1, cmp_b1, cmp_w2, cmp_pos):
    b, t = kv.shape[:2]
    hq = N_HEADS * HEAD_DIM
    gates = proj[:, hq:hq + 3 * N_HEADS]
    qblk = math.gcd(t, Q_BLOCK)
    assert qblk == Q_BLOCK and t % Q_BLOCK == 0 and WINDOW % Q_BLOCK == 0 and WINDOW - qblk + 1 >= REL_MAX_DIST
    nqb = t // qblk
    nch = t // CMP_STRIDE
    n_sel = t // SEL_BLOCK
    nsp = -(-n_sel // LANES) * LANES
    xc = jnp.transpose(kv[:, :, :2], (2, 0, 3, 1, 4)).reshape(2, b * N_KV_HEADS, nch, CMP_STRIDE * HEAD_DIM)
    kvc = _compress(xc, cmp_w1, cmp_b1, cmp_w2, cmp_pos).reshape(2, b, N_KV_HEADS, nch, HEAD_DIM)
    vct = jnp.swapaxes(kvc[1], -1, -2)
    kvc = jnp.pad(kvc, ((0, 0), (0, 0), (0, 0), (CMP_PAD, CMP_PAD), (0, 0)))
    tbl = _bias_by_dist(rel_bias)
    bfar = _cols_table(jnp.broadcast_to(tbl[:, REL_MAX_DIST][:, None, None], (N_HEADS, qblk, 1)))
    bnear = _cols_table(tbl[:, _dist_tile(qblk, CMP_NEAR, CMP_STRIDE * CMP_PAD - CMP_BLOCK + 1, CMP_STRIDE)])
    qi = np.arange(qblk)[:, None]
    kj = np.arange(qblk)[None, :]
    far_tile = jnp.broadcast_to(tbl[:, REL_MAX_DIST][:, None, None], (N_HEADS, qblk, qblk))
    bt = jnp.stack([_cols_table(jnp.where(kj <= qi, tbl[:, _dist_tile(qblk, qblk, 0)], NEG)),
                    _cols_table(tbl[:, _dist_tile(qblk, qblk, qblk)]),
                    _cols_table(jnp.where(kj >= qi, far_tile, NEG))], axis=1)
    ov_np = _overlap_matrix(nch, nsp)
    ov = jnp.asarray(ov_np, BF16)
    ovt = jnp.asarray(ov_np[CMP_PAD:CMP_PAD + nch].T, BF16)
    gates_blk = jnp.transpose(gates.reshape(b, nqb, qblk, N_KV_HEADS, GQ, 3), (0, 3, 1, 5, 4, 2))
    gates_blk = gates_blk.reshape(b, N_KV_HEADS, nqb, 3, GQ * qblk)
    oct, selt = _cmp_attention(proj, b, t, kvc[0], kvc[1], vct, ov, ovt, bnear, bfar, nch, min(SEL_TOP, n_sel))
    kvb = jnp.transpose(kv[:, :, 2:].astype(BF16), (2, 0, 3, 1, 4)).reshape(4, b, N_KV_HEADS, nqb, qblk, HEAD_DIM)
    sk, wk = kvb[0], kvb[2]
    svt, wvt = jnp.swapaxes(kvb[1], -1, -2), jnp.swapaxes(kvb[3], -1, -2)
    o_t = _slc_win_attention(proj, b, t, selt, oct, gates_blk, sk, svt, wk, wvt, bt, bfar)
    o_t = o_t.reshape(b, N_KV_HEADS, nqb, HEAD_DIM, GQ, qblk)
    return jnp.transpose(o_t, (0, 2, 5, 1, 4, 3)).reshape(b, t, N_HEADS * HEAD_DIM)


CMP_TAP_PAIRS = CMP_STRIDE // 2
PAGE_CHUNKS = PAGE_SIZE // CMP_STRIDE


def _paged_compress_kernel(pt_ref, cache_ref, wp_ref, w1_ref, pos_ref, b1_ref, w2_ref, o_ref, buft, buf, carry, sem,
                           *, pg, npages, hid):
    g = pl.program_id(1)
    step = pl.program_id(0) * pl.num_programs(1) + g
    total = pl.num_programs(0) * pl.num_programs(1)
    m = pg * PAGE_CHUNKS
    hk = N_KV_HEADS * HEAD_DIM
    pairs = hk // LANES

    def page_copy(s, p, slot):
        return pltpu.make_async_copy(cache_ref.at[pt_ref[s * pg + p], pl.ds(0, 2)], buft.at[slot, p], sem.at[slot])

    @pl.when(step == 0)
    def _():
        for p in range(pg):
            page_copy(0, p, 0).start()

    for slot in range(2):
        @pl.when(jnp.logical_and(step + 1 < total, (step + 1) % 2 == slot))
        def _():
            for p in range(pg):
                page_copy(step + 1, p, slot).start()

    @pl.when(g == 0)
    def _():
        carry[...] = jnp.zeros_like(carry)

    for slot in range(2):
        @pl.when(step % 2 == slot)
        def _():
            for p in range(pg):
                page_copy(step, p, slot).wait()

    cur = step % 2

    def to_rows(p, carry_):
        r0 = pl.multiple_of(p * PAGE_SIZE, PAGE_SIZE)
        for comp in range(2):
            for hp in range(pairs):
                plane = buft[cur, p, comp, hp * LANES:(hp + 1) * LANES, :]
                buf[comp * pairs + hp, pl.ds(r0, PAGE_SIZE), :] = plane.T
        return carry_

    lax.fori_loop(0, pg, to_rows, 0)

    row0 = lax.broadcasted_iota(I32, (m, hid), 0) == 0
    for comp in range(2):
        pt = _dot(pos_ref[comp].astype(BF16), w1_ref[comp])
        for hp in range(N_KV_HEADS // 2):
            lg = (comp * hk + hp * 2 * HEAD_DIM) // LANES
            acc = jnp.zeros((m, 2 * CMP_RATIO * hid), F32)
            for u in range(CMP_TAP_PAIRS):
                taps = [buf[lg, pl.ds(2 * u + e, m, stride=CMP_STRIDE), :] for e in range(2)]
                acc = acc + _dot(jnp.concatenate(taps, axis=1).astype(BF16), wp_ref[comp, u])
            for e in range(2):
                kvh = 2 * hp + e
                base = e * CMP_RATIO * hid
                p0 = acc[:, base:base + hid] + pt[0:1, :hid]
                p1 = acc[:, base + hid:base + 2 * hid] + pt[1:2, hid:]
                prev0 = jnp.where(row0, carry[comp, kvh], pltpu.roll(p0, 1, axis=0))
                carry[comp, kvh] = p0[m - 1:m, :]
                pre = prev0 + p1 + b1_ref[comp]
                o_ref[comp, 0, kvh] = _dot(jax.nn.gelu(pre).astype(BF16), w2_ref[comp])


def _paged_compress(page_table, cache_t, cmp_w1, cmp_b1, cmp_w2, cmp_pos):
    bs, npages = page_table.shape
    hid = cmp_w1.shape[-1]
    kdim = CMP_STRIDE * HEAD_DIM
    pg = math.gcd(npages, 32)
    m = pg * PAGE_CHUNKS
    w1 = cmp_w1.reshape(2, CMP_RATIO, kdim, hid)
    w1cat = jnp.concatenate([w1[:, r] for r in range(CMP_RATIO)], axis=-1)
    n1 = CMP_RATIO * hid
    wt = w1cat.reshape(2, CMP_TAP_PAIRS, 2, HEAD_DIM, n1)
    zero = jnp.zeros_like(wt)
    wp = jnp.concatenate([jnp.concatenate([wt, zero], axis=-1), jnp.concatenate([zero, wt], axis=-1)], axis=3)
    wp = wp.reshape(2, CMP_TAP_PAIRS, 4 * HEAD_DIM, 2 * n1).astype(BF16)
    pos = jnp.pad(cmp_pos.reshape(2, CMP_RATIO, kdim), ((0, 0), (0, SUBLANES - CMP_RATIO), (0, 0)))
    nch = npages * PAGE_CHUNKS
    grid_spec = pltpu.PrefetchScalarGridSpec(
        num_scalar_prefetch=1,
        grid=(bs, npages // pg),
        in_specs=[pl.BlockSpec(memory_space=pl.ANY),
                  pl.BlockSpec(wp.shape, lambda b, g, pt: (0, 0, 0, 0)),
                  pl.BlockSpec((2, kdim, n1), lambda b, g, pt: (0, 0, 0)),
                  pl.BlockSpec((2, SUBLANES, kdim), lambda b, g, pt: (0, 0, 0)),
                  pl.BlockSpec((2, 1, hid), lambda b, g, pt: (0, 0, 0)),
                  pl.BlockSpec((2, hid, HEAD_DIM), lambda b, g, pt: (0, 0, 0))],
        out_specs=pl.BlockSpec((2, 1, N_KV_HEADS, m, HEAD_DIM), lambda b, g, pt: (0, b, 0, g, 0)),
        scratch_shapes=[pltpu.VMEM((2, pg, 2, N_KV_HEADS * HEAD_DIM, PAGE_SIZE), F32),
                        pltpu.VMEM((2 * N_KV_HEADS * HEAD_DIM // LANES, pg * PAGE_SIZE, LANES), F32),
                        pltpu.VMEM((2, N_KV_HEADS, 1, hid), F32), pltpu.SemaphoreType.DMA((2,))],
    )
    return pl.pallas_call(
        functools.partial(_paged_compress_kernel, pg=pg, npages=npages, hid=hid),
        grid_spec=grid_spec,
        out_shape=jax.ShapeDtypeStruct((2, bs, N_KV_HEADS, nch, HEAD_DIM), F32),
        compiler_params=_cparams("arbitrary", "arbitrary"),
        name="nsa_paged_compress",
    )(page_table.reshape(-1), cache_t, wp, w1cat.astype(BF16), pos, cmp_b1.reshape(2, 1, hid), cmp_w2.astype(BF16))


def _rows8(x):
    return jnp.concatenate([x, jnp.zeros((SUBLANES - x.shape[0], x.shape[1]), x.dtype)], axis=0)


def _sample_cmp_kernel(q_ref, kc_ref, vc_ref, ov_ref, bias_ref, oc_ref, idx_ref, *, nch, n_sel, n_past_blk, ntop):
    scale = HEAD_DIM ** -0.5
    nsp = ov_ref.shape[1]
    row = lax.broadcasted_iota(I32, (SUBLANES, nch), 0)
    lane = lax.broadcasted_iota(I32, (SUBLANES, nch), 1)
    mask = jnp.logical_and(lane >= 1, row < GQ)
    imp = jnp.zeros((SUBLANES, nsp), F32)
    row_s = lax.broadcasted_iota(I32, (SUBLANES, nsp), 0)
    for kvh in range(N_KV_HEADS):
        q8 = _rows8(q_ref[0, kvh * GQ:(kvh + 1) * GQ, :])
        s = jnp.where(mask, _dot3_nt(q8, kc_ref[0, kvh]) * scale + bias_ref[kvh], NEG)
        mx = jnp.max(s, axis=-1, keepdims=True)
        p = jnp.where(mask, jnp.exp(s - mx), 0.0)
        l = jnp.sum(p, axis=-1, keepdims=True)
        linv = 1.0 / jnp.where(l > 0, l, 1.0)
        o = _dot(p.astype(BF16), vc_ref[0, kvh].astype(BF16)) * linv
        oc_ref[0, kvh * GQ:(kvh + 1) * GQ, :] = o[0:GQ]
        p_hi, p_lo = _split(p * linv)
        contrib = _dot(p_hi, ov_ref[...]) + _dot(p_lo, ov_ref[...])
        imp = jnp.where(row_s == kvh, jnp.sum(contrib, axis=0, keepdims=True), imp)
    s_io = lax.broadcasted_iota(I32, (SUBLANES, nsp), 1)
    t = n_past_blk * SEL_BLOCK
    cur = t // SEL_BLOCK
    forced = jnp.logical_or(s_io == 0, jnp.logical_or(s_io == cur, s_io == cur - 1))
    valid = jnp.logical_and(s_io * SEL_BLOCK <= t, s_io < n_sel)
    score = jnp.where(valid, imp + jnp.where(forced, FORCE_SCORE, 0.0), -jnp.inf)
    lane_f = s_io.astype(F32)
    out_lane = lax.broadcasted_iota(I32, (SUBLANES, LANES), 1)
    picks = jnp.zeros((SUBLANES, LANES), F32)
    for it in range(ntop):
        mx = jnp.max(score, axis=-1, keepdims=True)
        idx = jnp.min(jnp.where(score == mx, lane_f, float(nsp)), axis=-1, keepdims=True)
        picks = jnp.where(out_lane == it, idx, picks)
        score = jnp.where(lane_f == idx, -jnp.inf, score)
    idx_ref[0] = picks.astype(I32)


def _sample_cmp_attention(q_hm, kc, vc, ov, bias, n_sel, n_past_blk, ntop):
    bs = q_hm.shape[0]
    nch = kc.shape[2]
    kvspec = pl.BlockSpec((1, N_KV_HEADS, nch, HEAD_DIM), lambda b: (b, 0, 0, 0))
    return pl.pallas_call(
        functools.partial(_sample_cmp_kernel, nch=nch, n_sel=n_sel, n_past_blk=n_past_blk, ntop=ntop),
        grid=(bs,),
        in_specs=[pl.BlockSpec((1, N_HEADS, HEAD_DIM), lambda b: (b, 0, 0)), kvspec, kvspec,
                  pl.BlockSpec(ov.shape, lambda b: (0, 0)), pl.BlockSpec(bias.shape, lambda b: (0, 0, 0))],
        out_specs=[pl.BlockSpec((1, N_HEADS, HEAD_DIM), lambda b: (b, 0, 0)),
                   pl.BlockSpec((1, SUBLANES, LANES), lambda b: (b, 0, 0))],
        out_shape=[jax.ShapeDtypeStruct(q_hm.shape, F32), jax.ShapeDtypeStruct((bs, SUBLANES, LANES), I32)],
        compiler_params=_cparams("parallel"),
        name="nsa_sample_cmp_attention",
    )(q_hm, kc, vc, ov, bias)


def _sample_slc_win_kernel(page_ref, half_ref, new_ref, d0_ref, var_ref,
                           q_ref, oc_ref, gate_ref, cache_ref, newt_ref, wint_ref, winnew_ref,
                           bslc_ref, bwin_ref, bwin_new_ref, o_ref, buf, sem, *, ntop):
    b = pl.program_id(0)
    scale = HEAD_DIM ** -0.5
    hk = N_KV_HEADS * HEAD_DIM
    nslots = N_KV_HEADS * ntop

    def plane_copies(slot):
        i = b * nslots + slot
        kvh = slot // ntop
        return [pltpu.make_async_copy(cache_ref.at[page_ref[i], 2 + e, kvh], buf.at[slot, e], sem) for e in range(2)]

    for slot in range(nslots):
        is_new = new_ref[b * nslots + slot] == 1

        @pl.when(jnp.logical_not(is_new))
        def _():
            for cp in plane_copies(slot):
                cp.start()

        @pl.when(is_new)
        def _():
            buf[slot] = newt_ref[0, :, slot // ntop]

    for slot in range(nslots):
        @pl.when(new_ref[b * nslots + slot] == 0)
        def _():
            for cp in plane_copies(slot):
                cp.wait()

    lane = lax.broadcasted_iota(I32, (SUBLANES, PAGE_SIZE), 1)
    gt = gate_ref[0]
    for kvh in range(N_KV_HEADS):
        hs = slice(kvh * GQ, (kvh + 1) * GQ)
        ks = slice(kvh * HEAD_DIM, (kvh + 1) * HEAD_DIM)
        vsl = slice(hk + kvh * HEAD_DIM, hk + (kvh + 1) * HEAD_DIM)
        q8f = _rows8(q_ref[0, hs, :])
        q8 = q8f.astype(BF16)
        ss, masks = [], []
        for j in range(ntop):
            slot = kvh * ntop + j
            i = b * nslots + slot
            s = _dot(q8, buf[slot, 0].astype(BF16)) * scale + bslc_ref[kvh, var_ref[i]]
            off = lane - half_ref[i] * SEL_BLOCK
            mask = jnp.logical_and(jnp.logical_and(off >= 0, off < SEL_BLOCK), off <= d0_ref[i])
            ss.append(jnp.where(mask, s, NEG))
            masks.append(mask)
        mx = ss[0].max(axis=-1, keepdims=True)
        for s in ss[1:]:
            mx = jnp.maximum(mx, s.max(axis=-1, keepdims=True))
        l = jnp.zeros((SUBLANES, 1), F32)
        acc = jnp.zeros((SUBLANES, HEAD_DIM), F32)
        for j in range(ntop):
            p = jnp.where(masks[j], jnp.exp(ss[j] - mx), 0.0)
            l = l + jnp.sum(p, axis=-1, keepdims=True)
            acc = acc + _dot_nt(p.astype(BF16), buf[kvh * ntop + j, 1].astype(BF16))
        o_s = acc / jnp.where(l > 0, l, 1.0)
        sw = _dot(q8, wint_ref[0, 0, kvh].astype(BF16)) * scale + bwin_ref[kvh]
        s_new = jnp.sum(q8f * winnew_ref[0, 0:1, ks], axis=-1, keepdims=True) * scale + bwin_new_ref[kvh]
        mw = jnp.maximum(sw.max(axis=-1, keepdims=True), s_new)
        pw = jnp.exp(sw - mw)
        p_new = jnp.exp(s_new - mw)
        lw = jnp.sum(pw, axis=-1, keepdims=True) + p_new
        o_w = (_dot_nt(pw.astype(BF16), wint_ref[0, 1, kvh].astype(BF16)) + p_new * winnew_ref[0, 0:1, vsl]) / lw
        g3 = gt[hs, :]
        o_ref[0, hs, :] = g3[:, 0:1] * oc_ref[0, hs, :] + g3[:, 1:2] * o_s[0:GQ] + g3[:, 2:3] * o_w[0:GQ]


def _sample_slc_win(picks, q_hm, oc, gates_h, cache_t, newt, wint, winnew, bslc, bwin, bwin_new, ntop):
    bs = q_hm.shape[0]
    hspec = pl.BlockSpec((1, N_HEADS, HEAD_DIM), lambda b, *_: (b, 0, 0))
    grid_spec = pltpu.PrefetchScalarGridSpec(
        num_scalar_prefetch=5,
        grid=(bs,),
        in_specs=[hspec, hspec, pl.BlockSpec((1, N_HEADS, 3), lambda b, *_: (b, 0, 0)),
                  pl.BlockSpec(memory_space=pl.ANY),
                  pl.BlockSpec((1,) + newt.shape[1:], lambda b, *_: (b, 0, 0, 0, 0)),
                  pl.BlockSpec((1,) + wint.shape[1:], lambda b, *_: (b, 0, 0, 0, 0)),
                  pl.BlockSpec((1,) + winnew.shape[1:], lambda b, *_: (b, 0, 0)),
                  pl.BlockSpec(bslc.shape, lambda b, *_: (0, 0, 0, 0)),
                  pl.BlockSpec(bwin.shape, lambda b, *_: (0, 0, 0)),
                  pl.BlockSpec(bwin_new.shape, lambda b, *_: (0, 0, 0))],
        out_specs=hspec,
        scratch_shapes=[pltpu.VMEM((N_KV_HEADS * ntop, 2, HEAD_DIM, PAGE_SIZE), F32), pltpu.SemaphoreType.DMA(())],
    )
    return pl.pallas_call(
        functools.partial(_sample_slc_win_kernel, ntop=ntop),
        grid_spec=grid_spec,
        out_shape=jax.ShapeDtypeStruct(q_hm.shape, F32),
        compiler_params=_cparams("arbitrary"),
        name="nsa_sample_slc_win",
    )(*picks, q_hm, oc, gates_h, cache_t, newt, wint, winnew, bslc, bwin, bwin_new)


def _nsa_sample(q, gates, kv_new, win_buf, cache, page_table, rel_bias, cmp_w1, cmp_b1, cmp_w2, cmp_pos):
    bs, npages = page_table.shape
    pool = cache.shape[0]
    hk = N_KV_HEADS * HEAD_DIM
    past = npages * PAGE_SIZE
    nch = past // CMP_STRIDE
    n_past_blk = past // SEL_BLOCK
    n_sel = n_past_blk + 1
    ntop = min(SEL_TOP, n_sel)
    nsp = -(-n_sel // LANES) * LANES
    wb = win_buf.shape[1]
    assert n_sel >= SEL_TOP and past >= wb
    cache_t = jnp.transpose(cache, (0, 2, 3, 4, 1))
    kvc = _paged_compress(page_table, cache_t.reshape(pool, 4, hk, PAGE_SIZE), cmp_w1, cmp_b1, cmp_w2, cmp_pos)
    tbl = _bias_by_dist(rel_bias)
    tok = np.arange(nch) - 1
    dist_c = np.clip(past - (tok * CMP_STRIDE + CMP_BLOCK - 1), 0, REL_MAX_DIST)
    bias_c = jnp.pad(tbl[:, dist_c].reshape(N_KV_HEADS, GQ, nch), ((0, 0), (0, SUBLANES - GQ), (0, 0)))
    ov = np.zeros((nch, nsp), np.float32)
    ov[1:] = _overlap_matrix(nch - 1, nsp)[CMP_PAD:-CMP_PAD]
    q_hm = q.reshape(bs, N_HEADS, HEAD_DIM)
    oc, idx = _sample_cmp_attention(q_hm, kvc[0], kvc[1], jnp.asarray(ov, BF16), bias_c, n_sel, n_past_blk, ntop)
    idx = idx[:, :N_KV_HEADS, :ntop]
    ip = jnp.minimum(idx, n_past_blk - 1)
    per_page = PAGE_SIZE // SEL_BLOCK
    page = jnp.take_along_axis(page_table, (ip // per_page).reshape(bs, -1), axis=1)
    is_new = (idx >= n_past_blk).astype(I32)
    half = jnp.where(is_new == 1, 0, ip % per_page)
    d0 = (n_past_blk - idx) * SEL_BLOCK
    case = jnp.clip(n_past_blk - idx, 0, 3)
    picks = [a.reshape(-1).astype(I32) for a in (page, half, is_new, d0, case * per_page + half)]
    lanes = np.arange(PAGE_SIZE)
    dist_s = np.stack([np.clip(cs * SEL_BLOCK - (lanes - hf * SEL_BLOCK), 0, REL_MAX_DIST)
                       for cs in range(4) for hf in range(per_page)])
    nvar = dist_s.shape[0]
    bslc = jnp.pad(jnp.transpose(tbl[:, dist_s].reshape(N_KV_HEADS, GQ, nvar, PAGE_SIZE), (0, 2, 1, 3)),
                   ((0, 0), (0, 0), (0, SUBLANES - GQ), (0, 0)))
    dist_w = np.clip(wb - np.arange(wb), 0, REL_MAX_DIST)
    bwin = jnp.pad(tbl[:, dist_w].reshape(N_KV_HEADS, GQ, wb), ((0, 0), (0, SUBLANES - GQ), (0, 0)))
    bwin_new = jnp.pad(tbl[:, 0].reshape(N_KV_HEADS, GQ, 1), ((0, 0), (0, SUBLANES - GQ), (0, 0)))
    newt = jnp.pad(kv_new[:, 2:4][..., None], ((0, 0),) * 4 + ((0, PAGE_SIZE - 1),))
    winnew = jnp.pad(kv_new[:, 4:6].reshape(bs, 1, 2 * hk), ((0, 0), (0, SUBLANES - 1), (0, 0)))
    wint = jnp.transpose(win_buf, (0, 2, 3, 4, 1))
    o = _sample_slc_win(picks, q_hm, oc, gates.reshape(bs, N_HEADS, 3), cache_t, newt, wint, winnew,
                        bslc, bwin, bwin_new, ntop)
    return o.reshape(bs, N_HEADS * HEAD_DIM)


def _mods(mod_l, nb, mr):
    return [m.reshape(nb, mr, -1) for m in jnp.split(mod_l, 6, axis=-1)]


def _s5_layer(x, nb, rb, seq, norm_g, mods, s5w, dsk, w_glu_bf16, h0r, h0i):
    sh1, sc1, g1 = mods[0], mods[1], mods[2]
    d = x.shape[-1]
    ar, ai, wbr, wbi, wcr, wci = s5w
    act, st_r, st_i = _s5_scan(x.reshape(nb, rb, d), norm_g.reshape(1, d), sh1, sc1, dsk.reshape(1, d),
                               wbr, wbi, ar, ai, wcr, wci, h0r, h0i, seq=seq)
    return _glu_residual(act.reshape(-1, d), x, g1, w_glu_bf16, rb), st_r, st_i


def kernel(x_prompt, x_sample, c_prompt, c_sample, state_s5, cache_nsa_kv, cache_win_kv, page_table, w_mod, b_mod, norm_g, s5_lambda_re, s5_lambda_im, s5_log_dt, s5_b_re, s5_b_im, s5_c_re, s5_c_im, s5_d, s5_w_glu, kv_mod_w, kv_mod_b, kv_norm_g, w_kv, cmp_w1, cmp_b1, cmp_w2, cmp_pos, nsa_w_in, nsa_w_out, rel_bias, moe_w_router, moe_b_router, moe_w_gu, moe_b_gu, moe_w_down, moe_b_down, final_norm_g):
    bp, lp, d = x_prompt.shape
    bs, ls, _ = x_sample.shape
    assert ls == 1, "the sample path advances exactly one token per sequence"
    depth = w_mod.shape[0]
    n_a = s5_lambda_re.shape[0]
    hq = N_HEADS * HEAD_DIM
    groups = d // S5_GROUP

    c_all = jnp.concatenate([c_prompt, c_sample], axis=0)
    mod_all = _adaln(c_all, w_mod, b_mod)
    kvmod_all = _adaln(c_all, kv_mod_w[None], kv_mod_b[None])[0]
    s5ws = [_s5_weights(s5_lambda_re[i], s5_lambda_im[i], s5_log_dt[i], s5_b_re[i], s5_b_im[i],
                        s5_c_re[i], s5_c_im[i]) for i in range(n_a)]
    wglus = [s5_w_glu[i].astype(BF16) for i in range(n_a)]
    w_kv_b = w_kv.astype(BF16)
    n_in = nsa_w_in.shape[-1]
    n_in_pad = -(-n_in // LANES) * LANES
    w_ins = [jnp.pad(nsa_w_in[j], ((0, 0), (0, n_in_pad - n_in))).astype(BF16) for j in range(depth - n_a)]
    w_outs = [nsa_w_out[j].astype(BF16) for j in range(depth - n_a)]

    def run(x3, sl, seq, h0_all, mixer):
        if seq:
            nb, rb = x3.shape[0], x3.shape[1]
            mr = 1
        else:
            nb, rb = 1, x3.shape[0]
            mr = rb
        x = x3.reshape(-1, d)
        states = []
        kv = None
        for i in range(depth):
            mods = _mods(mod_all[i, sl], nb, mr)
            if i < n_a:
                ns = s5ws[i][0].shape[-1]
                h0r = h0_all[i][..., 0].reshape(nb, -1, ns)
                h0i = h0_all[i][..., 1].reshape(nb, -1, ns)
                x, sr, si = _s5_layer(x, nb, rb, seq, norm_g[i, 0], mods, s5ws[i], s5_d[i], wglus[i], h0r, h0i)
                states.append(jnp.stack([sr.reshape(-1, groups, S5_STATE), si.reshape(-1, groups, S5_STATE)], axis=-1))
            else:
                j = i - n_a
                if kv is None:
                    kvsh, kvsc = [m.reshape(nb, mr, d) for m in jnp.split(kvmod_all[sl], 2, axis=-1)]
                    kv = _normmod_matmul(x, rb, kv_norm_g.reshape(1, d), kvsh, kvsc, w_kv_b)
                proj = _normmod_matmul(x, rb, norm_g[i, 0].reshape(1, d), mods[0], mods[1], w_ins[j], sig_from=hq)
                o = mixer(proj, kv)
                x = _matmul_residual(o, x, mods[2], w_outs[j], rb)
            x = _moe(x, rb, norm_g[i, 1].reshape(1, d), mods[3], mods[4], mods[5], i, moe_w_router[i], moe_b_router[i],
                     moe_w_gu, moe_b_gu, moe_w_down, moe_b_down)
        return _final_norm(x, final_norm_g, rb), jnp.stack(states), kv

    def prompt_mixer(proj, kv):
        o = _nsa_prompt(proj, kv.reshape(bp, lp, N_BRANCH_KV, N_KV_HEADS, HEAD_DIM), rel_bias,
                        cmp_w1, cmp_b1, cmp_w2, cmp_pos)
        return o.reshape(bp * lp, hq)

    def sample_mixer(proj, kv):
        return _nsa_sample(proj[:, :hq], proj[:, hq:n_in], kv.reshape(bs, N_BRANCH_KV, N_KV_HEADS, HEAD_DIM),
                           cache_win_kv, cache_nsa_kv, page_table, rel_bias, cmp_w1, cmp_b1, cmp_w2, cmp_pos)

    h0_p = jnp.zeros((n_a, bp, groups, S5_STATE, 2), F32)
    y_p, st_p, kv_p = run(x_prompt, slice(0, bp), True, h0_p, prompt_mixer)
    y_s, st_s, kv_s = run(x_sample.reshape(bs, d), slice(bp, bp + bs), False, state_s5, sample_mixer)
    kv_p = kv_p.reshape(bp, lp, N_BRANCH_KV, N_KV_HEADS, HEAD_DIM)
    kv_s = kv_s.reshape(bs, ls, N_BRANCH_KV, N_KV_HEADS, HEAD_DIM)
    win_p = kv_p[:, -min(WINDOW, lp):, 4:6]
    win_s = jnp.concatenate([cache_win_kv, kv_s[:, :, 4:6]], axis=1)[:, -cache_win_kv.shape[1]:]
    return (y_p.reshape(bp, lp, d), y_s.reshape(bs, ls, d), st_p, st_s, kv_p[:, :, :4], kv_s[:, :, :4], win_p, win_s)
```

```python
import functools
import math

import numpy as np
import jax
import jax.numpy as jnp
from jax import lax
from jax.experimental import pallas as pl
from jax.experimental.pallas import tpu as pltpu

F32 = jnp.float32
BF16 = jnp.bfloat16
I32 = jnp.int32

S5_GROUP = 16
S5_STATE = 64
N_HEADS = 16
HEAD_DIM = 64
N_KV_HEADS = 4
GQ = N_HEADS // N_KV_HEADS
CMP_BLOCK = 32
CMP_STRIDE = 16
CMP_RATIO = CMP_BLOCK // CMP_STRIDE
SEL_BLOCK = 64
SEL_TOP = 16
WINDOW = 512
Q_BLOCK = 128
FORCE_SCORE = 1e4
N_BRANCH_KV = 6
REL_BUCKETS = 32
REL_MAX_DIST = 128
N_EXPERTS = 32
TOP_K = 4
SWIGLU_LIMIT = 7.0
SWIGLU_ALPHA = 1.702
RMS_EPS = 1e-5
PAGE_SIZE = 128

LANES = 128
SUBLANES = 8
VMEM_LIMIT = 56 * 1024 * 1024

MOE_ROWS = 256
FAR_GROUP = 4
CMP_BLOCKS_PER_STEP = 2
S5_LANES = 256
S5_STATES = S5_LANES // S5_GROUP * S5_STATE
NEG = -1e30


def _cparams(*sem):
    return pltpu.CompilerParams(dimension_semantics=sem, vmem_limit_bytes=VMEM_LIMIT)


def _normmod(x, g, sh, sc):
    y = x * lax.rsqrt(jnp.mean(x * x, axis=-1, keepdims=True) + RMS_EPS)
    return (y * g) * (1.0 + sc) + sh


def _dot(a, b):
    return jnp.dot(a, b, preferred_element_type=F32)


def _dot_nt(a, b):
    return lax.dot_general(a, b, (((1,), (1,)), ((), ())), preferred_element_type=F32)


def _split(x):
    hi = x.astype(BF16)
    lo = (x - hi.astype(F32)).astype(BF16)
    return hi, lo


def _mod_spec(m, rb, tm):
    nb, mr, d = m.shape
    if mr == 1:
        return pl.BlockSpec((1, 1, d), lambda i, *_: ((i * tm) // rb, 0, 0))
    assert mr == rb and rb % tm == 0
    per = rb // tm
    return pl.BlockSpec((1, tm, d), lambda i, *_: (i // per, i % per, 0))


def _row_tile(rb, want):
    t = math.gcd(rb, want)
    return t


def _adaln_kernel(c_ref, w_ref, b_ref, o_ref):
    c = c_ref[...]
    a = (c * jax.nn.sigmoid(c)).astype(BF16)
    o_ref[0] = _dot(a, w_ref[0].astype(BF16)) + b_ref[0]


def _adaln(c, w, b):
    n, d = c.shape
    nl, _, nn = w.shape
    npad = -(-n // SUBLANES) * SUBLANES
    cp = jnp.pad(c, ((0, npad - n), (0, 0)))
    tn = math.gcd(nn, 1024)
    out = pl.pallas_call(
        _adaln_kernel,
        grid=(nl, nn // tn),
        in_specs=[pl.BlockSpec((npad, d), lambda l, j: (0, 0)),
                  pl.BlockSpec((1, d, tn), lambda l, j: (l, 0, j)),
                  pl.BlockSpec((1, 1, tn), lambda l, j: (l, 0, j))],
        out_specs=pl.BlockSpec((1, npad, tn), lambda l, j: (l, 0, j)),
        out_shape=jax.ShapeDtypeStruct((nl, npad, nn), F32),
        compiler_params=_cparams("parallel", "parallel"),
        name="adaln",
    )(cp, w, b.reshape(nl, 1, nn))
    return out[:, :n]


def _s5_disc_kernel(lr_ref, li_ref, ldt_ref, br_ref, bi_ref, ar_ref, ai_ref, bbr_ref, bbi_ref):
    lr, li = lr_ref[...], li_ref[...]
    dt = jnp.exp(ldt_ref[...])
    mag = jnp.exp(dt * lr)
    ar, ai = mag * jnp.cos(dt * li), mag * jnp.sin(dt * li)
    den = lr * lr + li * li
    nr, ni = ar - 1.0, ai
    fr = (nr * lr + ni * li) / den
    fi = (ni * lr - nr * li) / den
    br, bi = br_ref[...], bi_ref[...]
    ar_ref[...] = ar
    ai_ref[...] = ai
    bbr_ref[...] = fr * br - fi * bi
    bbi_ref[...] = fr * bi + fi * br


def _s5_discretize(lam_re, lam_im, log_dt, b_re, b_im):
    g, p = lam_re.shape
    c = b_re.shape[-1]
    full3 = lambda s: pl.BlockSpec(s, lambda: (0, 0, 0))
    return pl.pallas_call(
        _s5_disc_kernel,
        in_specs=[full3((g, p, 1)), full3((g, p, 1)), full3((g, 1, 1)), full3((g, p, c)), full3((g, p, c))],
        out_specs=[full3((g, p, 1)), full3((g, p, 1)), full3((g, p, c)), full3((g, p, c))],
        out_shape=[jax.ShapeDtypeStruct((g, p, 1), F32)] * 2 + [jax.ShapeDtypeStruct((g, p, c), F32)] * 2,
        name="s5_discretize",
    )(lam_re.reshape(g, p, 1), lam_im.reshape(g, p, 1), log_dt.reshape(g, 1, 1), b_re, b_im)


def _s5_kernel(x_ref, g_ref, sh_ref, sc_ref, d_ref, wbr_ref, wbi_ref, ar_ref, ai_ref, wcr_ref, wci_ref,
               h0r_ref, h0i_ref, y_ref, str_ref, sti_ref, bur, bui, sre, sim, car_r, car_i, *, seq, rows, nj):
    c = pl.program_id(1)
    h = _normmod(x_ref[0], g_ref[...], sh_ref[0], sc_ref[0])
    dsk = d_ref[...]
    for j in range(nj):
        lo, hi = j * S5_LANES, (j + 1) * S5_LANES
        slo, shi = j * S5_STATES, (j + 1) * S5_STATES
        hj = h[:, lo:hi]
        hb = hj.astype(BF16)
        ar, ai = ar_ref[:, slo:shi], ai_ref[:, slo:shi]
        if seq:
            @pl.when(c == 0)
            def _():
                car_r[j] = h0r_ref[0, :, slo:shi]
                car_i[j] = h0i_ref[0, :, slo:shi]
            bur[...] = _dot(hb, wbr_ref[j])
            bui[...] = _dot(hb, wbi_ref[j])

            def step(t, carry):
                sr, si = carry
                br = bur[pl.ds(t, 1), :]
                bi = bui[pl.ds(t, 1), :]
                nr = ar * sr - ai * si + br
                ni = ar * si + ai * sr + bi
                sre[pl.ds(t, 1), :] = nr
                sim[pl.ds(t, 1), :] = ni
                return nr, ni

            sr, si = lax.fori_loop(0, rows, step, (car_r[j], car_i[j]), unroll=8)
            car_r[j] = sr
            car_i[j] = si
            str_ref[0, :, slo:shi] = sr
            sti_ref[0, :, slo:shi] = si
            s_r, s_i = sre[...], sim[...]
        else:
            h0r, h0i = h0r_ref[0, :, slo:shi], h0i_ref[0, :, slo:shi]
            s_r = ar * h0r - ai * h0i + _dot(hb, wbr_ref[j])
            s_i = ar * h0i + ai * h0r + _dot(hb, wbi_ref[j])
            str_ref[0, :, slo:shi] = s_r
            sti_ref[0, :, slo:shi] = s_i
        y = _dot(s_r.astype(BF16), wcr_ref[j]) - _dot(s_i.astype(BF16), wci_ref[j])
        y_ref[0, :, lo:hi] = jax.nn.gelu(y + dsk[:, lo:hi] * hj)


def _s5_scan(x3, g, sh, sc, dsk, wbr, wbi, ar, ai, wcr, wci, h0r, h0i, *, seq):
    nb, rb, d = x3.shape
    ns = ar.shape[-1]
    nj = d // S5_LANES
    rows = _row_tile(rb, 256)
    hr = 1 if seq else rows
    mrow = sh.shape[1]
    mspec = (pl.BlockSpec((1, 1, d), lambda b, c: (b, 0, 0)) if mrow == 1
             else pl.BlockSpec((1, rows, d), lambda b, c: (b, c, 0)))
    hspec = (pl.BlockSpec((1, 1, ns), lambda b, c: (b, 0, 0)) if seq
             else pl.BlockSpec((1, rows, ns), lambda b, c: (b, c, 0)))
    const2 = lambda s: pl.BlockSpec(s, lambda b, c: (0, 0))
    const3 = lambda s: pl.BlockSpec(s, lambda b, c: (0, 0, 0))
    kern = functools.partial(_s5_kernel, seq=seq, rows=rows, nj=nj)
    return pl.pallas_call(
        kern,
        grid=(nb, rb // rows),
        in_specs=[pl.BlockSpec((1, rows, d), lambda b, c: (b, c, 0)), const2((1, d)), mspec, mspec, const2((1, d)),
                  const3(wbr.shape), const3(wbi.shape), const2((1, ns)), const2((1, ns)),
                  const3(wcr.shape), const3(wci.shape), hspec, hspec],
        out_specs=[pl.BlockSpec((1, rows, d), lambda b, c: (b, c, 0)), hspec, hspec],
        out_shape=[jax.ShapeDtypeStruct((nb, rb, d), F32),
                   jax.ShapeDtypeStruct(h0r.shape, F32), jax.ShapeDtypeStruct(h0i.shape, F32)],
        scratch_shapes=[pltpu.VMEM((rows, S5_STATES), F32)] * 4 + [pltpu.VMEM((nj, 1, S5_STATES), F32)] * 2,
        compiler_params=_cparams("parallel", "arbitrary"),
        name="s5_scan" if seq else "s5_step",
    )(x3, g, sh, sc, dsk, wbr, wbi, ar, ai, wcr, wci, h0r, h0i)


def _s5_weights(lam_re, lam_im, log_dt, b_re, b_im, c_re, c_im):
    ar, ai, bbr, bbi = _s5_discretize(lam_re, lam_im, log_dt, b_re, b_im)
    g, p, c = bbr.shape
    gs = S5_LANES // c
    nj = g // gs
    eye = jnp.eye(gs, dtype=F32)

    def wb(bb):
        w = jnp.einsum('jgpc,gh->jgchp', bb.reshape(nj, gs, p, c), eye)
        return w.reshape(nj, gs * c, gs * p).astype(BF16)

    def wc(cc):
        w = jnp.einsum('jgcp,gh->jgphc', cc.reshape(nj, gs, c, p), eye)
        return w.reshape(nj, gs * p, gs * c).astype(BF16)

    return (ar.reshape(1, g * p), ai.reshape(1, g * p), wb(bbr), wb(bbi), wc(c_re), wc(c_im))


def _glu_kernel(a_ref, x_ref, g1_ref, w_ref, o_ref, *, d):
    gu = _dot(a_ref[...].astype(BF16), w_ref[...])
    o_ref[...] = x_ref[...] + g1_ref[0] * (gu[:, :d] * jax.nn.sigmoid(gu[:, d:]))


def _glu_residual(a, x, g1, w_bf16, rb):
    t, d = x.shape
    tm = _row_tile(rb, 512)
    return pl.pallas_call(
        functools.partial(_glu_kernel, d=d),
        grid=(t // tm,),
        in_specs=[pl.BlockSpec((tm, d), lambda i: (i, 0)), pl.BlockSpec((tm, d), lambda i: (i, 0)),
                  _mod_spec(g1, rb, tm), pl.BlockSpec(w_bf16.shape, lambda i: (0, 0))],
        out_specs=pl.BlockSpec((tm, d), lambda i: (i, 0)),
        out_shape=jax.ShapeDtypeStruct((t, d), F32),
        compiler_params=_cparams("parallel"),
        name="glu_residual",
    )(a, x, g1, w_bf16)


def _router_kernel(x_ref, g_ref, sh_ref, sc_ref, w_ref, b_ref, ti_ref, tw_ref, rk_ref, cnt_ref, carry, *, tm):
    i = pl.program_id(0)

    @pl.when(i == 0)
    def _():
        carry[...] = jnp.zeros_like(carry)

    h = _normmod(x_ref[...], g_ref[...], sh_ref[0], sc_ref[0])
    h_hi, h_lo = _split(h)
    w_hi, w_lo = _split(w_ref[...])
    logits = _dot(h_hi, w_hi) + (_dot(h_lo, w_hi) + _dot(h_hi, w_lo)) + b_ref[...]
    lane = lax.broadcasted_iota(I32, (tm, LANES), 1)
    lane_f = lane.astype(F32)
    work = jnp.where(lane < N_EXPERTS, logits, -jnp.inf)
    vals, sels, idxs = [], [], []
    for _ in range(TOP_K):
        m = jnp.max(work, axis=-1, keepdims=True)
        idx = jnp.min(jnp.where(work == m, lane_f, float(LANES)), axis=-1, keepdims=True)
        sel = lane_f == idx
        work = jnp.where(sel, -jnp.inf, work)
        vals.append(m)
        sels.append(sel)
        idxs.append(idx)
    es = [jnp.exp(v - vals[0]) for v in vals]
    tot = es[0] + es[1] + es[2] + es[3]
    oh = jnp.zeros((tm, LANES), F32)
    for sel in sels:
        oh = oh + sel.astype(F32)
    r_io = lax.broadcasted_iota(I32, (tm, tm), 0)
    c_io = lax.broadcasted_iota(I32, (tm, tm), 1)
    lstrict = (c_io < r_io).astype(BF16)
    base = carry[...] + _dot(lstrict, oh.astype(BF16))
    ti = jnp.zeros((tm, LANES), F32)
    tw = jnp.zeros((tm, LANES), F32)
    rk = jnp.zeros((tm, LANES), F32)
    for k in range(TOP_K):
        rank = jnp.sum(jnp.where(sels[k], base, 0.0), axis=-1, keepdims=True)
        ti = jnp.where(lane == k, idxs[k], ti)
        tw = jnp.where(lane == k, es[k] / tot, tw)
        rk = jnp.where(lane == k, rank, rk)
    ti_ref[...] = ti.astype(I32)
    tw_ref[...] = tw
    rk_ref[...] = rk.astype(I32)
    carry[...] = carry[...] + jnp.sum(oh, axis=0, keepdims=True)
    cnt_ref[...] = carry[...]


def _router(x, rb, g, sh, sc, w_pad, b_pad):
    t, d = x.shape
    tm = _row_tile(rb, 512)
    row = lambda i: (i, 0)
    return pl.pallas_call(
        functools.partial(_router_kernel, tm=tm),
        grid=(t // tm,),
        in_specs=[pl.BlockSpec((tm, d), row), pl.BlockSpec((1, d), lambda i: (0, 0)),
                  _mod_spec(sh, rb, tm), _mod_spec(sc, rb, tm),
                  pl.BlockSpec((d, LANES), lambda i: (0, 0)), pl.BlockSpec((1, LANES), lambda i: (0, 0))],
        out_specs=[pl.BlockSpec((tm, LANES), row)] * 3 + [pl.BlockSpec((1, LANES), lambda i: (0, 0))],
        out_shape=[jax.ShapeDtypeStruct((t, LANES), I32), jax.ShapeDtypeStruct((t, LANES), F32),
                   jax.ShapeDtypeStruct((t, LANES), I32), jax.ShapeDtypeStruct((1, LANES), F32)],
        scratch_shapes=[pltpu.VMEM((1, LANES), F32)],
        compiler_params=_cparams("arbitrary"),
        name="moe_router",
    )(x, g, sh, sc, w_pad, b_pad)


def _row_copy(src, dst, sem):
    return pltpu.make_async_copy(src, dst, sem)


def _dispatch_kernel(dest_ref, x_ref, g_ref, sh_ref, sc_ref, xs_in_ref, xs_ref, hbuf, sem, *, tm):
    del xs_in_ref
    hbuf[...] = _normmod(x_ref[...], g_ref[...], sh_ref[0], sc_ref[0])

    def start(r, _):
        for k in range(TOP_K):
            d = dest_ref[r * TOP_K + k]
            _row_copy(hbuf.at[pl.ds(r, 1)], xs_ref.at[pl.ds(d, 1)], sem).start(priority=k % 2)
        return 0

    lax.fori_loop(0, tm, start, 0)

    def wait(r, _):
        for k in range(TOP_K):
            _row_copy(hbuf.at[pl.ds(0, 1)], xs_ref.at[pl.ds(0, 1)], sem).wait()
        return 0

    lax.fori_loop(0, tm, wait, 0)


def _dispatch(dest, x, rb, g, sh, sc, n_rows):
    t, d = x.shape
    tm = _row_tile(rb, 256)
    zeros = jnp.zeros((n_rows, d), F32)
    return pl.pallas_call(
        functools.partial(_dispatch_kernel, tm=tm),
        grid=(t // tm,),
        in_specs=[pl.BlockSpec((tm * TOP_K,), lambda i: (i,), memory_space=pltpu.SMEM),
                  pl.BlockSpec((tm, d), lambda i: (i, 0)), pl.BlockSpec((1, d), lambda i: (0, 0)),
                  _mod_spec(sh, rb, tm), _mod_spec(sc, rb, tm),
                  pl.BlockSpec(memory_space=pl.ANY)],
        out_specs=pl.BlockSpec(memory_space=pl.ANY),
        out_shape=jax.ShapeDtypeStruct((n_rows, d), F32),
        scratch_shapes=[pltpu.VMEM((tm, d), F32), pltpu.SemaphoreType.DMA(())],
        input_output_aliases={5: 0},
        compiler_params=_cparams("arbitrary"),
        name="moe_dispatch",
    )(dest, x, g, sh, sc, zeros)


def _expert_kernel(be_ref, xs_ref, wgu_ref, bgu_ref, wd_ref, bd_ref, ys_ref, wgu_s, wd_s, *, de):
    i = pl.program_id(0)
    changed = jnp.logical_or(i == 0, be_ref[i] != be_ref[jnp.maximum(i - 1, 0)])

    @pl.when(changed)
    def _():
        wgu_s[...] = wgu_ref[0].astype(BF16)
        wd_s[...] = wd_ref[0].astype(BF16)

    gu = _dot(xs_ref[...].astype(BF16), wgu_s[...]) + bgu_ref[0]
    gate = jnp.minimum(gu[:, :de], SWIGLU_LIMIT)
    up = jnp.clip(gu[:, de:], -SWIGLU_LIMIT, SWIGLU_LIMIT)
    act = (up + 1.0) * gate * jax.nn.sigmoid(SWIGLU_ALPHA * gate)
    ys_ref[...] = _dot(act.astype(BF16), wd_s[...]) + bd_ref[0]


def _experts(blk_e, xs, layer, w_gu, b_gu, w_down, b_down):
    n_rows, d = xs.shape
    nl, ne, _, de2 = w_gu.shape
    de = de2 // 2
    n_blk = n_rows // MOE_ROWS
    grid_spec = pltpu.PrefetchScalarGridSpec(
        num_scalar_prefetch=1,
        grid=(n_blk,),
        in_specs=[pl.BlockSpec((MOE_ROWS, d), lambda i, be: (i, 0)),
                  pl.BlockSpec((None, 1, d, de2), lambda i, be: (layer, be[i], 0, 0)),
                  pl.BlockSpec((None, 1, 1, de2), lambda i, be: (layer, be[i], 0, 0)),
                  pl.BlockSpec((None, 1, de, d), lambda i, be: (layer, be[i], 0, 0)),
                  pl.BlockSpec((None, 1, 1, d), lambda i, be: (layer, be[i], 0, 0))],
        out_specs=pl.BlockSpec((MOE_ROWS, d), lambda i, be: (i, 0)),
        scratch_shapes=[pltpu.VMEM((d, de2), BF16), pltpu.VMEM((de, d), BF16)],
    )
    return pl.pallas_call(
        functools.partial(_expert_kernel, de=de),
        grid_spec=grid_spec,
        out_shape=jax.ShapeDtypeStruct((n_rows, d), F32),
        compiler_params=_cparams("arbitrary"),
        name="moe_experts",
    )(blk_e, xs, w_gu, b_gu.reshape(nl, ne, 1, de2), w_down, b_down.reshape(nl, ne, 1, d))


def _combine_kernel(dest_ref, tw_ref, x_ref, g2_ref, ys_ref, o_ref, gbuf, sem, *, tm):
    def start(r, _):
        for k in range(TOP_K):
            d = dest_ref[r * TOP_K + k]
            _row_copy(ys_ref.at[pl.ds(d, 1)], gbuf.at[k, pl.ds(r, 1)], sem).start(priority=k % 2)
        return 0

    lax.fori_loop(0, tm, start, 0)

    def wait(r, _):
        for k in range(TOP_K):
            _row_copy(ys_ref.at[pl.ds(0, 1)], gbuf.at[k, pl.ds(0, 1)], sem).wait()
        return 0

    lax.fori_loop(0, tm, wait, 0)
    tw = tw_ref[...]
    y = tw[:, 0:1] * gbuf[0]
    for k in range(1, TOP_K):
        y = y + tw[:, k:k + 1] * gbuf[k]
    o_ref[...] = x_ref[...] + g2_ref[0] * y


def _combine(dest, tw, x, rb, g2, ys):
    t, d = x.shape
    tm = _row_tile(rb, 256)
    return pl.pallas_call(
        functools.partial(_combine_kernel, tm=tm),
        grid=(t // tm,),
        in_specs=[pl.BlockSpec((tm * TOP_K,), lambda i: (i,), memory_space=pltpu.SMEM),
                  pl.BlockSpec((tm, LANES), lambda i: (i, 0)), pl.BlockSpec((tm, d), lambda i: (i, 0)),
                  _mod_spec(g2, rb, tm), pl.BlockSpec(memory_space=pl.ANY)],
        out_specs=pl.BlockSpec((tm, d), lambda i: (i, 0)),
        out_shape=jax.ShapeDtypeStruct((t, d), F32),
        scratch_shapes=[pltpu.VMEM((TOP_K, tm, d), F32), pltpu.SemaphoreType.DMA(())],
        compiler_params=_cparams("arbitrary"),
        name="moe_combine",
    )(dest, tw, x, g2, ys)


def _moe(x, rb, g, sh, sc, g2, layer, w_router, b_router, w_gu, b_gu, w_down, b_down):
    t, d = x.shape
    ne = w_router.shape[1]
    w_pad = jnp.pad(w_router, ((0, 0), (0, LANES - ne)))
    b_pad = jnp.pad(b_router, (0, LANES - ne)).reshape(1, LANES)
    ti, tw, rk, cnt = _router(x, rb, g, sh, sc, w_pad, b_pad)
    counts = cnt[0, :ne].astype(I32)
    padded = (counts + MOE_ROWS - 1) // MOE_ROWS * MOE_ROWS
    pend = jnp.cumsum(padded)
    pstart = pend - padded
    top_i = ti[:, :TOP_K]
    start_of = jnp.sum(jnp.where(top_i[..., None] == jnp.arange(ne, dtype=I32), pstart, 0), axis=-1)
    dest = (start_of + rk[:, :TOP_K]).reshape(-1).astype(I32)
    n_blk = -(-(t * TOP_K) // MOE_ROWS) + ne
    first_row = jnp.arange(n_blk, dtype=I32)[:, None] * MOE_ROWS
    blk_e = jnp.minimum(jnp.sum((pend[None, :] <= first_row).astype(I32), axis=1), ne - 1)
    xs = _dispatch(dest, x, rb, g, sh, sc, n_blk * MOE_ROWS)
    ys = _experts(blk_e, xs, layer, w_gu, b_gu, w_down, b_down)
    return _combine(dest, tw, x, rb, g2, ys)


def _nm_matmul_kernel(x_ref, g_ref, sh_ref, sc_ref, w_ref, o_ref, *, sig_from):
    h = _normmod(x_ref[...], g_ref[...], sh_ref[0], sc_ref[0])
    o = _dot(h.astype(BF16), w_ref[...])
    if sig_from is not None:
        col = lax.broadcasted_iota(I32, o.shape, 1)
        o = jnp.where(col >= sig_from, jax.nn.sigmoid(o), o)
    o_ref[...] = o


def _normmod_matmul(x, rb, g, sh, sc, w_bf16, sig_from=None):
    t, d = x.shape
    n = w_bf16.shape[1]
    tm = _row_tile(rb, 512)
    return pl.pallas_call(
        functools.partial(_nm_matmul_kernel, sig_from=sig_from),
        grid=(t // tm,),
        in_specs=[pl.BlockSpec((tm, d), lambda i: (i, 0)), pl.BlockSpec((1, d), lambda i: (0, 0)),
                  _mod_spec(sh, rb, tm), _mod_spec(sc, rb, tm), pl.BlockSpec((d, n), lambda i: (0, 0))],
        out_specs=pl.BlockSpec((tm, n), lambda i: (i, 0)),
        out_shape=jax.ShapeDtypeStruct((t, n), F32),
        compiler_params=_cparams("parallel"),
        name="normmod_matmul",
    )(x, g, sh, sc, w_bf16)


def _mm_residual_kernel(a_ref, x_ref, g1_ref, w_ref, o_ref):
    o_ref[...] = x_ref[...] + g1_ref[0] * _dot(a_ref[...].astype(BF16), w_ref[...])


def _matmul_residual(a, x, g1, w_bf16, rb):
    t, d = x.shape
    k = a.shape[1]
    tm = _row_tile(rb, 512)
    return pl.pallas_call(
        _mm_residual_kernel,
        grid=(t // tm,),
        in_specs=[pl.BlockSpec((tm, k), lambda i: (i, 0)), pl.BlockSpec((tm, d), lambda i: (i, 0)),
                  _mod_spec(g1, rb, tm), pl.BlockSpec((k, d), lambda i: (0, 0))],
        out_specs=pl.BlockSpec((tm, d), lambda i: (i, 0)),
        out_shape=jax.ShapeDtypeStruct((t, d), F32),
        compiler_params=_cparams("parallel"),
        name="matmul_residual",
    )(a, x, g1, w_bf16)


def _final_norm_kernel(x_ref, g_ref, o_ref):
    x = x_ref[...]
    o_ref[...] = x * lax.rsqrt(jnp.mean(x * x, axis=-1, keepdims=True) + RMS_EPS) * g_ref[...]


def _final_norm(x, g, rb):
    t, d = x.shape
    tm = _row_tile(rb, 512)
    return pl.pallas_call(
        _final_norm_kernel,
        grid=(t // tm,),
        in_specs=[pl.BlockSpec((tm, d), lambda i: (i, 0)), pl.BlockSpec((1, d), lambda i: (0, 0))],
        out_specs=pl.BlockSpec((tm, d), lambda i: (i, 0)),
        out_shape=jax.ShapeDtypeStruct((t, d), F32),
        compiler_params=_cparams("parallel"),
        name="final_norm",
    )(x, g.reshape(1, d))


def _rel_bucket_table():
    n = np.arange(REL_MAX_DIST + 1)
    max_exact = REL_BUCKETS // 2
    nf = np.maximum(n, 1).astype(np.float32)
    large = max_exact + (np.log(nf / max_exact) / math.log(REL_MAX_DIST / max_exact)
                         * (REL_BUCKETS - max_exact)).astype(np.int32)
    return np.where(n < max_exact, n, np.minimum(large, REL_BUCKETS - 1)).astype(np.int32)


def _bias_by_dist(rel_bias):
    return rel_bias.astype(F32)[_rel_bucket_table()].T


def _stack_heads(tab):
    h, r, c = tab.shape
    return tab.reshape(N_KV_HEADS, GQ * r, c)


def _dist_tile(rows, cols, offset, col_step=1):
    d = offset + np.arange(rows)[:, None] - col_step * np.arange(cols)[None, :]
    return np.clip(d, 0, REL_MAX_DIST)


CMP_NEAR = 32
CMP_PAD = 16


def _overlap_matrix(n_cmp, n_sel_pad):
    ci = np.arange(n_cmp)[:, None] * CMP_STRIDE
    sj = np.arange(n_sel_pad)[None, :] * SEL_BLOCK
    ov = ((ci < sj + SEL_BLOCK) & (ci + CMP_BLOCK > sj)).astype(np.float32)
    return np.pad(ov, ((CMP_PAD, CMP_PAD), (0, 0)))


def _compress_kernel(x_ref, pos_ref, w1_ref, b1_ref, w2_ref, o_ref, *, hid):
    w1 = w1_ref[0]
    parts = _dot(x_ref[0, 0].astype(BF16), w1)
    pt = _dot(pos_ref[0].astype(BF16), w1)
    p0 = parts[:, :hid] + pt[0:1, :hid]
    p1 = parts[:, hid:] + pt[1:2, hid:]
    n = p1.shape[0]
    pre = p0 + pltpu.roll(p1, n - 1, axis=0) + b1_ref[0]
    o_ref[0, 0] = _dot(jax.nn.gelu(pre).astype(BF16), w2_ref[0])


def _compress(xc, cmp_w1, cmp_b1, cmp_w2, cmp_pos):
    _, ng, nch, kdim = xc.shape
    hid = cmp_w1.shape[-1]
    hd = cmp_w2.shape[-1]
    w1 = cmp_w1.reshape(2, CMP_RATIO, kdim, hid)
    w1cat = jnp.concatenate([w1[:, r] for r in range(CMP_RATIO)], axis=-1).astype(BF16)
    pos = jnp.pad(cmp_pos.reshape(2, CMP_RATIO, kdim), ((0, 0), (0, SUBLANES - CMP_RATIO), (0, 0)))
    return pl.pallas_call(
        functools.partial(_compress_kernel, hid=hid),
        grid=(2, ng),
        in_specs=[pl.BlockSpec((1, 1, nch, kdim), lambda c, n: (c, n, 0, 0)),
                  pl.BlockSpec((1, SUBLANES, kdim), lambda c, n: (c, 0, 0)),
                  pl.BlockSpec((1, kdim, CMP_RATIO * hid), lambda c, n: (c, 0, 0)),
                  pl.BlockSpec((1, 1, hid), lambda c, n: (c, 0, 0)),
                  pl.BlockSpec((1, hid, hd), lambda c, n: (c, 0, 0))],
        out_specs=pl.BlockSpec((1, 1, nch, hd), lambda c, n: (c, n, 0, 0)),
        out_shape=jax.ShapeDtypeStruct((2, ng, nch, hd), F32),
        compiler_params=_cparams("parallel", "parallel"),
        name="nsa_compress",
    )(xc, pos, w1cat, cmp_b1.reshape(2, 1, hid), cmp_w2.astype(BF16))


def _dot3_nt(a, b):
    a_hi, a_lo = _split(a)
    b_hi, b_lo = _split(b)
    return _dot_nt(a_hi, b_hi) + (_dot_nt(a_lo, b_hi) + _dot_nt(a_hi, b_lo))


def _eye(n):
    r = lax.broadcasted_iota(I32, (n, n), 0)
    c = lax.broadcasted_iota(I32, (n, n), 1)
    return (r == c).astype(BF16)


def _transpose_bf16(x, eye):
    return _dot_nt(eye, x).astype(BF16)


def _group_q_transposed(q):
    full = _transpose_bf16(q, _eye(GQ * HEAD_DIM))
    return jnp.concatenate([full[g * HEAD_DIM:(g + 1) * HEAD_DIM, :] for g in range(GQ)], axis=1)


def _top_blocks_cols(score, ntop):
    row_f = lax.broadcasted_iota(I32, score.shape, 0).astype(F32)
    sel = jnp.zeros(score.shape, F32)
    for _ in range(ntop):
        m = jnp.max(score, axis=0, keepdims=True)
        idx = jnp.min(jnp.where(score == m, row_f, float(score.shape[0])), axis=0, keepdims=True)
        hit = row_f == idx
        sel = jnp.where(jnp.logical_and(hit, m > -jnp.inf), 1.0, sel)
        score = jnp.where(hit, -jnp.inf, score)
    return sel


def _cmp_attn_kernel(q_ref, kc_ref, vc_ref, vct_ref, ov_ref, ovt_ref, bnear_ref, bfar_ref, oct_ref, selt_ref,
                     *, nch, ntop, qblk, nper):
    for h in range(nper):
        _cmp_attn_block(h, pl.program_id(2) * nper + h, q_ref, kc_ref, vc_ref, vct_ref, ov_ref, ovt_ref, bnear_ref,
                        bfar_ref, oct_ref, selt_ref, nch=nch, ntop=ntop, qblk=qblk)


def _cmp_attn_block(h, qb, q_ref, kc_ref, vc_ref, vct_ref, ov_ref, ovt_ref, bnear_ref, bfar_ref, oct_ref, selt_ref,
                    *, nch, ntop, qblk):
    cols = GQ * qblk
    scale = HEAD_DIM ** -0.5
    c = pl.multiple_of(qb * (qblk // CMP_STRIDE), SUBLANES)
    eye = _eye(HEAD_DIM)
    q_hi, q_lo = _split(q_ref[h * qblk:(h + 1) * qblk, :] * scale)
    qt_hi = _group_q_transposed(q_hi)
    qt_lo = _group_q_transposed(q_lo)

    def logits(k):
        k_hi, k_lo = _split(k)
        return _dot(k_hi, qt_hi) + (_dot(k_lo, qt_hi) + _dot(k_hi, qt_lo))

    s_far = logits(kc_ref[0, 0, CMP_PAD:CMP_PAD + nch, :]) + bfar_ref[0]
    s_near = logits(kc_ref[0, 0, pl.ds(c, CMP_NEAR), :]) + bnear_ref[0]
    mask_far = lax.broadcasted_iota(I32, (nch, cols), 0) < c - CMP_PAD
    rel = lax.broadcasted_iota(I32, (CMP_NEAR, cols), 0)
    i_io = lax.broadcasted_iota(I32, (CMP_NEAR, cols), 1) & (qblk - 1)
    mask_near = jnp.logical_and(CMP_STRIDE * (rel - CMP_PAD) + CMP_BLOCK - 1 <= i_io, rel + c - CMP_PAD >= 0)
    sf = jnp.where(mask_far, s_far, NEG)
    sn = jnp.where(mask_near, s_near, NEG)
    m = jnp.maximum(jnp.max(sf, axis=0, keepdims=True), jnp.max(sn, axis=0, keepdims=True))
    pf = jnp.where(mask_far, jnp.exp(sf - m), 0.0)
    pn = jnp.where(mask_near, jnp.exp(sn - m), 0.0)
    l = jnp.sum(pf, axis=0, keepdims=True) + jnp.sum(pn, axis=0, keepdims=True)
    linv = 1.0 / jnp.where(l > 0, l, 1.0)
    vnt = _transpose_bf16(vc_ref[0, 0, pl.ds(c, CMP_NEAR), :].astype(BF16), eye)
    o = _dot(vct_ref[0, 0].astype(BF16), pf.astype(BF16)) + _dot(vnt, pn.astype(BF16))
    oct_ref[0, 0, h] = o * linv
    pfn = pf * linv
    pnn = pn * linv
    pgf = pfn[:, 0:qblk]
    pgn = pnn[:, 0:qblk]
    for g in range(1, GQ):
        pgf = pgf + pfn[:, g * qblk:(g + 1) * qblk]
        pgn = pgn + pnn[:, g * qblk:(g + 1) * qblk]
    f_hi, f_lo = _split(pgf)
    n_hi, n_lo = _split(pgn)
    nsp = ovt_ref.shape[0]
    ovt = ovt_ref[...]
    ovnt = _transpose_bf16(ov_ref[pl.ds(c, CMP_NEAR), :], _eye(nsp))
    imp = (_dot(ovt, f_hi) + _dot(ovnt, n_hi)) + (_dot(ovt, f_lo) + _dot(ovnt, n_lo))
    s_io = lax.broadcasted_iota(I32, (nsp, qblk), 0)
    t = qb * qblk + lax.broadcasted_iota(I32, (nsp, qblk), 1)
    cur = lax.shift_right_logical(t, int(math.log2(SEL_BLOCK)))
    forced = jnp.logical_or(s_io == 0, jnp.logical_or(s_io == cur, s_io == cur - 1))
    score = jnp.where(s_io * SEL_BLOCK <= t, imp + jnp.where(forced, FORCE_SCORE, 0.0), -jnp.inf)
    selt_ref[0, 0, h] = jnp.where(_top_blocks_cols(score, ntop) > 0.5, 0.0, NEG)


def _cmp_attention(q2d, b, t, kc_pad, vc_pad, vct, ov, ovt, bnear, bfar, nch, ntop):
    hd = HEAD_DIM
    qblk = math.gcd(t, Q_BLOCK)
    nqb = t // qblk
    nsp = ov.shape[1]
    cols = GQ * qblk
    nper = math.gcd(nqb, CMP_BLOCKS_PER_STEP)
    nsteps = nqb // nper
    kern = functools.partial(_cmp_attn_kernel, nch=nch, ntop=ntop, qblk=qblk, nper=nper)
    kvspec = pl.BlockSpec((1, 1, kc_pad.shape[2], hd), lambda bb, k, i: (bb, k, 0, 0))
    return pl.pallas_call(
        kern,
        grid=(b, N_KV_HEADS, nsteps),
        in_specs=[pl.BlockSpec((nper * qblk, GQ * hd), lambda bb, k, i: (bb * nsteps + i, k)), kvspec, kvspec,
                  pl.BlockSpec((1, 1, hd, nch), lambda bb, k, i: (bb, k, 0, 0)),
                  pl.BlockSpec(ov.shape, lambda bb, k, i: (0, 0)), pl.BlockSpec(ovt.shape, lambda bb, k, i: (0, 0)),
                  pl.BlockSpec((1,) + bnear.shape[1:], lambda bb, k, i: (k, 0, 0)),
                  pl.BlockSpec((1,) + bfar.shape[1:], lambda bb, k, i: (k, 0, 0))],
        out_specs=[pl.BlockSpec((1, 1, nper, hd, cols), lambda bb, k, i: (bb, k, i, 0, 0)),
                   pl.BlockSpec((1, 1, nper, nsp, qblk), lambda bb, k, i: (bb, k, i, 0, 0))],
        out_shape=[jax.ShapeDtypeStruct((b, N_KV_HEADS, nqb, hd, cols), F32),
                   jax.ShapeDtypeStruct((b, N_KV_HEADS, nqb, nsp, qblk), F32)],
        compiler_params=_cparams("parallel", "parallel", "parallel"),
        name="nsa_cmp_attention",
    )(q2d, kc_pad, vc_pad, vct, ov, ovt, bnear, bfar)


def _slc_win_kernel(q_ref, selt_ref, oct_ref, gate_ref, sk_ref, svt_ref, wk_ref, wvt_ref, bt_ref, bfar_ref, o_ref,
                    m_s, l_s, acc_s, m_w, l_w, acc_w, s_a, s_b, *, qblk):
    qb = pl.program_id(2)
    cols = GQ * qblk
    scale = HEAD_DIM ** -0.5
    qt = _group_q_transposed((q_ref[...] * scale).astype(BF16))
    bfar = bfar_ref[0]
    per = qblk // SEL_BLOCK

    def neg_unless(cond):
        return jnp.where(cond, 0.0, NEG)

    def sel_add(kt, extra):
        pieces = [jnp.broadcast_to(selt_ref[0, 0, 0, pl.ds(per * kt + e, 1), :] + extra, (SEL_BLOCK, qblk))
                  for e in range(per)]
        mk = jnp.concatenate(pieces, axis=0)
        return jnp.concatenate([mk] * GQ, axis=1)

    slc_state = (m_s, l_s, acc_s)
    win_state = (m_w, l_w, acc_w)

    def reset(state):
        m_r, l_r, acc_r = state
        m_r[...] = jnp.full(m_r.shape, NEG, F32)
        l_r[...] = jnp.zeros(l_r.shape, F32)
        acc_r[...] = jnp.zeros(acc_r.shape, F32)

    def update(state, pieces, vts):
        update_cat(state, jnp.concatenate(pieces, axis=0), jnp.concatenate(vts, axis=1))

    def update_cat(state, s, vt):
        m_r, l_r, acc_r = state
        m_prev = m_r[...]
        m_new = jnp.maximum(m_prev, jnp.max(s, axis=0, keepdims=True))
        alpha = jnp.exp(m_prev - m_new)
        p = jnp.exp(s - m_new)
        l_r[...] = alpha * l_r[...] + jnp.sum(p, axis=0, keepdims=True)
        acc_r[...] = alpha * acc_r[...] + _dot(vt, p.astype(BF16))
        m_r[...] = m_new

    def result(state):
        return state[2][...] / state[1][...]

    def logits(k_ref, kt):
        return _dot(k_ref[0, 0, kt], qt)

    reset(slc_state)
    reset(win_state)
    n_far = jnp.maximum(qb - 1, 0)
    last = sk_ref.shape[2] - 1

    n_groups = (n_far + FAR_GROUP - 1) // FAR_GROUP

    def stage_logits(g, buf):
        for e in range(FAR_GROUP):
            kt = g * FAR_GROUP + e
            ktc = jnp.minimum(kt, last)
            buf[e * qblk:(e + 1) * qblk, :] = logits(sk_ref, ktc) + (sel_add(ktc, neg_unless(kt < n_far)) + bfar)

    def consume(g, buf):
        vts = [svt_ref[0, 0, jnp.minimum(g * FAR_GROUP + e, last)] for e in range(FAR_GROUP)]
        update_cat(slc_state, buf[...], jnp.concatenate(vts, axis=1))

    stage_logits(0, s_a)

    def far(j, carry):
        stage_logits(2 * j + 1, s_b)
        consume(2 * j, s_a)
        stage_logits(2 * j + 2, s_a)
        consume(2 * j + 1, s_b)
        return carry

    lax.fori_loop(0, (n_groups + 1) // 2, far, 0)
    kt1 = jnp.maximum(qb - 1, 0)
    near = [logits(sk_ref, kt1) + (sel_add(kt1, neg_unless(qb >= 1)) + bt_ref[0, 1]),
            logits(sk_ref, qb) + (sel_add(qb, 0.0) + bt_ref[0, 0])]

    nwin = WINDOW // qblk
    pieces, vts = [], []
    for u in range(nwin + 1):
        kt = qb - nwin + u
        ktc = jnp.maximum(kt, 0)
        delta = (nwin - u) * qblk
        if delta == 0:
            bias = bt_ref[0, 0]
        elif delta == qblk:
            bias = bt_ref[0, 1] + neg_unless(kt >= 0)
        elif u == 0:
            bias = bt_ref[0, 2] + neg_unless(kt >= 0)
        else:
            bias = bfar + neg_unless(kt >= 0)
        pieces.append(logits(wk_ref, ktc) + bias)
        vts.append(wvt_ref[0, 0, ktc])
    update(slc_state, near, [svt_ref[0, 0, kt1], svt_ref[0, 0, qb]])
    update(win_state, pieces, vts)
    o_s = result(slc_state)
    o_w = result(win_state)

    gt = gate_ref[0, 0, 0]
    ot = gt[0:1] * oct_ref[0, 0, 0] + gt[1:2] * o_s + gt[2:3] * o_w
    o_ref[0, 0, 0] = ot.astype(BF16)


def _slc_win_attention(q2d, b, t, selt, oct, gates_blk, sk, svt, wk, wvt, bt, bfar):
    hd = HEAD_DIM
    qblk = math.gcd(t, Q_BLOCK)
    nqb = t // qblk
    nsp = selt.shape[-2]
    cols = GQ * qblk
    kspec = pl.BlockSpec((1, 1, nqb, qblk, hd), lambda bb, k, i: (bb, k, 0, 0, 0))
    vspec = pl.BlockSpec((1, 1, nqb, hd, qblk), lambda bb, k, i: (bb, k, 0, 0, 0))
    qspec = pl.BlockSpec((qblk, GQ * hd), lambda bb, k, i: (bb * nqb + i, k))
    return pl.pallas_call(
        functools.partial(_slc_win_kernel, qblk=qblk),
        grid=(b, N_KV_HEADS, nqb),
        in_specs=[qspec, pl.BlockSpec((1, 1, 1, nsp, qblk), lambda bb, k, i: (bb, k, i, 0, 0)),
                  pl.BlockSpec((1, 1, 1, hd, cols), lambda bb, k, i: (bb, k, i, 0, 0)),
                  pl.BlockSpec((1, 1, 1, 3, cols), lambda bb, k, i: (bb, k, i, 0, 0)),
                  kspec, vspec, kspec, vspec,
                  pl.BlockSpec((1,) + bt.shape[1:], lambda bb, k, i: (k, 0, 0, 0)),
                  pl.BlockSpec((1,) + bfar.shape[1:], lambda bb, k, i: (k, 0, 0))],
        out_specs=pl.BlockSpec((1, 1, 1, hd, cols), lambda bb, k, i: (bb, k, i, 0, 0)),
        out_shape=jax.ShapeDtypeStruct((b, N_KV_HEADS, nqb, hd, cols), BF16),
        scratch_shapes=[pltpu.VMEM((1, cols), F32), pltpu.VMEM((1, cols), F32), pltpu.VMEM((hd, cols), F32)] * 2
        + [pltpu.VMEM((FAR_GROUP * qblk, cols), F32), pltpu.VMEM((FAR_GROUP * qblk, cols), F32)],
        compiler_params=_cparams("parallel", "parallel", "parallel"),
        name="nsa_slc_win_attention",
    )(q2d, selt, oct, gates_blk, sk, svt, wk, wvt, bt, bfar)


def _cols_table(tab):
    h, nq, nk = tab.shape
    return jnp.transpose(tab.reshape(N_KV_HEADS, GQ, nq, nk), (0, 3, 1, 2)).reshape(N_KV_HEADS, nk, GQ * nq)


def _nsa_prompt(proj, kv, rel_bias, cmp_w1, cmp_b1, cmp_w2, cmp_pos):
    b, t = kv.shape[:2]
    hq = N_HEADS * HEAD_DIM
    gates = proj[:, hq:hq + 3 * N_HEADS]
    qblk = math.gcd(t, Q_BLOCK)
    assert qblk == Q_BLOCK and t % Q_BLOCK == 0 and WINDOW % Q_BLOCK == 0 and WINDOW - qblk + 1 >= REL_MAX_DIST
    nqb = t // qblk
    nch = t // CMP_STRIDE
    n_sel = t // SEL_BLOCK
    nsp = -(-n_sel // LANES) * LANES
    kv_hm = jnp.transpose(kv, (2, 0, 3, 1, 4))
    xc = kv_hm[:2].reshape(2, b * N_KV_HEADS, nch, CMP_STRIDE * HEAD_DIM)
    kvc = _compress(xc, cmp_w1, cmp_b1, cmp_w2, cmp_pos).reshape(2, b, N_KV_HEADS, nch, HEAD_DIM)
    vct = jnp.swapaxes(kvc[1], -1, -2)
    kvc = jnp.pad(kvc, ((0, 0), (0, 0), (0, 0), (CMP_PAD, CMP_PAD), (0, 0)))
    tbl = _bias_by_dist(rel_bias)
    bfar = _cols_table(jnp.broadcast_to(tbl[:, REL_MAX_DIST][:, None, None], (N_HEADS, qblk, 1)))
    bnear = _cols_table(tbl[:, _dist_tile(qblk, CMP_NEAR, CMP_STRIDE * CMP_PAD - CMP_BLOCK + 1, CMP_STRIDE)])
    qi = np.arange(qblk)[:, None]
    kj = np.arange(qblk)[None, :]
    far_tile = jnp.broadcast_to(tbl[:, REL_MAX_DIST][:, None, None], (N_HEADS, qblk, qblk))
    bt = jnp.stack([_cols_table(jnp.where(kj <= qi, tbl[:, _dist_tile(qblk, qblk, 0)], NEG)),
                    _cols_table(tbl[:, _dist_tile(qblk, qblk, qblk)]),
                    _cols_table(jnp.where(kj >= qi, far_tile, NEG))], axis=1)
    ov_np = _overlap_matrix(nch, nsp)
    ov = jnp.asarray(ov_np, BF16)
    ovt = jnp.asarray(ov_np[CMP_PAD:CMP_PAD + nch].T, BF16)
    gates_blk = jnp.transpose(gates.reshape(b, nqb, qblk, N_KV_HEADS, GQ, 3), (0, 3, 1, 5, 4, 2))
    gates_blk = gates_blk.reshape(b, N_KV_HEADS, nqb, 3, GQ * qblk)
    oct, selt = _cmp_attention(proj, b, t, kvc[0], kvc[1], vct, ov, ovt, bnear, bfar, nch, min(SEL_TOP, n_sel))
    kvb = kv_hm[2:].astype(BF16).reshape(4, b, N_KV_HEADS, nqb, qblk, HEAD_DIM)
    sk, wk = kvb[0], kvb[2]
    svt, wvt = jnp.swapaxes(kvb[1], -1, -2), jnp.swapaxes(kvb[3], -1, -2)
    o_t = _slc_win_attention(proj, b, t, selt, oct, gates_blk, sk, svt, wk, wvt, bt, bfar)
    o_t = o_t.reshape(b, N_KV_HEADS, nqb, HEAD_DIM, GQ, qblk)
    return jnp.transpose(o_t, (0, 2, 5, 1, 4, 3)).reshape(b, t, N_HEADS * HEAD_DIM)


CMP_TAP_PAIRS = CMP_STRIDE // 2
PAGE_CHUNKS = PAGE_SIZE // CMP_STRIDE


def _paged_compress_kernel(pt_ref, cache_ref, wp_ref, w1_ref, pos_ref, b1_ref, w2_ref, o_ref, buft, buf, carry, sem,
                           *, pg, npages, hid):
    g = pl.program_id(1)
    step = pl.program_id(0) * pl.num_programs(1) + g
    total = pl.num_programs(0) * pl.num_programs(1)
    m = pg * PAGE_CHUNKS
    hk = N_KV_HEADS * HEAD_DIM
    pairs = hk // LANES

    def page_copy(s, p, slot):
        return pltpu.make_async_copy(cache_ref.at[pt_ref[s * pg + p], pl.ds(0, 2)], buft.at[slot, p], sem.at[slot])

    @pl.when(step == 0)
    def _():
        for p in range(pg):
            page_copy(0, p, 0).start()

    for slot in range(2):
        @pl.when(jnp.logical_and(step + 1 < total, (step + 1) % 2 == slot))
        def _():
            for p in range(pg):
                page_copy(step + 1, p, slot).start()

    @pl.when(g == 0)
    def _():
        carry[...] = jnp.zeros_like(carry)

    for slot in range(2):
        @pl.when(step % 2 == slot)
        def _():
            for p in range(pg):
                page_copy(step, p, slot).wait()

    cur = step % 2

    def to_rows(p, carry_):
        r0 = pl.multiple_of(p * PAGE_SIZE, PAGE_SIZE)
        for comp in range(2):
            for hp in range(pairs):
                plane = buft[cur, p, comp, hp * LANES:(hp + 1) * LANES, :]
                buf[comp * pairs + hp, pl.ds(r0, PAGE_SIZE), :] = plane.T
        return carry_

    lax.fori_loop(0, pg, to_rows, 0)

    row0 = lax.broadcasted_iota(I32, (m, hid), 0) == 0
    for comp in range(2):
        pt = _dot(pos_ref[comp].astype(BF16), w1_ref[comp])
        for hp in range(N_KV_HEADS // 2):
            lg = (comp * hk + hp * 2 * HEAD_DIM) // LANES
            acc = jnp.zeros((m, 2 * CMP_RATIO * hid), F32)
            for u in range(CMP_TAP_PAIRS):
                taps = [buf[lg, pl.ds(2 * u + e, m, stride=CMP_STRIDE), :] for e in range(2)]
                acc = acc + _dot(jnp.concatenate(taps, axis=1).astype(BF16), wp_ref[comp, u])
            for e in range(2):
                kvh = 2 * hp + e
                base = e * CMP_RATIO * hid
                p0 = acc[:, base:base + hid] + pt[0:1, :hid]
                p1 = acc[:, base + hid:base + 2 * hid] + pt[1:2, hid:]
                prev0 = jnp.where(row0, carry[comp, kvh], pltpu.roll(p0, 1, axis=0))
                carry[comp, kvh] = p0[m - 1:m, :]
                pre = prev0 + p1 + b1_ref[comp]
                o_ref[comp, 0, kvh] = _dot(jax.nn.gelu(pre).astype(BF16), w2_ref[comp])


def _paged_compress(page_table, cache_t, cmp_w1, cmp_b1, cmp_w2, cmp_pos):
    bs, npages = page_table.shape
    hid = cmp_w1.shape[-1]
    kdim = CMP_STRIDE * HEAD_DIM
    pg = math.gcd(npages, 32)
    m = pg * PAGE_CHUNKS
    w1 = cmp_w1.reshape(2, CMP_RATIO, kdim, hid)
    w1cat = jnp.concatenate([w1[:, r] for r in range(CMP_RATIO)], axis=-1)
    n1 = CMP_RATIO * hid
    wt = w1cat.reshape(2, CMP_TAP_PAIRS, 2, HEAD_DIM, n1)
    zero = jnp.zeros_like(wt)
    wp = jnp.concatenate([jnp.concatenate([wt, zero], axis=-1), jnp.concatenate([zero, wt], axis=-1)], axis=3)
    wp = wp.reshape(2, CMP_TAP_PAIRS, 4 * HEAD_DIM, 2 * n1).astype(BF16)
    pos = jnp.pad(cmp_pos.reshape(2, CMP_RATIO, kdim), ((0, 0), (0, SUBLANES - CMP_RATIO), (0, 0)))
    nch = npages * PAGE_CHUNKS
    grid_spec = pltpu.PrefetchScalarGridSpec(
        num_scalar_prefetch=1,
        grid=(bs, npages // pg),
        in_specs=[pl.BlockSpec(memory_space=pl.ANY),
                  pl.BlockSpec(wp.shape, lambda b, g, pt: (0, 0, 0, 0)),
                  pl.BlockSpec((2, kdim, n1), lambda b, g, pt: (0, 0, 0)),
                  pl.BlockSpec((2, SUBLANES, kdim), lambda b, g, pt: (0, 0, 0)),
                  pl.BlockSpec((2, 1, hid), lambda b, g, pt: (0, 0, 0)),
                  pl.BlockSpec((2, hid, HEAD_DIM), lambda b, g, pt: (0, 0, 0))],
        out_specs=pl.BlockSpec((2, 1, N_KV_HEADS, m, HEAD_DIM), lambda b, g, pt: (0, b, 0, g, 0)),
        scratch_shapes=[pltpu.VMEM((2, pg, 2, N_KV_HEADS * HEAD_DIM, PAGE_SIZE), F32),
                        pltpu.VMEM((2 * N_KV_HEADS * HEAD_DIM // LANES, pg * PAGE_SIZE, LANES), F32),
                        pltpu.VMEM((2, N_KV_HEADS, 1, hid), F32), pltpu.SemaphoreType.DMA((2,))],
    )
    return pl.pallas_call(
        functools.partial(_paged_compress_kernel, pg=pg, npages=npages, hid=hid),
        grid_spec=grid_spec,
        out_shape=jax.ShapeDtypeStruct((2, bs, N_KV_HEADS, nch, HEAD_DIM), F32),
        compiler_params=_cparams("arbitrary", "arbitrary"),
        name="nsa_paged_compress",
    )(page_table.reshape(-1), cache_t, wp, w1cat.astype(BF16), pos, cmp_b1.reshape(2, 1, hid), cmp_w2.astype(BF16))


def _rows8(x):
    return jnp.concatenate([x, jnp.zeros((SUBLANES - x.shape[0], x.shape[1]), x.dtype)], axis=0)


def _sample_cmp_kernel(q_ref, kc_ref, vc_ref, ov_ref, bias_ref, oc_ref, idx_ref, *, nch, n_sel, n_past_blk, ntop):
    scale = HEAD_DIM ** -0.5
    nsp = ov_ref.shape[1]
    row = lax.broadcasted_iota(I32, (SUBLANES, nch), 0)
    lane = lax.broadcasted_iota(I32, (SUBLANES, nch), 1)
    mask = jnp.logical_and(lane >= 1, row < GQ)
    imp = jnp.zeros((SUBLANES, nsp), F32)
    row_s = lax.broadcasted_iota(I32, (SUBLANES, nsp), 0)
    for kvh in range(N_KV_HEADS):
        q8 = _rows8(q_ref[0, kvh * GQ:(kvh + 1) * GQ, :])
        s = jnp.where(mask, _dot3_nt(q8, kc_ref[0, kvh]) * scale + bias_ref[kvh], NEG)
        mx = jnp.max(s, axis=-1, keepdims=True)
        p = jnp.where(mask, jnp.exp(s - mx), 0.0)
        l = jnp.sum(p, axis=-1, keepdims=True)
        linv = 1.0 / jnp.where(l > 0, l, 1.0)
        o = _dot(p.astype(BF16), vc_ref[0, kvh].astype(BF16)) * linv
        oc_ref[0, kvh * GQ:(kvh + 1) * GQ, :] = o[0:GQ]
        p_hi, p_lo = _split(p * linv)
        contrib = _dot(p_hi, ov_ref[...]) + _dot(p_lo, ov_ref[...])
        imp = jnp.where(row_s == kvh, jnp.sum(contrib, axis=0, keepdims=True), imp)
    s_io = lax.broadcasted_iota(I32, (SUBLANES, nsp), 1)
    t = n_past_blk * SEL_BLOCK
    cur = t // SEL_BLOCK
    forced = jnp.logical_or(s_io == 0, jnp.logical_or(s_io == cur, s_io == cur - 1))
    valid = jnp.logical_and(s_io * SEL_BLOCK <= t, s_io < n_sel)
    score = jnp.where(valid, imp + jnp.where(forced, FORCE_SCORE, 0.0), -jnp.inf)
    lane_f = s_io.astype(F32)
    out_lane = lax.broadcasted_iota(I32, (SUBLANES, LANES), 1)
    picks = jnp.zeros((SUBLANES, LANES), F32)
    for it in range(ntop):
        mx = jnp.max(score, axis=-1, keepdims=True)
        idx = jnp.min(jnp.where(score == mx, lane_f, float(nsp)), axis=-1, keepdims=True)
        picks = jnp.where(out_lane == it, idx, picks)
        score = jnp.where(lane_f == idx, -jnp.inf, score)
    idx_ref[0] = picks.astype(I32)


def _sample_cmp_attention(q_hm, kc, vc, ov, bias, n_sel, n_past_blk, ntop):
    bs = q_hm.shape[0]
    nch = kc.shape[2]
    kvspec = pl.BlockSpec((1, N_KV_HEADS, nch, HEAD_DIM), lambda b: (b, 0, 0, 0))
    return pl.pallas_call(
        functools.partial(_sample_cmp_kernel, nch=nch, n_sel=n_sel, n_past_blk=n_past_blk, ntop=ntop),
        grid=(bs,),
        in_specs=[pl.BlockSpec((1, N_HEADS, HEAD_DIM), lambda b: (b, 0, 0)), kvspec, kvspec,
                  pl.BlockSpec(ov.shape, lambda b: (0, 0)), pl.BlockSpec(bias.shape, lambda b: (0, 0, 0))],
        out_specs=[pl.BlockSpec((1, N_HEADS, HEAD_DIM), lambda b: (b, 0, 0)),
                   pl.BlockSpec((1, SUBLANES, LANES), lambda b: (b, 0, 0))],
        out_shape=[jax.ShapeDtypeStruct(q_hm.shape, F32), jax.ShapeDtypeStruct((bs, SUBLANES, LANES), I32)],
        compiler_params=_cparams("parallel"),
        name="nsa_sample_cmp_attention",
    )(q_hm, kc, vc, ov, bias)


def _sample_slc_win_kernel(page_ref, half_ref, new_ref, d0_ref, var_ref,
                           q_ref, oc_ref, gate_ref, cache_ref, newt_ref, wint_ref, winnew_ref,
                           bslc_ref, bwin_ref, bwin_new_ref, o_ref, buf, sem, *, ntop):
    b = pl.program_id(0)
    scale = HEAD_DIM ** -0.5
    hk = N_KV_HEADS * HEAD_DIM
    nslots = N_KV_HEADS * ntop

    def plane_copies(slot):
        i = b * nslots + slot
        kvh = slot // ntop
        return [pltpu.make_async_copy(cache_ref.at[page_ref[i], 2 + e, kvh], buf.at[slot, e], sem) for e in range(2)]

    for slot in range(nslots):
        is_new = new_ref[b * nslots + slot] == 1

        @pl.when(jnp.logical_not(is_new))
        def _():
            for cp in plane_copies(slot):
                cp.start()

        @pl.when(is_new)
        def _():
            buf[slot] = newt_ref[0, :, slot // ntop]

    for slot in range(nslots):
        @pl.when(new_ref[b * nslots + slot] == 0)
        def _():
            for cp in plane_copies(slot):
                cp.wait()

    lane = lax.broadcasted_iota(I32, (SUBLANES, PAGE_SIZE), 1)
    gt = gate_ref[0]
    for kvh in range(N_KV_HEADS):
        hs = slice(kvh * GQ, (kvh + 1) * GQ)
        ks = slice(kvh * HEAD_DIM, (kvh + 1) * HEAD_DIM)
        vsl = slice(hk + kvh * HEAD_DIM, hk + (kvh + 1) * HEAD_DIM)
        q8f = _rows8(q_ref[0, hs, :])
        q8 = q8f.astype(BF16)
        ss, masks = [], []
        for j in range(ntop):
            slot = kvh * ntop + j
            i = b * nslots + slot
            s = _dot(q8, buf[slot, 0].astype(BF16)) * scale + bslc_ref[kvh, var_ref[i]]
            off = lane - half_ref[i] * SEL_BLOCK
            mask = jnp.logical_and(jnp.logical_and(off >= 0, off < SEL_BLOCK), off <= d0_ref[i])
            ss.append(jnp.where(mask, s, NEG))
            masks.append(mask)
        mx = ss[0].max(axis=-1, keepdims=True)
        for s in ss[1:]:
            mx = jnp.maximum(mx, s.max(axis=-1, keepdims=True))
        l = jnp.zeros((SUBLANES, 1), F32)
        acc = jnp.zeros((SUBLANES, HEAD_DIM), F32)
        for j in range(ntop):
            p = jnp.where(masks[j], jnp.exp(ss[j] - mx), 0.0)
            l = l + jnp.sum(p, axis=-1, keepdims=True)
            acc = acc + _dot_nt(p.astype(BF16), buf[kvh * ntop + j, 1].astype(BF16))
        o_s = acc / jnp.where(l > 0, l, 1.0)
        sw = _dot(q8, wint_ref[0, 0, kvh].astype(BF16)) * scale + bwin_ref[kvh]
        s_new = jnp.sum(q8f * winnew_ref[0, 0:1, ks], axis=-1, keepdims=True) * scale + bwin_new_ref[kvh]
        mw = jnp.maximum(sw.max(axis=-1, keepdims=True), s_new)
        pw = jnp.exp(sw - mw)
        p_new = jnp.exp(s_new - mw)
        lw = jnp.sum(pw, axis=-1, keepdims=True) + p_new
        o_w = (_dot_nt(pw.astype(BF16), wint_ref[0, 1, kvh].astype(BF16)) + p_new * winnew_ref[0, 0:1, vsl]) / lw
        g3 = gt[hs, :]
        o_ref[0, hs, :] = g3[:, 0:1] * oc_ref[0, hs, :] + g3[:, 1:2] * o_s[0:GQ] + g3[:, 2:3] * o_w[0:GQ]


def _sample_slc_win(picks, q_hm, oc, gates_h, cache_t, newt, wint, winnew, bslc, bwin, bwin_new, ntop):
    bs = q_hm.shape[0]
    hspec = pl.BlockSpec((1, N_HEADS, HEAD_DIM), lambda b, *_: (b, 0, 0))
    grid_spec = pltpu.PrefetchScalarGridSpec(
        num_scalar_prefetch=5,
        grid=(bs,),
        in_specs=[hspec, hspec, pl.BlockSpec((1, N_HEADS, 3), lambda b, *_: (b, 0, 0)),
                  pl.BlockSpec(memory_space=pl.ANY),
                  pl.BlockSpec((1,) + newt.shape[1:], lambda b, *_: (b, 0, 0, 0, 0)),
                  pl.BlockSpec((1,) + wint.shape[1:], lambda b, *_: (b, 0, 0, 0, 0)),
                  pl.BlockSpec((1,) + winnew.shape[1:], lambda b, *_: (b, 0, 0)),
                  pl.BlockSpec(bslc.shape, lambda b, *_: (0, 0, 0, 0)),
                  pl.BlockSpec(bwin.shape, lambda b, *_: (0, 0, 0)),
                  pl.BlockSpec(bwin_new.shape, lambda b, *_: (0, 0, 0))],
        out_specs=hspec,
        scratch_shapes=[pltpu.VMEM((N_KV_HEADS * ntop, 2, HEAD_DIM, PAGE_SIZE), F32), pltpu.SemaphoreType.DMA(())],
    )
    return pl.pallas_call(
        functools.partial(_sample_slc_win_kernel, ntop=ntop),
        grid_spec=grid_spec,
        out_shape=jax.ShapeDtypeStruct(q_hm.shape, F32),
        compiler_params=_cparams("arbitrary"),
        name="nsa_sample_slc_win",
    )(*picks, q_hm, oc, gates_h, cache_t, newt, wint, winnew, bslc, bwin, bwin_new)


def _nsa_sample(q, gates, kv_new, win_buf, cache, page_table, rel_bias, cmp_w1, cmp_b1, cmp_w2, cmp_pos):
    bs, npages = page_table.shape
    pool = cache.shape[0]
    hk = N_KV_HEADS * HEAD_DIM
    past = npages * PAGE_SIZE
    nch = past // CMP_STRIDE
    n_past_blk = past // SEL_BLOCK
    n_sel = n_past_blk + 1
    ntop = min(SEL_TOP, n_sel)
    nsp = -(-n_sel // LANES) * LANES
    wb = win_buf.shape[1]
    assert n_sel >= SEL_TOP and past >= wb
    cache_t = jnp.transpose(cache, (0, 2, 3, 4, 1))
    kvc = _paged_compress(page_table, cache_t.reshape(pool, 4, hk, PAGE_SIZE), cmp_w1, cmp_b1, cmp_w2, cmp_pos)
    tbl = _bias_by_dist(rel_bias)
    tok = np.arange(nch) - 1
    dist_c = np.clip(past - (tok * CMP_STRIDE + CMP_BLOCK - 1), 0, REL_MAX_DIST)
    bias_c = jnp.pad(tbl[:, dist_c].reshape(N_KV_HEADS, GQ, nch), ((0, 0), (0, SUBLANES - GQ), (0, 0)))
    ov = np.zeros((nch, nsp), np.float32)
    ov[1:] = _overlap_matrix(nch - 1, nsp)[CMP_PAD:-CMP_PAD]
    q_hm = q.reshape(bs, N_HEADS, HEAD_DIM)
    oc, idx = _sample_cmp_attention(q_hm, kvc[0], kvc[1], jnp.asarray(ov, BF16), bias_c, n_sel, n_past_blk, ntop)
    idx = idx[:, :N_KV_HEADS, :ntop]
    ip = jnp.minimum(idx, n_past_blk - 1)
    per_page = PAGE_SIZE // SEL_BLOCK
    page = jnp.take_along_axis(page_table, (ip // per_page).reshape(bs, -1), axis=1)
    is_new = (idx >= n_past_blk).astype(I32)
    half = jnp.where(is_new == 1, 0, ip % per_page)
    d0 = (n_past_blk - idx) * SEL_BLOCK
    case = jnp.clip(n_past_blk - idx, 0, 3)
    picks = [a.reshape(-1).astype(I32) for a in (page, half, is_new, d0, case * per_page + half)]
    lanes = np.arange(PAGE_SIZE)
    dist_s = np.stack([np.clip(cs * SEL_BLOCK - (lanes - hf * SEL_BLOCK), 0, REL_MAX_DIST)
                       for cs in range(4) for hf in range(per_page)])
    nvar = dist_s.shape[0]
    bslc = jnp.pad(jnp.transpose(tbl[:, dist_s].reshape(N_KV_HEADS, GQ, nvar, PAGE_SIZE), (0, 2, 1, 3)),
                   ((0, 0), (0, 0), (0, SUBLANES - GQ), (0, 0)))
    dist_w = np.clip(wb - np.arange(wb), 0, REL_MAX_DIST)
    bwin = jnp.pad(tbl[:, dist_w].reshape(N_KV_HEADS, GQ, wb), ((0, 0), (0, SUBLANES - GQ), (0, 0)))
    bwin_new = jnp.pad(tbl[:, 0].reshape(N_KV_HEADS, GQ, 1), ((0, 0), (0, SUBLANES - GQ), (0, 0)))
    newt = jnp.pad(kv_new[:, 2:4][..., None], ((0, 0),) * 4 + ((0, PAGE_SIZE - 1),))
    winnew = jnp.pad(kv_new[:, 4:6].reshape(bs, 1, 2 * hk), ((0, 0), (0, SUBLANES - 1), (0, 0)))
    wint = jnp.transpose(win_buf, (0, 2, 3, 4, 1))
    o = _sample_slc_win(picks, q_hm, oc, gates.reshape(bs, N_HEADS, 3), cache_t, newt, wint, winnew,
                        bslc, bwin, bwin_new, ntop)
    return o.reshape(bs, N_HEADS * HEAD_DIM)


def _mods(mod_l, nb, mr):
    return [m.reshape(nb, mr, -1) for m in jnp.split(mod_l, 6, axis=-1)]


def _s5_layer(x, nb, rb, seq, norm_g, mods, s5w, dsk, w_glu_bf16, h0r, h0i):
    sh1, sc1, g1 = mods[0], mods[1], mods[2]
    d = x.shape[-1]
    ar, ai, wbr, wbi, wcr, wci = s5w
    act, st_r, st_i = _s5_scan(x.reshape(nb, rb, d), norm_g.reshape(1, d), sh1, sc1, dsk.reshape(1, d),
                               wbr, wbi, ar, ai, wcr, wci, h0r, h0i, seq=seq)
    return _glu_residual(act.reshape(-1, d), x, g1, w_glu_bf16, rb), st_r, st_i


def kernel(x_prompt, x_sample, c_prompt, c_sample, state_s5, cache_nsa_kv, cache_win_kv, page_table, w_mod, b_mod, norm_g, s5_lambda_re, s5_lambda_im, s5_log_dt, s5_b_re, s5_b_im, s5_c_re, s5_c_im, s5_d, s5_w_glu, kv_mod_w, kv_mod_b, kv_norm_g, w_kv, cmp_w1, cmp_b1, cmp_w2, cmp_pos, nsa_w_in, nsa_w_out, rel_bias, moe_w_router, moe_b_router, moe_w_gu, moe_b_gu, moe_w_down, moe_b_down, final_norm_g):
    bp, lp, d = x_prompt.shape
    bs, ls, _ = x_sample.shape
    assert ls == 1, "the sample path advances exactly one token per sequence"
    depth = w_mod.shape[0]
    n_a = s5_lambda_re.shape[0]
    hq = N_HEADS * HEAD_DIM
    groups = d // S5_GROUP

    c_all = jnp.concatenate([c_prompt, c_sample], axis=0)
    mod_all = _adaln(c_all, w_mod, b_mod)
    kvmod_all = _adaln(c_all, kv_mod_w[None], kv_mod_b[None])[0]
    s5ws = [_s5_weights(s5_lambda_re[i], s5_lambda_im[i], s5_log_dt[i], s5_b_re[i], s5_b_im[i],
                        s5_c_re[i], s5_c_im[i]) for i in range(n_a)]
    wglus = [s5_w_glu[i].astype(BF16) for i in range(n_a)]
    w_kv_b = w_kv.astype(BF16)
    n_in = nsa_w_in.shape[-1]
    n_in_pad = -(-n_in // LANES) * LANES
    w_ins = [jnp.pad(nsa_w_in[j], ((0, 0), (0, n_in_pad - n_in))).astype(BF16) for j in range(depth - n_a)]
    w_outs = [nsa_w_out[j].astype(BF16) for j in range(depth - n_a)]

    def run(x3, sl, seq, h0_all, mixer):
        if seq:
            nb, rb = x3.shape[0], x3.shape[1]
            mr = 1
        else:
            nb, rb = 1, x3.shape[0]
            mr = rb
        x = x3.reshape(-1, d)
        states = []
        kv = None
        for i in range(depth):
            mods = _mods(mod_all[i, sl], nb, mr)
            if i < n_a:
                ns = s5ws[i][0].shape[-1]
                h0r = h0_all[i][..., 0].reshape(nb, -1, ns)
                h0i = h0_all[i][..., 1].reshape(nb, -1, ns)
                x, sr, si = _s5_layer(x, nb, rb, seq, norm_g[i, 0], mods, s5ws[i], s5_d[i], wglus[i], h0r, h0i)
                states.append(jnp.stack([sr.reshape(-1, groups, S5_STATE), si.reshape(-1, groups, S5_STATE)], axis=-1))
            else:
                j = i - n_a
                if kv is None:
                    kvsh, kvsc = [m.reshape(nb, mr, d) for m in jnp.split(kvmod_all[sl], 2, axis=-1)]
                    kv = _normmod_matmul(x, rb, kv_norm_g.reshape(1, d), kvsh, kvsc, w_kv_b)
                proj = _normmod_matmul(x, rb, norm_g[i, 0].reshape(1, d), mods[0], mods[1], w_ins[j], sig_from=hq)
                o = mixer(proj, kv)
                x = _matmul_residual(o, x, mods[2], w_outs[j], rb)
            x = _moe(x, rb, norm_g[i, 1].reshape(1, d), mods[3], mods[4], mods[5], i, moe_w_router[i], moe_b_router[i],
                     moe_w_gu, moe_b_gu, moe_w_down, moe_b_down)
        return _final_norm(x, final_norm_g, rb), jnp.stack(states), kv

    def prompt_mixer(proj, kv):
        o = _nsa_prompt(proj, kv.reshape(bp, lp, N_BRANCH_KV, N_KV_HEADS, HEAD_DIM), rel_bias,
                        cmp_w1, cmp_b1, cmp_w2, cmp_pos)
        return o.reshape(bp * lp, hq)

    def sample_mixer(proj, kv):
        return _nsa_sample(proj[:, :hq], proj[:, hq:n_in], kv.reshape(bs, N_BRANCH_KV, N_KV_HEADS, HEAD_DIM),
                           cache_win_kv, cache_nsa_kv, page_table, rel_bias, cmp_w1, cmp_b1, cmp_w2, cmp_pos)

    h0_p = jnp.zeros((n_a, bp, groups, S5_STATE, 2), F32)
    y_p, st_p, kv_p = run(x_prompt, slice(0, bp), True, h0_p, prompt_mixer)
    y_s, st_s, kv_s = run(x_sample.reshape(bs, d), slice(bp, bp + bs), False, state_s5, sample_mixer)
    kv_p = kv_p.reshape(bp, lp, N_BRANCH_KV, N_KV_HEADS, HEAD_DIM)
    kv_s = kv_s.reshape(bs, ls, N_BRANCH_KV, N_KV_HEADS, HEAD_DIM)
    win_p = kv_p[:, -min(WINDOW, lp):, 4:6]
    win_s = jnp.concatenate([cache_win_kv, kv_s[:, :, 4:6]], axis=1)[:, -cache_win_kv.shape[1]:]
    return (y_p.reshape(bp, lp, d), y_s.reshape(bs, ls, d), st_p, st_s, kv_p[:, :, :4], kv_s[:, :, :4], win_p, win_s)
```
